```python
import jax, jax.numpy as jnp
from jax import lax
import numpy as np

D_MODEL = 2048
BATCH = 4
SEQ = 2048
DEPTH = 2

EPS = 1e-6
D_MIX = D_MODEL
D_CONV = D_MIX // 4
CONV_WIDTH = 31
D_GLA = D_MIX // 2
GLA_HEADS = 4
GLA_DK = D_GLA // 2 // GLA_HEADS
GLA_DV = D_GLA // GLA_HEADS
GLA_KEY = GLA_HEADS * GLA_DK
GLA_RANK = 16
GLA_GATE_NORMALIZER = 16.0
GLA_CHUNK = 64
D_FOX = D_MIX - D_CONV - D_GLA
FOX_HEADS = 4
FOX_DH = D_FOX // FOX_HEADS
FOX_BLOCK = 128
IN_SPLITS = (D_CONV, D_CONV, GLA_KEY, GLA_KEY, D_GLA, D_GLA, GLA_RANK, D_FOX, D_FOX, D_FOX, FOX_HEADS)
D_IN = 2 * D_CONV + 2 * GLA_KEY + 2 * D_GLA + GLA_RANK + 3 * D_FOX + FOX_HEADS
N_GROUPS = 4
EXPERTS_PER_GROUP = 8
D_EXPERT = D_MODEL // 4
TOP_K_INNER = 2

kernel_name = 'hymba_style_conv_gla_fox_hmoe'


def rmsnorm(x, g):
    xf = x.astype(jnp.float32)
    y = xf * lax.rsqrt(jnp.mean(xf * xf, axis=-1, keepdims=True) + EPS)
    return (y * g.astype(jnp.float32)).astype(x.dtype)


def conv_mixer(a, gate, w, b, ln_g, ln_b):
    u = a * jax.nn.sigmoid(gate)
    y = lax.conv_general_dilated(u, w[:, None, :], window_strides=(1,), padding=[(CONV_WIDTH - 1, 0)],
                                 dimension_numbers=('NWC', 'WIO', 'NWC'), feature_group_count=D_CONV) + b
    yf = y.astype(jnp.float32)
    mu = jnp.mean(yf, axis=-1, keepdims=True)
    var = jnp.mean(jnp.square(yf - mu), axis=-1, keepdims=True)
    yn = (yf - mu) * lax.rsqrt(var + EPS) * ln_g.astype(jnp.float32) + ln_b.astype(jnp.float32)
    return jax.nn.silu(yn).astype(a.dtype)


def gla_mixer(q, k, v, g, a_low, w2, b2, norm_g):
    B, S, _ = q.shape
    N = S // GLA_CHUNK

    def heads(t, d):
        return t.astype(jnp.float32).reshape(B, N, GLA_CHUNK, GLA_HEADS, d).transpose(0, 3, 1, 2, 4)

    log_a = jax.nn.log_sigmoid((a_low @ w2 + b2).astype(jnp.float32)) / GLA_GATE_NORMALIZER
    qh = heads(q, GLA_DK) * (GLA_DK ** -0.5)
    kh = heads(k, GLA_DK)
    vh = heads(v, GLA_DV)
    cum = jnp.cumsum(heads(log_a, GLA_DK), axis=3)
    last = cum[:, :, :, -1:, :]
    q_dec = qh * jnp.exp(cum)
    k_inv = kh * jnp.exp(-cum)
    k_end = kh * jnp.exp(last - cum)
    causal = jnp.tril(jnp.ones((GLA_CHUNK, GLA_CHUNK), dtype=bool))
    att = jnp.where(causal, jnp.einsum('bhncd,bhnsd->bhncs', q_dec, k_inv), 0.0)
    o_intra = jnp.einsum('bhncs,bhnse->bhnce', att, vh)
    kv = jnp.einsum('bhncd,bhnce->bhnde', k_end, vh)
    decay = jnp.exp(last[:, :, :, 0, :])

    def step(state, inp):
        dec, kv_n = inp
        return dec[..., None] * state + kv_n, state

    init = jnp.zeros((B, GLA_HEADS, GLA_DK, GLA_DV), jnp.float32)
    _, s_prev = lax.scan(step, init, (jnp.moveaxis(decay, 2, 0), jnp.moveaxis(kv, 2, 0)))
    o = o_intra + jnp.einsum('bhncd,nbhde->bhnce', q_dec, s_prev)
    o = o * lax.rsqrt(jnp.mean(o * o, axis=-1, keepdims=True) + EPS) * norm_g.astype(jnp.float32)
    o = o.transpose(0, 2, 3, 1, 4).reshape(B, S, D_GLA)
    return (o * jax.nn.silu(g.astype(jnp.float32))).astype(q.dtype)


def fox_mixer(q, k, v, f_logit, f_b):
    B, S, _ = q.shape

    def heads(t):
        return t.astype(jnp.float32).reshape(B, S, FOX_HEADS, FOX_DH).transpose(0, 2, 1, 3)

    qh = heads(q) * (FOX_DH ** -0.5)
    kh = heads(k)
    vh = heads(v)
    log_f = jax.nn.log_sigmoid(f_logit.astype(jnp.float32) + f_b.astype(jnp.float32))
    F = jnp.cumsum(log_f, axis=1).transpose(0, 2, 1)
    within = jnp.tril(jnp.ones((FOX_BLOCK, FOX_BLOCK), dtype=bool))
    outs = []
    for i in range(S // FOX_BLOCK):
        lo, hi = i * FOX_BLOCK, (i + 1) * FOX_BLOCK
        s = jnp.einsum('bhqd,bhkd->bhqk', qh[:, :, lo:hi], kh[:, :, :hi])
        s = s + F[:, :, lo:hi, None] - F[:, :, None, :hi]
        mask = jnp.concatenate([jnp.ones((FOX_BLOCK, lo), dtype=bool), within], axis=1)
        p = jax.nn.softmax(jnp.where(mask, s, -jnp.inf), axis=-1)
        outs.append(jnp.einsum('bhqk,bhkd->bhqd', p, vh[:, :, :hi]))
    o = jnp.concatenate(outs, axis=2)
    return o.transpose(0, 2, 1, 3).reshape(B, S, D_FOX).astype(q.dtype)


def hier_moe(x, wg, bg, we, be, w_gate, w_up, w_down):
    B, S, D = x.shape
    t = x.reshape(B * S, D)
    g_logits = (t @ wg + bg).astype(jnp.float32)
    g_prob = jax.nn.softmax(g_logits, axis=-1)
    g_top_v, g_top_i = lax.top_k(g_prob, 1)
    grp = g_top_i[:, 0]
    e_logits = (jnp.einsum('td,gde->tge', t, we) + be).astype(jnp.float32)
    e_sel = jnp.take_along_axis(e_logits, grp[:, None, None], axis=1)[:, 0]
    top_v, top_i = lax.top_k(e_sel, TOP_K_INNER)
    gate = g_top_v * jax.nn.softmax(top_v, axis=-1)
    e_w = jnp.sum(jax.nn.one_hot(top_i, EXPERTS_PER_GROUP, dtype=jnp.float32) * gate[:, :, None], axis=1)
    combine = (jax.nn.one_hot(grp, N_GROUPS, dtype=jnp.float32)[:, :, None] * e_w[:, None, :]).astype(t.dtype)
    y = jnp.zeros_like(t)
    for gi in range(N_GROUPS):
        h = jax.nn.silu(jnp.einsum('td,edf->tef', t, w_gate[gi])) * jnp.einsum('td,edf->tef', t, w_up[gi])
        y = y + jnp.einsum('tef,efd->td', h * combine[:, gi, :, None], w_down[gi])
    return y.reshape(B, S, D)


def setup_inputs(seed: int = 0) -> dict:
    key = jax.random.key(seed)
    ks = jax.random.split(key, 24)
    f32 = jnp.float32
    nrm = lambda k, shape, s: jax.random.normal(k, shape, f32) * s
    L = DEPTH
    return {
        'x': nrm(ks[0], (BATCH, SEQ, D_MODEL), 1.0),
        'norm1_g': 1.0 + nrm(ks[1], (L, D_MODEL), 0.02),
        'w_in': nrm(ks[2], (L, D_MODEL, D_IN), D_MODEL ** -0.5),
        'conv_w': nrm(ks[3], (L, CONV_WIDTH, D_CONV), CONV_WIDTH ** -0.5),
        'conv_b': nrm(ks[4], (L, D_CONV), 0.01),
        'conv_ln_g': 1.0 + nrm(ks[5], (L, D_CONV), 0.02),
        'conv_ln_b': nrm(ks[6], (L, D_CONV), 0.01),
        'gla_w2': nrm(ks[7], (L, GLA_RANK, GLA_KEY), GLA_RANK ** -0.5),
        'gla_b2': nrm(ks[8], (L, GLA_KEY), 0.01),
        'gla_norm_g': 1.0 + nrm(ks[9], (L, GLA_DV), 0.02),
        'fox_f_b': 2.0 + nrm(ks[10], (L, FOX_HEADS), 0.5),
        'w_out': nrm(ks[11], (L, D_MIX, D_MODEL), D_MIX ** -0.5),
        'norm2_g': 1.0 + nrm(ks[12], (L, D_MODEL), 0.02),
        'router_group_w': nrm(ks[13], (L, D_MODEL, N_GROUPS), D_MODEL ** -0.5),
        'router_group_b': nrm(ks[14], (L, N_GROUPS), 0.01),
        'router_expert_w': nrm(ks[15], (L, N_GROUPS, D_MODEL, EXPERTS_PER_GROUP), D_MODEL ** -0.5),
        'router_expert_b': nrm(ks[16], (L, N_GROUPS, EXPERTS_PER_GROUP), 0.01),
        'ffn_w_gate': nrm(ks[17], (L, N_GROUPS, EXPERTS_PER_GROUP, D_MODEL, D_EXPERT), D_MODEL ** -0.5),
        'ffn_w_up': nrm(ks[18], (L, N_GROUPS, EXPERTS_PER_GROUP, D_MODEL, D_EXPERT), D_MODEL ** -0.5),
        'ffn_w_down': nrm(ks[19], (L, N_GROUPS, EXPERTS_PER_GROUP, D_EXPERT, D_MODEL), D_EXPERT ** -0.5),
        'final_norm_g': 1.0 + nrm(ks[20], (D_MODEL,), 0.02),
    }


def reference(x, norm1_g, w_in, conv_w, conv_b, conv_ln_g, conv_ln_b, gla_w2, gla_b2, gla_norm_g, fox_f_b,
              w_out, norm2_g, router_group_w, router_group_b, router_expert_w, router_expert_b,
              ffn_w_gate, ffn_w_up, ffn_w_down, final_norm_g):
    split_points = [int(s) for s in np.cumsum(IN_SPLITS)[:-1]]
    for l in range(DEPTH):
        h = rmsnorm(x, norm1_g[l])
        (c_a, c_g, g_q, g_k, g_v, g_g, g_low, f_q, f_k, f_v, f_f) = jnp.split(h @ w_in[l], split_points, axis=-1)
        y_conv = conv_mixer(c_a, c_g, conv_w[l], conv_b[l], conv_ln_g[l], conv_ln_b[l])
        y_gla = gla_mixer(g_q, g_k, g_v, g_g, g_low, gla_w2[l], gla_b2[l], gla_norm_g[l])
        y_fox = fox_mixer(f_q, f_k, f_v, f_f, fox_f_b[l])
        mix = jnp.concatenate([y_conv, y_gla, y_fox], axis=-1)
        x = x + mix @ w_out[l]
        x = x + hier_moe(rmsnorm(x, norm2_g[l]), router_group_w[l], router_group_b[l], router_expert_w[l],
                         router_expert_b[l], ffn_w_gate[l], ffn_w_up[l], ffn_w_down[l])
    return rmsnorm(x, final_norm_g)
```

```python
import functools

import jax
import jax.numpy as jnp
from jax import lax
from jax.experimental import pallas as pl
from jax.experimental.pallas import tpu as pltpu

F32 = jnp.float32
BF16 = jnp.bfloat16
I32 = jnp.int32

D_MODEL = 2048
EPS = 1e-6
D_CONV = 512
CONV_WIDTH = 31
D_GLA = 1024
GLA_HEADS = 4
GLA_DK = 128
GLA_DV = 256
GLA_KEY = GLA_HEADS * GLA_DK
GLA_RANK = 16
GLA_GATE_NORMALIZER = 16.0
GLA_CHUNK = 64
D_FOX = 512
FOX_HEADS = 4
FOX_DH = 128
N_GROUPS = 4
EXPERTS_PER_GROUP = 8
N_EXPERTS = N_GROUPS * EXPERTS_PER_GROUP
D_EXPERT = 512

LANES = 128
SUBLANES = 8
D_MAIN = 2 * D_CONV + 2 * GLA_KEY + 2 * D_GLA
D_PROJ = D_MAIN + 3 * D_FOX
SMALL_FOX_LANE = GLA_RANK
ROUTER_EXPERT_LANE = N_GROUPS

VMEM_LIMIT = 56 * 1024 * 1024

ROW_TILE = 256
MM_TM = 1024
MM_TN = 512
GLA_TS = 256
FOX_TQ = 256
FOX_TK = 256
CONV_RC = 64
ROUTE_TR = 512
MOE_TM = 256


def _cparams(sem):
    return pltpu.CompilerParams(dimension_semantics=sem, vmem_limit_bytes=VMEM_LIMIT)


def _split_bf16(x):
    hi = x.astype(BF16)
    lo = (x - hi.astype(F32)).astype(BF16)
    return hi, lo


def _dot(a, b):
    return jnp.dot(a, b, preferred_element_type=F32)


def _dot3(a, b_hi, b_lo):
    a_hi, a_lo = _split_bf16(a)
    return _dot(a_hi, b_hi) + _dot(a_lo, b_hi) + _dot(a_hi, b_lo)


def _sigmoid(x):
    return 1.0 / (1.0 + jnp.exp(-x))


def _log_sigmoid(x):
    return jnp.minimum(x, 0.0) - jnp.log(1.0 + jnp.exp(-jnp.abs(x)))


def _norm_body(*refs, combine, project):
    it = iter(refs)
    x_ref = next(it)
    if combine:
        ya_ref, yb_ref, gates_ref = next(it), next(it), next(it)
    g_ref = next(it)
    if project:
        ws_hi_ref, ws_lo_ref = next(it), next(it)
    if combine:
        xo_ref = next(it)
    h_ref = next(it)
    if project:
        small_ref = next(it)

    x = x_ref[...]
    if combine:
        gates = gates_ref[...]
        x = x + gates[:, 0:1] * ya_ref[...] + gates[:, 1:2] * yb_ref[...]
        xo_ref[...] = x
    y = x * lax.rsqrt(jnp.mean(x * x, axis=-1, keepdims=True) + EPS) * g_ref[...]
    h_ref[...] = y.astype(h_ref.dtype)
    if project:
        small_ref[...] = _dot3(y, ws_hi_ref[...], ws_lo_ref[...])


def _norm_call(x, g_row, *, moe=None, small_w=None, out_dtype=BF16):
    T = x.shape[0]
    tm = ROW_TILE
    combine = moe is not None
    project = small_w is not None
    row_spec = pl.BlockSpec((tm, D_MODEL), lambda i: (i, 0))
    lane_spec = pl.BlockSpec((tm, LANES), lambda i: (i, 0))
    const = lambda shape: pl.BlockSpec(shape, lambda i: (0, 0))
    nblk = T // tm
    ins, in_specs = [x], [row_spec]
    if combine:
        y2, gates = moe
        ins += [y2, y2, gates]
        in_specs += [row_spec, pl.BlockSpec((tm, D_MODEL), lambda i: (i + nblk, 0)), lane_spec]
    ins.append(g_row)
    in_specs.append(const((1, D_MODEL)))
    if project:
        ins += list(small_w)
        in_specs += [const((D_MODEL, LANES)), const((D_MODEL, LANES))]
    out_shape, out_specs = [], []
    if combine:
        out_shape.append(jax.ShapeDtypeStruct((T, D_MODEL), F32))
        out_specs.append(row_spec)
    out_shape.append(jax.ShapeDtypeStruct((T, D_MODEL), out_dtype))
    out_specs.append(row_spec)
    if project:
        out_shape.append(jax.ShapeDtypeStruct((T, LANES), F32))
        out_specs.append(lane_spec)
    return pl.pallas_call(
        functools.partial(_norm_body, combine=combine, project=project),
        grid=(nblk,),
        in_specs=in_specs,
        out_specs=out_specs,
        out_shape=out_shape,
        compiler_params=_cparams(("parallel",)),
        name="norm",
    )(*ins)


def _matmul_body(h_ref, w_ref, o_ref):
    o_ref[...] = _dot(h_ref[...], w_ref[...]).astype(o_ref.dtype)


def _inproj_call(h, w):
    T = h.shape[0]
    tm = min(MM_TM, T)
    return pl.pallas_call(
        _matmul_body,
        grid=(T // tm, D_PROJ // MM_TN),
        in_specs=[pl.BlockSpec((tm, D_MODEL), lambda i, j: (i, 0)),
                  pl.BlockSpec((D_MODEL, MM_TN), lambda i, j: (0, j))],
        out_specs=pl.BlockSpec((tm, MM_TN), lambda i, j: (i, j)),
        out_shape=jax.ShapeDtypeStruct((T, D_PROJ), BF16),
        compiler_params=_cparams(("parallel", "parallel")),
        name="inproj",
    )(h, w)


CONV_PAD = 32


def _conv_body(a_ref, g_ref, w_ref, b_ref, lng_ref, lnb_ref, o_ref, u_ref):
    S = a_ref.shape[0]
    u_ref[0:CONV_PAD, :] = jnp.zeros((CONV_PAD, D_CONV), F32)
    u_ref[CONV_PAD:CONV_PAD + S, :] = a_ref[...].astype(F32) * _sigmoid(g_ref[...].astype(F32))
    bias = b_ref[...]
    lng = lng_ref[...]
    lnb = lnb_ref[...]
    first = CONV_PAD - (CONV_WIDTH - 1)

    def chunk(c, carry):
        r0 = pl.multiple_of(c * CONV_RC, CONV_RC)
        acc = jnp.broadcast_to(bias, (CONV_RC, D_CONV))
        win = u_ref[pl.ds(r0, CONV_RC + CONV_PAD), :]
        for j in range(CONV_WIDTH):
            acc = acc + w_ref[j:j + 1, :] * win[first + j:first + j + CONV_RC, :]
        mu = jnp.mean(acc, axis=-1, keepdims=True)
        d = acc - mu
        var = jnp.mean(d * d, axis=-1, keepdims=True)
        yn = d * lax.rsqrt(var + EPS) * lng + lnb
        o_ref[pl.ds(r0, CONV_RC), :] = (yn * _sigmoid(yn)).astype(o_ref.dtype)
        return carry

    lax.fori_loop(0, S // CONV_RC, chunk, 0)


def _conv_call(proj, w_pad, b_row, lng_row, lnb_row, B, S):
    T = B * S
    const = lambda shape: pl.BlockSpec(shape, lambda b: (0, 0))
    return pl.pallas_call(
        _conv_body,
        grid=(B,),
        in_specs=[pl.BlockSpec((S, D_CONV), lambda b: (b, 0)),
                  pl.BlockSpec((S, D_CONV), lambda b: (b, 1)),
                  const((CONV_PAD, D_CONV)), const((1, D_CONV)), const((1, D_CONV)), const((1, D_CONV))],
        out_specs=pl.BlockSpec((S, D_CONV), lambda b: (b, 0)),
        out_shape=jax.ShapeDtypeStruct((T, D_CONV), BF16),
        scratch_shapes=[pltpu.VMEM((CONV_PAD + S, D_CONV), F32)],
        compiler_params=_cparams(("parallel",)),
        name="conv_mixer",
    )(proj, proj, w_pad, b_row, lng_row, lnb_row)


def _gla_body(q_ref, k_ref, v_ref, g_ref, low_ref, w2hi_ref, w2lo_ref, b2_ref, ng_ref, o_ref, st_ref):
    ts = q_ref.shape[0]
    nchunk = ts // GLA_CHUNK

    @pl.when(pl.program_id(1) == 0)
    def _():
        st_ref[...] = jnp.zeros(st_ref.shape, F32)

    la = _log_sigmoid(_dot3(low_ref[...], w2hi_ref[...], w2lo_ref[...]) + b2_ref[...]) * (1.0 / GLA_GATE_NORMALIZER)
    r = lax.broadcasted_iota(I32, (2 * ts, ts), 0)
    c = lax.broadcasted_iota(I32, (2 * ts, ts), 1)
    rr = jnp.where(r >= ts, r - ts, r)
    same_chunk = (rr // GLA_CHUNK) == (c // GLA_CHUNK)
    sel = jnp.where(same_chunk & ((r >= ts) | (c <= rr)), 1.0, 0.0).astype(BF16)
    la_hi, la_lo = _split_bf16(la)
    sums = _dot(sel, la_hi) + _dot(sel, la_lo)
    cum = sums[0:ts, :]
    last = sums[ts:2 * ts, :]
    e_q = jnp.exp(cum)
    e_inv = jnp.exp(-cum)
    e_end = jnp.exp(last - cum)
    e_last = jnp.exp(last)

    qr = lax.broadcasted_iota(I32, (ts, ts), 0)
    qc = lax.broadcasted_iota(I32, (ts, ts), 1)
    att_mask = ((qr // GLA_CHUNK) == (qc // GLA_CHUNK)) & (qc <= qr)
    ng = ng_ref[...]

    for h in range(GLA_HEADS):
        ks = slice(h * GLA_DK, (h + 1) * GLA_DK)
        vs = slice(h * GLA_DV, (h + 1) * GLA_DV)
        qh = q_ref[:, ks].astype(F32) * (GLA_DK ** -0.5)
        kh = k_ref[:, ks].astype(F32)
        vh = v_ref[:, vs]
        q_dec = (qh * e_q[:, ks]).astype(BF16)
        k_inv = (kh * e_inv[:, ks]).astype(BF16)
        k_end = (kh * e_end[:, ks]).astype(BF16)
        att = lax.dot_general(q_dec, k_inv, (((1,), (1,)), ((), ())), preferred_element_type=F32)
        att = jnp.where(att_mask, att, 0.0).astype(BF16)
        o_intra = _dot(att, vh)
        state = st_ref[h]
        outs = []
        for n in range(nchunk):
            rs = slice(n * GLA_CHUNK, (n + 1) * GLA_CHUNK)
            inter = lax.dot_general(q_dec[rs], state.astype(BF16), (((1,), (1,)), ((), ())),
                                    preferred_element_type=F32)
            outs.append(o_intra[rs] + inter)
            kv_t = lax.dot_general(vh[rs], k_end[rs], (((0,), (0,)), ((), ())), preferred_element_type=F32)
            state = state * e_last[n * GLA_CHUNK:n * GLA_CHUNK + 1, ks] + kv_t
        st_ref[h] = state
        o = jnp.concatenate(outs, axis=0)
        o = o * lax.rsqrt(jnp.mean(o * o, axis=-1, keepdims=True) + EPS) * ng
        gate = g_ref[:, vs].astype(F32)
        o_ref[:, vs] = (o * (gate * _sigmoid(gate))).astype(o_ref.dtype)


def _gla_call(proj, small, w2_hi, w2_lo, b2_row, ng_row, B, S):
    T = B * S
    ts = GLA_TS
    nst = S // ts
    row = lambda b, s: b * nst + s
    const = lambda shape: pl.BlockSpec(shape, lambda b, s: (0, 0))
    return pl.pallas_call(
        _gla_body,
        grid=(B, nst),
        in_specs=[pl.BlockSpec((ts, GLA_KEY), lambda b, s: (row(b, s), 2)),
                  pl.BlockSpec((ts, GLA_KEY), lambda b, s: (row(b, s), 3)),
                  pl.BlockSpec((ts, D_GLA), lambda b, s: (row(b, s), 2)),
                  pl.BlockSpec((ts, D_GLA), lambda b, s: (row(b, s), 3)),
                  pl.BlockSpec((ts, LANES), lambda b, s: (row(b, s), 0)),
                  const((LANES, GLA_KEY)), const((LANES, GLA_KEY)), const((1, GLA_KEY)), const((1, GLA_DV))],
        out_specs=pl.BlockSpec((ts, D_GLA), lambda b, s: (row(b, s), 0)),
        out_shape=jax.ShapeDtypeStruct((T, D_GLA), BF16),
        scratch_shapes=[pltpu.VMEM((GLA_HEADS, GLA_DV, GLA_DK), F32)],
        compiler_params=_cparams(("parallel", "arbitrary")),
        name="gla_mixer",
    )(proj, proj, proj, proj, small, w2_hi, w2_lo, b2_row, ng_row)


FGATE_BLK = 256


def _fgate_body(small_ref, fb_ref, fcol_ref, frow_ref):
    S = small_ref.shape[0]
    r = lax.broadcasted_iota(I32, (FGATE_BLK, FGATE_BLK), 0)
    c = lax.broadcasted_iota(I32, (FGATE_BLK, FGATE_BLK), 1)
    tri = jnp.where(c <= r, 1.0, 0.0).astype(BF16)
    carry = jnp.zeros((1, LANES), F32)
    for n in range(S // FGATE_BLK):
        rs = slice(n * FGATE_BLK, (n + 1) * FGATE_BLK)
        lf = _log_sigmoid(small_ref[rs, :] + fb_ref[...])
        p0 = lf.astype(BF16)
        r1 = lf - p0.astype(F32)
        p1 = r1.astype(BF16)
        p2 = (r1 - p1.astype(F32)).astype(BF16)
        blk = _dot(tri, p0) + _dot(tri, p1) + _dot(tri, p2) + carry
        fcol_ref[rs, :] = blk
        carry = blk[FGATE_BLK - 1:FGATE_BLK, :]
    ft = fcol_ref[...].T
    for h in range(FOX_HEADS):
        frow_ref[0, h] = ft[SMALL_FOX_LANE + h:SMALL_FOX_LANE + h + 1, :]


def _fgate_call(small, fb_row, B, S):
    T = B * S
    return pl.pallas_call(
        _fgate_body,
        grid=(B,),
        in_specs=[pl.BlockSpec((S, LANES), lambda b: (b, 0)),
                  pl.BlockSpec((1, LANES), lambda b: (0, 0))],
        out_specs=[pl.BlockSpec((S, LANES), lambda b: (b, 0)),
                   pl.BlockSpec((1, FOX_HEADS, 1, S), lambda b: (b, 0, 0, 0))],
        out_shape=[jax.ShapeDtypeStruct((T, LANES), F32),
                   jax.ShapeDtypeStruct((B, FOX_HEADS, 1, S), F32)],
        compiler_params=_cparams(("parallel",)),
        name="fox_gate",
    )(small, fb_row)


def _fox_body(q_ref, k_ref, v_ref, fcol_ref, frow_ref, o_ref):
    tq = q_ref.shape[0]
    tk = FOX_TK
    i = pl.program_id(1)
    row = lax.broadcasted_iota(I32, (tq, tk), 0)
    col = lax.broadcasted_iota(I32, (tq, tk), 1)
    causal = col <= row
    fcol = fcol_ref[...]

    for h in range(FOX_HEADS):
        hs = slice(h * FOX_DH, (h + 1) * FOX_DH)
        qh = (q_ref[:, hs].astype(F32) * (FOX_DH ** -0.5)).astype(BF16)
        f_t = fcol[:, SMALL_FOX_LANE + h:SMALL_FOX_LANE + h + 1]

        def scores(j):
            k0 = pl.multiple_of(j * tk, tk)
            kt = k_ref[pl.ds(k0, tk), hs]
            vt = v_ref[pl.ds(k0, tk), hs]
            f_s = frow_ref[0, h, :, pl.ds(k0, tk)]
            s = lax.dot_general(qh, kt, (((1,), (1,)), ((), ())), preferred_element_type=F32)
            return s + f_t - f_s, vt

        s, vt = scores(i)
        s = jnp.where(causal, s, -jnp.inf)
        m = jnp.max(s, axis=-1, keepdims=True)
        p = jnp.exp(s - m)
        l = jnp.sum(p, axis=-1, keepdims=True)
        acc = _dot(p.astype(BF16), vt)

        def step(j, carry):
            m, l, acc = carry
            s, vt = scores(j)
            m_new = jnp.maximum(m, jnp.max(s, axis=-1, keepdims=True))
            alpha = jnp.exp(m - m_new)
            p = jnp.exp(s - m_new)
            l = alpha * l + jnp.sum(p, axis=-1, keepdims=True)
            acc = alpha * acc + _dot(p.astype(BF16), vt)
            return m_new, l, acc

        m, l, acc = lax.fori_loop(0, i, step, (m, l, acc))
        o_ref[:, hs] = (acc / l).astype(o_ref.dtype)


def _fox_call(proj, fcol, frow, B, S):
    T = B * S
    tq = FOX_TQ
    nq = S // tq
    col0 = D_MAIN // D_FOX
    return pl.pallas_call(
        _fox_body,
        grid=(B, nq),
        in_specs=[pl.BlockSpec((tq, D_FOX), lambda b, i: (b * nq + i, col0)),
                  pl.BlockSpec((S, D_FOX), lambda b, i: (b, col0 + 1)),
                  pl.BlockSpec((S, D_FOX), lambda b, i: (b, col0 + 2)),
                  pl.BlockSpec((tq, LANES), lambda b, i: (b * nq + i, 0)),
                  pl.BlockSpec((1, FOX_HEADS, 1, S), lambda b, i: (b, 0, 0, 0))],
        out_specs=pl.BlockSpec((tq, D_FOX), lambda b, i: (b * nq + i, 0)),
        out_shape=jax.ShapeDtypeStruct((T, D_FOX), BF16),
        compiler_params=_cparams(("parallel", "parallel")),
        name="fox_mixer",
    )(proj, proj, proj, fcol, frow)


def _outproj_body(yc_ref, yg_ref, yf_ref, x_ref, w_ref, g_ref, wr_hi_ref, wr_lo_ref, br_ref,
                  xo_ref, h_ref, lg_ref):
    acc = x_ref[...]
    acc = acc + _dot(yc_ref[...], w_ref[0:D_CONV, :])
    acc = acc + _dot(yg_ref[...], w_ref[D_CONV:D_CONV + D_GLA, :])
    acc = acc + _dot(yf_ref[...], w_ref[D_CONV + D_GLA:D_MODEL, :])
    xo_ref[...] = acc
    hn = acc * lax.rsqrt(jnp.mean(acc * acc, axis=-1, keepdims=True) + EPS) * g_ref[...]
    h_ref[...] = hn
    lg_ref[...] = _dot3(hn, wr_hi_ref[...], wr_lo_ref[...]) + br_ref[...]


def _outproj_call(yc, yg, yf, x, w, g_row, wr_hi, wr_lo, br_row):
    T = x.shape[0]
    tm = ROW_TILE
    const = lambda shape: pl.BlockSpec(shape, lambda i: (0, 0))
    rows = lambda width: pl.BlockSpec((tm, width), lambda i: (i, 0))
    return pl.pallas_call(
        _outproj_body,
        grid=(T // tm,),
        in_specs=[rows(D_CONV), rows(D_GLA), rows(D_FOX), rows(D_MODEL),
                  const((D_MODEL, D_MODEL)), const((1, D_MODEL)),
                  const((D_MODEL, LANES)), const((D_MODEL, LANES)), const((1, LANES))],
        out_specs=[rows(D_MODEL), rows(D_MODEL), rows(LANES)],
        out_shape=[jax.ShapeDtypeStruct((T, D_MODEL), F32),
                   jax.ShapeDtypeStruct((T, D_MODEL), F32),
                   jax.ShapeDtypeStruct((T, LANES), F32)],
        compiler_params=_cparams(("parallel",)),
        name="outproj",
    )(yc, yg, yf, x, w, g_row, wr_hi, wr_lo, br_row)


def _router_body(lg_ref, ri_ref, rf_ref, cnt_ref, carry_ref):
    tr = lg_ref.shape[0]

    @pl.when(pl.program_id(0) == 0)
    def _():
        carry_ref[...] = jnp.zeros(carry_ref.shape, F32)

    lg = lg_ref[...]
    lane = lax.broadcasted_iota(I32, (tr, LANES), 1).astype(F32)
    big = float(LANES)
    neg = -jnp.inf

    is_g = lane < N_GROUPS
    gl = jnp.where(is_g, lg, neg)
    gmax = jnp.max(gl, axis=-1, keepdims=True)
    gexp = jnp.where(is_g, jnp.exp(lg - gmax), 0.0)
    gprob = gexp / jnp.sum(gexp, axis=-1, keepdims=True)
    gtop = jnp.max(gprob, axis=-1, keepdims=True)
    grp = jnp.min(jnp.where(is_g & (gprob == gtop), lane, big), axis=-1, keepdims=True)

    lo = ROUTER_EXPERT_LANE + grp * EXPERTS_PER_GROUP
    in_grp = (lane >= lo) & (lane < lo + EXPERTS_PER_GROUP)
    el = jnp.where(in_grp, lg, neg)
    v1 = jnp.max(el, axis=-1, keepdims=True)
    i1 = jnp.min(jnp.where(in_grp & (el == v1), lane, big), axis=-1, keepdims=True)
    rest = in_grp & (lane != i1)
    el2 = jnp.where(rest, lg, neg)
    v2 = jnp.max(el2, axis=-1, keepdims=True)
    i2 = jnp.min(jnp.where(rest & (el2 == v2), lane, big), axis=-1, keepdims=True)
    ex = jnp.exp(v2 - v1)
    p1 = 1.0 / (1.0 + ex)
    p2 = ex / (1.0 + ex)

    hit1 = lane == i1
    hit2 = lane == i2
    onehot = jnp.where(hit1 | hit2, 1.0, 0.0)
    r = lax.broadcasted_iota(I32, (tr, tr), 0)
    c = lax.broadcasted_iota(I32, (tr, tr), 1)
    strict = jnp.where(c < r, 1.0, 0.0).astype(BF16)
    before = _dot(strict, onehot.astype(BF16)) + carry_ref[...]
    rank1 = jnp.sum(jnp.where(hit1, before, 0.0), axis=-1, keepdims=True)
    rank2 = jnp.sum(jnp.where(hit2, before, 0.0), axis=-1, keepdims=True)
    carry_ref[...] = carry_ref[...] + jnp.sum(onehot, axis=0, keepdims=True)
    cnt_ref[...] = carry_ref[...]

    e1 = i1 - ROUTER_EXPERT_LANE
    e2 = i2 - ROUTER_EXPERT_LANE
    ri = jnp.where(lane == 0, e1, jnp.where(lane == 1, e2, jnp.where(lane == 2, rank1, jnp.where(lane == 3, rank2, 0.0))))
    ri_ref[...] = ri.astype(I32)
    rf_ref[...] = jnp.where(lane == 0, gtop * p1, jnp.where(lane == 1, gtop * p2, 0.0))


def _router_call(logits):
    T = logits.shape[0]
    tr = ROUTE_TR
    rows = pl.BlockSpec((tr, LANES), lambda i: (i, 0))
    return pl.pallas_call(
        _router_body,
        grid=(T // tr,),
        in_specs=[rows],
        out_specs=[rows, rows, pl.BlockSpec((1, LANES), lambda i: (0, 0))],
        out_shape=[jax.ShapeDtypeStruct((T, LANES), I32),
                   jax.ShapeDtypeStruct((T, LANES), F32),
                   jax.ShapeDtypeStruct((1, LANES), F32)],
        scratch_shapes=[pltpu.VMEM((1, LANES), F32)],
        compiler_params=_cparams(("arbitrary",)),
        name="router",
    )(logits)


def _moe_body(tile_e_ref, tile_n_ref, src_ref, dst_ref, h_hbm, wg_ref, wu_ref, wd_ref, y_hbm,
              xbuf, ybuf, gsem, ssem):
    k = pl.program_id(0)
    nv = tile_n_ref[k]
    base = k * MOE_TM
    n_rows = y_hbm.shape[0] - SUBLANES

    @pl.when(k == 0)
    def _():
        ybuf[0:SUBLANES, :] = jnp.zeros((SUBLANES, D_MODEL), F32)
        spare = pltpu.make_async_copy(ybuf.at[pl.ds(0, SUBLANES)], y_hbm.at[pl.ds(n_rows, SUBLANES)], ssem)
        spare.start()
        spare.wait()

    @pl.when(nv > 0)
    def _():
        def gather(r, carry):
            pltpu.make_async_copy(h_hbm.at[pl.ds(src_ref[base + r], 1)], xbuf.at[pl.ds(r, 1)], gsem).start()
            return carry

        lax.fori_loop(0, MOE_TM, gather, 0)
        pltpu.make_async_copy(h_hbm.at[pl.ds(0, MOE_TM)], xbuf, gsem).wait()

        xb = xbuf[...].astype(BF16)
        gate = _dot(xb, wg_ref[...].astype(BF16))
        up = _dot(xb, wu_ref[...].astype(BF16))
        mid = (gate * _sigmoid(gate) * up).astype(BF16)
        ybuf[...] = _dot(mid, wd_ref[...].astype(BF16))

        def scatter(r, carry):
            pltpu.make_async_copy(ybuf.at[pl.ds(r, 1)], y_hbm.at[pl.ds(dst_ref[base + r], 1)], ssem).start()
            return carry

        nv8 = pl.multiple_of(((nv + SUBLANES - 1) // SUBLANES) * SUBLANES, SUBLANES)
        lax.fori_loop(0, nv8, scatter, 0)
        pltpu.make_async_copy(ybuf.at[pl.ds(0, nv8)], y_hbm.at[pl.ds(0, nv8)], ssem).wait()


def _moe_call(tile_e, tile_n, src, dst, h, wg, wu, wd, layer, n_tiles):
    T = h.shape[0]
    e_of = lambda k, te: layer * N_EXPERTS + te[k]
    grid_spec = pltpu.PrefetchScalarGridSpec(
        num_scalar_prefetch=4,
        grid=(n_tiles,),
        in_specs=[pl.BlockSpec(memory_space=pl.ANY),
                  pl.BlockSpec((None, D_MODEL, D_EXPERT), lambda k, te, tn, s, d: (e_of(k, te), 0, 0)),
                  pl.BlockSpec((None, D_MODEL, D_EXPERT), lambda k, te, tn, s, d: (e_of(k, te), 0, 0)),
                  pl.BlockSpec((None, D_EXPERT, D_MODEL), lambda k, te, tn, s, d: (e_of(k, te), 0, 0))],
        out_specs=pl.BlockSpec(memory_space=pl.ANY),
        scratch_shapes=[pltpu.VMEM((MOE_TM, D_MODEL), F32),
                        pltpu.VMEM((MOE_TM, D_MODEL), F32),
                        pltpu.SemaphoreType.DMA(()),
                        pltpu.SemaphoreType.DMA(())],
    )
    return pl.pallas_call(
        _moe_body,
        grid_spec=grid_spec,
        out_shape=jax.ShapeDtypeStruct((2 * T + SUBLANES, D_MODEL), F32),
        compiler_params=_cparams(("arbitrary",)),
        name="moe_experts",
    )(tile_e, tile_n, src, dst, h, wg, wu, wd)


def _route_tables(route_i, counts, T, n_tiles):
    n_e = counts[0, ROUTER_EXPERT_LANE:ROUTER_EXPERT_LANE + N_EXPERTS].astype(I32)
    tiles_e = (n_e + MOE_TM - 1) // MOE_TM
    tile_end = jnp.cumsum(tiles_e)
    tile_start = tile_end - tiles_e
    pos1 = tile_start[route_i[:, 0]] * MOE_TM + route_i[:, 2]
    pos2 = tile_start[route_i[:, 1]] * MOE_TM + route_i[:, 3]
    tok = jnp.arange(T, dtype=I32)
    slot = jnp.full((n_tiles * MOE_TM,), -1, I32)
    slot = slot.at[jnp.concatenate([pos1, pos2])].set(jnp.concatenate([tok, tok + T]), unique_indices=True)
    spare = 2 * T + jnp.arange(n_tiles * MOE_TM, dtype=I32) % SUBLANES
    dst = jnp.where(slot >= 0, slot, spare)
    src = jnp.where(slot >= T, slot - T, jnp.maximum(slot, 0))
    kk = jnp.arange(n_tiles, dtype=I32)
    last_tile = jnp.maximum(tile_end[-1] - 1, 0)
    tile_e = jnp.sum((tile_end[None, :] <= jnp.minimum(kk, last_tile)[:, None]).astype(I32), axis=1)
    tile_e = jnp.minimum(tile_e, N_EXPERTS - 1)
    tile_n = jnp.clip(n_e[tile_e] - (kk - tile_start[tile_e]) * MOE_TM, 0, MOE_TM)
    tile_n = jnp.where(kk < tile_end[-1], tile_n, 0).astype(I32)
    return tile_e, tile_n, src, dst


def _pad_lanes(w, offset=0):
    return jnp.pad(w, ((0, 0), (offset, LANES - offset - w.shape[1])))


def kernel(x, norm1_g, w_in, conv_w, conv_b, conv_ln_g, conv_ln_b, gla_w2, gla_b2, gla_norm_g, fox_f_b, w_out, norm2_g, router_group_w, router_group_b, router_expert_w, router_expert_b, ffn_w_gate, ffn_w_up, ffn_w_down, final_norm_g):
    B, S, D = x.shape
    T = B * S
    depth = w_in.shape[0]
    n_tiles = (2 * T) // MOE_TM + N_EXPERTS
    fox_start = D_MAIN + GLA_RANK

    wg_all = ffn_w_gate.reshape(depth * N_EXPERTS, D_MODEL, D_EXPERT)
    wu_all = ffn_w_up.reshape(depth * N_EXPERTS, D_MODEL, D_EXPERT)
    wd_all = ffn_w_down.reshape(depth * N_EXPERTS, D_EXPERT, D_MODEL)

    def small_weights(l):
        ws = jnp.concatenate([w_in[l][:, D_MAIN:fox_start], w_in[l][:, fox_start + 3 * D_FOX:]], axis=1)
        return _split_bf16(_pad_lanes(ws))

    xt = x.reshape(T, D)
    h, small = _norm_call(xt, norm1_g[0][None, :], small_w=small_weights(0))
    for l in range(depth):
        w_proj = jnp.concatenate([w_in[l][:, :D_MAIN], w_in[l][:, fox_start:fox_start + 3 * D_FOX]], axis=1).astype(BF16)
        proj = _inproj_call(h, w_proj)

        y_conv = _conv_call(proj, jnp.pad(conv_w[l], ((0, CONV_PAD - CONV_WIDTH), (0, 0))), conv_b[l][None, :],
                            conv_ln_g[l][None, :], conv_ln_b[l][None, :], B, S)
        w2_hi, w2_lo = _split_bf16(jnp.pad(gla_w2[l], ((0, LANES - GLA_RANK), (0, 0))))
        y_gla = _gla_call(proj, small, w2_hi, w2_lo, gla_b2[l][None, :], gla_norm_g[l][None, :], B, S)
        fcol, frow = _fgate_call(small, _pad_lanes(fox_f_b[l][None, :], SMALL_FOX_LANE), B, S)
        y_fox = _fox_call(proj, fcol, frow, B, S)

        w_route = jnp.concatenate([router_group_w[l],
                                   router_expert_w[l].transpose(1, 0, 2).reshape(D_MODEL, N_EXPERTS)], axis=1)
        wr_hi, wr_lo = _split_bf16(_pad_lanes(w_route))
        b_route = _pad_lanes(jnp.concatenate([router_group_b[l], router_expert_b[l].reshape(-1)])[None, :])
        xt, h2, logits = _outproj_call(y_conv, y_gla, y_fox, xt, w_out[l].astype(BF16), norm2_g[l][None, :],
                                       wr_hi, wr_lo, b_route)

        route_i, gates, counts = _router_call(logits)
        tile_e, tile_n, src, dst = _route_tables(route_i, counts, T, n_tiles)
        y2 = _moe_call(tile_e, tile_n, src, dst, h2, wg_all, wu_all, wd_all, l, n_tiles)

        if l + 1 < depth:
            xt, h, small = _norm_call(xt, norm1_g[l + 1][None, :], moe=(y2, gates), small_w=small_weights(l + 1))
        else:
            _, out = _norm_call(xt, final_norm_g[None, :], moe=(y2, gates), out_dtype=F32)
    return out.reshape(B, S, D)
```

```python
import functools

import jax
import jax.numpy as jnp
from jax import lax
from jax.experimental import pallas as pl
from jax.experimental.pallas import tpu as pltpu

F32 = jnp.float32
BF16 = jnp.bfloat16
I32 = jnp.int32

D_MODEL = 2048
EPS = 1e-6
D_CONV = 512
CONV_WIDTH = 31
D_GLA = 1024
GLA_HEADS = 4
GLA_DK = 128
GLA_DV = 256
GLA_KEY = GLA_HEADS * GLA_DK
GLA_RANK = 16
GLA_GATE_NORMALIZER = 16.0
GLA_CHUNK = 64
D_FOX = 512
FOX_HEADS = 4
FOX_DH = 128
N_GROUPS = 4
EXPERTS_PER_GROUP = 8
N_EXPERTS = N_GROUPS * EXPERTS_PER_GROUP
D_EXPERT = 512

LANES = 128
SUBLANES = 8
D_MAIN = 2 * D_CONV + 2 * GLA_KEY + 2 * D_GLA
D_PROJ = D_MAIN + 3 * D_FOX
SMALL_FOX_LANE = GLA_RANK
ROUTER_EXPERT_LANE = N_GROUPS

VMEM_LIMIT = 56 * 1024 * 1024

ROW_TILE = 256
MM_TM = 1024
MM_TN = 512
GLA_TS = 256
FOX_TQ = 256
FOX_TK = 256
FOX_VT_BLK = 512
CONV_RC = 64
ROUTE_TR = 512
MOE_TM = 256


def _cparams(sem):
    return pltpu.CompilerParams(dimension_semantics=sem, vmem_limit_bytes=VMEM_LIMIT)


def _split_bf16(x):
    hi = x.astype(BF16)
    lo = (x - hi.astype(F32)).astype(BF16)
    return hi, lo


def _dot(a, b):
    return jnp.dot(a, b, preferred_element_type=F32)


def _dot3(a, b_hi, b_lo):
    a_hi, a_lo = _split_bf16(a)
    return _dot(a_hi, b_hi) + _dot(a_lo, b_hi) + _dot(a_hi, b_lo)


def _sigmoid(x):
    return 1.0 / (1.0 + jnp.exp(-x))


def _log_sigmoid(x):
    return jnp.minimum(x, 0.0) - jnp.log(1.0 + jnp.exp(-jnp.abs(x)))


def _norm_body(*refs, combine, project):
    it = iter(refs)
    x_ref = next(it)
    if combine:
        ya_ref, yb_ref, gates_ref = next(it), next(it), next(it)
    g_ref = next(it)
    if project:
        ws_hi_ref, ws_lo_ref = next(it), next(it)
    if combine and project:
        xo_ref = next(it)
    h_ref = next(it)
    if project:
        small_ref = next(it)

    x = x_ref[...]
    if combine:
        gates = gates_ref[...]
        x = x + gates[:, 0:1] * ya_ref[...] + gates[:, 1:2] * yb_ref[...]
        if project:
            xo_ref[...] = x
    y = x * lax.rsqrt(jnp.mean(x * x, axis=-1, keepdims=True) + EPS) * g_ref[...]
    h_ref[...] = y.astype(h_ref.dtype)
    if project:
        small_ref[...] = _dot3(y, ws_hi_ref[...], ws_lo_ref[...])


def _norm_call(x, g_row, *, moe=None, small_w=None, out_dtype=BF16):
    T = x.shape[0]
    tm = ROW_TILE
    combine = moe is not None
    project = small_w is not None
    row_spec = pl.BlockSpec((tm, D_MODEL), lambda i: (i, 0))
    lane_spec = pl.BlockSpec((tm, LANES), lambda i: (i, 0))
    const = lambda shape: pl.BlockSpec(shape, lambda i: (0, 0))
    nblk = T // tm
    ins, in_specs = [x], [row_spec]
    if combine:
        y2, gates = moe
        ins += [y2, y2, gates]
        in_specs += [row_spec, pl.BlockSpec((tm, D_MODEL), lambda i: (i + nblk, 0)), lane_spec]
    ins.append(g_row)
    in_specs.append(const((1, D_MODEL)))
    if project:
        ins += list(small_w)
        in_specs += [const((D_MODEL, LANES)), const((D_MODEL, LANES))]
    out_shape, out_specs = [], []
    if combine and project:
        out_shape.append(jax.ShapeDtypeStruct((T, D_MODEL), F32))
        out_specs.append(row_spec)
    out_shape.append(jax.ShapeDtypeStruct((T, D_MODEL), out_dtype))
    out_specs.append(row_spec)
    if project:
        out_shape.append(jax.ShapeDtypeStruct((T, LANES), F32))
        out_specs.append(lane_spec)
    return pl.pallas_call(
        functools.partial(_norm_body, combine=combine, project=project),
        grid=(nblk,),
        in_specs=in_specs,
        out_specs=out_specs,
        out_shape=out_shape,
        compiler_params=_cparams(("parallel",)),
        name="norm",
    )(*ins)


def _matmul_body(h_ref, w_ref, o_ref):
    o_ref[...] = _dot(h_ref[...], w_ref[...]).astype(o_ref.dtype)


def _inproj_call(h, w):
    T = h.shape[0]
    tm = min(MM_TM, T)
    return pl.pallas_call(
        _matmul_body,
        grid=(T // tm, D_PROJ // MM_TN),
        in_specs=[pl.BlockSpec((tm, D_MODEL), lambda i, j: (i, 0)),
                  pl.BlockSpec((D_MODEL, MM_TN), lambda i, j: (0, j))],
        out_specs=pl.BlockSpec((tm, MM_TN), lambda i, j: (i, j)),
        out_shape=jax.ShapeDtypeStruct((T, D_PROJ), BF16),
        compiler_params=_cparams(("parallel", "parallel")),
        name="inproj",
    )(h, w)


CONV_PAD = 32


def _conv_body(a_ref, g_ref, w_ref, b_ref, lng_ref, lnb_ref, o_ref, u_ref):
    S = a_ref.shape[0]
    u_ref[0:CONV_PAD, :] = jnp.zeros((CONV_PAD, D_CONV), F32)
    u_ref[CONV_PAD:CONV_PAD + S, :] = a_ref[...].astype(F32) * _sigmoid(g_ref[...].astype(F32))
    bias = b_ref[...]
    lng = lng_ref[...]
    lnb = lnb_ref[...]
    first = CONV_PAD - (CONV_WIDTH - 1)

    def chunk(c, carry):
        r0 = pl.multiple_of(c * CONV_RC, CONV_RC)
        acc = jnp.broadcast_to(bias, (CONV_RC, D_CONV))
        win = u_ref[pl.ds(r0, CONV_RC + CONV_PAD), :]
        for j in range(CONV_WIDTH):
            acc = acc + w_ref[j:j + 1, :] * win[first + j:first + j + CONV_RC, :]
        mu = jnp.mean(acc, axis=-1, keepdims=True)
        d = acc - mu
        var = jnp.mean(d * d, axis=-1, keepdims=True)
        yn = d * lax.rsqrt(var + EPS) * lng + lnb
        o_ref[pl.ds(r0, CONV_RC), :] = (yn * _sigmoid(yn)).astype(o_ref.dtype)
        return carry

    lax.fori_loop(0, S // CONV_RC, chunk, 0)


def _conv_call(proj, w_pad, b_row, lng_row, lnb_row, B, S):
    T = B * S
    const = lambda shape: pl.BlockSpec(shape, lambda b: (0, 0))
    return pl.pallas_call(
        _conv_body,
        grid=(B,),
        in_specs=[pl.BlockSpec((S, D_CONV), lambda b: (b, 0)),
                  pl.BlockSpec((S, D_CONV), lambda b: (b, 1)),
                  const((CONV_PAD, D_CONV)), const((1, D_CONV)), const((1, D_CONV)), const((1, D_CONV))],
        out_specs=pl.BlockSpec((S, D_CONV), lambda b: (b, 0)),
        out_shape=jax.ShapeDtypeStruct((T, D_CONV), BF16),
        scratch_shapes=[pltpu.VMEM((CONV_PAD + S, D_CONV), F32)],
        compiler_params=_cparams(("parallel",)),
        name="conv_mixer",
    )(proj, proj, w_pad, b_row, lng_row, lnb_row)


def _gla_body(q_ref, k_ref, v_ref, g_ref, low_ref, w2hi_ref, w2lo_ref, b2_ref, ng_ref, o_ref, st_ref):
    ts = q_ref.shape[0]
    nchunk = ts // GLA_CHUNK

    @pl.when(pl.program_id(1) == 0)
    def _():
        st_ref[...] = jnp.zeros(st_ref.shape, F32)

    la = _log_sigmoid(_dot3(low_ref[...], w2hi_ref[...], w2lo_ref[...]) + b2_ref[...]) * (1.0 / GLA_GATE_NORMALIZER)
    r = lax.broadcasted_iota(I32, (2 * ts, ts), 0)
    c = lax.broadcasted_iota(I32, (2 * ts, ts), 1)
    rr = jnp.where(r >= ts, r - ts, r)
    same_chunk = (rr // GLA_CHUNK) == (c // GLA_CHUNK)
    sel = jnp.where(same_chunk & ((r >= ts) | (c <= rr)), 1.0, 0.0).astype(BF16)
    la_hi, la_lo = _split_bf16(la)
    sums = _dot(sel, la_hi) + _dot(sel, la_lo)
    cum = sums[0:ts, :]
    last = sums[ts:2 * ts, :]
    e_q = jnp.exp(cum)
    e_inv = jnp.exp(-cum)
    e_end = jnp.exp(last - cum)
    e_last = jnp.exp(last)

    qr = lax.broadcasted_iota(I32, (ts, ts), 0)
    qc = lax.broadcasted_iota(I32, (ts, ts), 1)
    att_mask = ((qr // GLA_CHUNK) == (qc // GLA_CHUNK)) & (qc <= qr)
    ng = ng_ref[...]

    for h in range(GLA_HEADS):
        ks = slice(h * GLA_DK, (h + 1) * GLA_DK)
        vs = slice(h * GLA_DV, (h + 1) * GLA_DV)
        qh = q_ref[:, ks].astype(F32) * (GLA_DK ** -0.5)
        kh = k_ref[:, ks].astype(F32)
        vh = v_ref[:, vs]
        q_dec = (qh * e_q[:, ks]).astype(BF16)
        k_inv = (kh * e_inv[:, ks]).astype(BF16)
        k_end = (kh * e_end[:, ks]).astype(BF16)
        att = lax.dot_general(q_dec, k_inv, (((1,), (1,)), ((), ())), preferred_element_type=F32)
        att = jnp.where(att_mask, att, 0.0).astype(BF16)
        o_intra = _dot(att, vh)
        state = st_ref[h]
        outs = []
        for n in range(nchunk):
            rs = slice(n * GLA_CHUNK, (n + 1) * GLA_CHUNK)
            inter = lax.dot_general(q_dec[rs], state.astype(BF16), (((1,), (1,)), ((), ())),
                                    preferred_element_type=F32)
            outs.append(o_intra[rs] + inter)
            kv_t = lax.dot_general(vh[rs], k_end[rs], (((0,), (0,)), ((), ())), preferred_element_type=F32)
            state = state * e_last[n * GLA_CHUNK:n * GLA_CHUNK + 1, ks] + kv_t
        st_ref[h] = state
        o = jnp.concatenate(outs, axis=0)
        o = o * lax.rsqrt(jnp.mean(o * o, axis=-1, keepdims=True) + EPS) * ng
        gate = g_ref[:, vs].astype(F32)
        o_ref[:, vs] = (o * (gate * _sigmoid(gate))).astype(o_ref.dtype)


def _gla_call(proj, small, w2_hi, w2_lo, b2_row, ng_row, B, S):
    T = B * S
    ts = GLA_TS
    nst = S // ts
    row = lambda b, s: b * nst + s
    const = lambda shape: pl.BlockSpec(shape, lambda b, s: (0, 0))
    return pl.pallas_call(
        _gla_body,
        grid=(B, nst),
        in_specs=[pl.BlockSpec((ts, GLA_KEY), lambda b, s: (row(b, s), 2)),
                  pl.BlockSpec((ts, GLA_KEY), lambda b, s: (row(b, s), 3)),
                  pl.BlockSpec((ts, D_GLA), lambda b, s: (row(b, s), 2)),
                  pl.BlockSpec((ts, D_GLA), lambda b, s: (row(b, s), 3)),
                  pl.BlockSpec((ts, LANES), lambda b, s: (row(b, s), 0)),
                  const((LANES, GLA_KEY)), const((LANES, GLA_KEY)), const((1, GLA_KEY)), const((1, GLA_DV))],
        out_specs=pl.BlockSpec((ts, D_GLA), lambda b, s: (row(b, s), 0)),
        out_shape=jax.ShapeDtypeStruct((T, D_GLA), BF16),
        scratch_shapes=[pltpu.VMEM((GLA_HEADS, GLA_DV, GLA_DK), F32)],
        compiler_params=_cparams(("parallel", "arbitrary")),
        name="gla_mixer",
    )(proj, proj, proj, proj, small, w2_hi, w2_lo, b2_row, ng_row)


FGATE_BLK = 256


def _fgate_body(small_ref, fb_ref, fcol_ref, frow_ref):
    S = small_ref.shape[0]
    r = lax.broadcasted_iota(I32, (FGATE_BLK, FGATE_BLK), 0)
    c = lax.broadcasted_iota(I32, (FGATE_BLK, FGATE_BLK), 1)
    tri = jnp.where(c <= r, 1.0, 0.0).astype(BF16)
    carry = jnp.zeros((1, LANES), F32)
    for n in range(S // FGATE_BLK):
        rs = slice(n * FGATE_BLK, (n + 1) * FGATE_BLK)
        lf = _log_sigmoid(small_ref[rs, :] + fb_ref[...])
        p0 = lf.astype(BF16)
        r1 = lf - p0.astype(F32)
        p1 = r1.astype(BF16)
        p2 = (r1 - p1.astype(F32)).astype(BF16)
        blk = _dot(tri, p0) + _dot(tri, p1) + _dot(tri, p2) + carry
        fcol_ref[rs, :] = blk
        carry = blk[FGATE_BLK - 1:FGATE_BLK, :]
    ft = fcol_ref[...].T
    for h in range(FOX_HEADS):
        frow_ref[0, h] = ft[SMALL_FOX_LANE + h:SMALL_FOX_LANE + h + 1, :]


def _fgate_call(small, fb_row, B, S):
    T = B * S
    return pl.pallas_call(
        _fgate_body,
        grid=(B,),
        in_specs=[pl.BlockSpec((S, LANES), lambda b: (b, 0)),
                  pl.BlockSpec((1, LANES), lambda b: (0, 0))],
        out_specs=[pl.BlockSpec((S, LANES), lambda b: (b, 0)),
                   pl.BlockSpec((1, FOX_HEADS, 1, S), lambda b: (b, 0, 0, 0))],
        out_shape=[jax.ShapeDtypeStruct((T, LANES), F32),
                   jax.ShapeDtypeStruct((B, FOX_HEADS, 1, S), F32)],
        compiler_params=_cparams(("parallel",)),
        name="fox_gate",
    )(small, fb_row)


def _fox_body(q_ref, k_ref, v_ref, fcol_ref, frow_ref, o_ref, vt_ref, fb_ref, acc_ref):
    tq = q_ref.shape[0]
    tk = FOX_TK
    S = k_ref.shape[0]
    i = pl.program_id(1)

    @pl.when(i == 0)
    def _():
        for c in range(S // FOX_VT_BLK):
            cs = slice(c * FOX_VT_BLK, (c + 1) * FOX_VT_BLK)
            vt_ref[:, cs] = v_ref[cs, :].astype(F32).T.astype(BF16)
        for h in range(FOX_HEADS):
            fb_ref[h] = jnp.broadcast_to(fcol_ref[:, SMALL_FOX_LANE + h:SMALL_FOX_LANE + h + 1], (S, LANES))

    q0 = pl.multiple_of(i * tq, tq)
    key = lax.broadcasted_iota(I32, (tk, tq), 0)
    qry = lax.broadcasted_iota(I32, (tk, tq), 1)
    causal = key <= qry

    heads = []
    for h in range(FOX_HEADS):
        hs = slice(h * FOX_DH, (h + 1) * FOX_DH)
        qh = (q_ref[:, hs].astype(F32) * (FOX_DH ** -0.5)).astype(BF16)
        f_t = frow_ref[0, h, :, pl.ds(q0, tq)]
        heads.append((hs, qh, f_t))

    def update(j, h, state, masked):
        hs, qh, f_t = heads[h]
        m, l = state
        k0 = pl.multiple_of(j * tk, tk)
        kt = k_ref[pl.ds(k0, tk), hs]
        f_s = fb_ref[h, pl.ds(k0, tk), :]
        z = lax.dot_general(kt, qh, (((1,), (1,)), ((), ())), preferred_element_type=F32)
        z = z - jnp.concatenate([f_s] * (tq // LANES), axis=1)
        if masked:
            z = jnp.where(causal, z, -jnp.inf)
        m_new = jnp.maximum(m, jnp.max(z, axis=0, keepdims=True) + f_t)
        p = jnp.exp(z + (f_t - m_new))
        alpha = jnp.exp(m - m_new)
        l = alpha * l + jnp.sum(p, axis=0, keepdims=True)
        pv = _dot(vt_ref[hs, pl.ds(k0, tk)], p.astype(BF16))
        acc_ref[h] = alpha * acc_ref[h] + pv
        return m_new, l

    acc_ref[...] = jnp.zeros(acc_ref.shape, F32)
    init = (jnp.full((1, tq), -jnp.inf, F32), jnp.zeros((1, tq), F32))
    states = tuple(update(i, h, init, True) for h in range(FOX_HEADS))

    def step(j, states):
        return tuple(update(j, h, states[h], False) for h in range(FOX_HEADS))

    states = lax.fori_loop(0, i, step, states)
    for h in range(FOX_HEADS):
        m, l = states[h]
        o_ref[:, heads[h][0]] = (acc_ref[h] / l).T.astype(o_ref.dtype)


def _fox_call(proj, fcol, frow, B, S):
    T = B * S
    tq = FOX_TQ
    nq = S // tq
    col0 = D_MAIN // D_FOX
    return pl.pallas_call(
        _fox_body,
        grid=(B, nq),
        in_specs=[pl.BlockSpec((tq, D_FOX), lambda b, i: (b * nq + i, col0)),
                  pl.BlockSpec((S, D_FOX), lambda b, i: (b, col0 + 1)),
                  pl.BlockSpec((S, D_FOX), lambda b, i: (b, col0 + 2)),
                  pl.BlockSpec((S, LANES), lambda b, i: (b, 0)),
                  pl.BlockSpec((1, FOX_HEADS, 1, S), lambda b, i: (b, 0, 0, 0))],
        out_specs=pl.BlockSpec((tq, D_FOX), lambda b, i: (b * nq + i, 0)),
        out_shape=jax.ShapeDtypeStruct((T, D_FOX), BF16),
        scratch_shapes=[pltpu.VMEM((D_FOX, S), BF16),
                        pltpu.VMEM((FOX_HEADS, S, LANES), F32),
                        pltpu.VMEM((FOX_HEADS, FOX_DH, tq), F32)],
        compiler_params=_cparams(("parallel", "arbitrary")),
        name="fox_mixer",
    )(proj, proj, proj, fcol, frow)


def _outproj_body(yc_ref, yg_ref, yf_ref, x_ref, w_ref, g_ref, wr_hi_ref, wr_lo_ref, br_ref,
                  xo_ref, h_ref, lg_ref):
    acc = x_ref[...]
    acc = acc + _dot(yc_ref[...], w_ref[0:D_CONV, :])
    acc = acc + _dot(yg_ref[...], w_ref[D_CONV:D_CONV + D_GLA, :])
    acc = acc + _dot(yf_ref[...], w_ref[D_CONV + D_GLA:D_MODEL, :])
    xo_ref[...] = acc
    hn = acc * lax.rsqrt(jnp.mean(acc * acc, axis=-1, keepdims=True) + EPS) * g_ref[...]
    h_ref[...] = hn
    lg_ref[...] = _dot3(hn, wr_hi_ref[...], wr_lo_ref[...]) + br_ref[...]


def _outproj_call(yc, yg, yf, x, w, g_row, wr_hi, wr_lo, br_row):
    T = x.shape[0]
    tm = ROW_TILE
    const = lambda shape: pl.BlockSpec(shape, lambda i: (0, 0))
    rows = lambda width: pl.BlockSpec((tm, width), lambda i: (i, 0))
    return pl.pallas_call(
        _outproj_body,
        grid=(T // tm,),
        in_specs=[rows(D_CONV), rows(D_GLA), rows(D_FOX), rows(D_MODEL),
                  const((D_MODEL, D_MODEL)), const((1, D_MODEL)),
                  const((D_MODEL, LANES)), const((D_MODEL, LANES)), const((1, LANES))],
        out_specs=[rows(D_MODEL), rows(D_MODEL), rows(LANES)],
        out_shape=[jax.ShapeDtypeStruct((T, D_MODEL), F32),
                   jax.ShapeDtypeStruct((T, D_MODEL), F32),
                   jax.ShapeDtypeStruct((T, LANES), F32)],
        compiler_params=_cparams(("parallel",)),
        name="outproj",
    )(yc, yg, yf, x, w, g_row, wr_hi, wr_lo, br_row)


def _router_body(lg_ref, ri_ref, rf_ref, cnt_ref, carry_ref):
    tr = lg_ref.shape[0]

    @pl.when(pl.program_id(0) == 0)
    def _():
        carry_ref[...] = jnp.zeros(carry_ref.shape, F32)

    lg = lg_ref[...]
    lane = lax.broadcasted_iota(I32, (tr, LANES), 1).astype(F32)
    big = float(LANES)
    neg = -jnp.inf

    is_g = lane < N_GROUPS
    gl = jnp.where(is_g, lg, neg)
    gmax = jnp.max(gl, axis=-1, keepdims=True)
    gexp = jnp.where(is_g, jnp.exp(lg - gmax), 0.0)
    gprob = gexp / jnp.sum(gexp, axis=-1, keepdims=True)
    gtop = jnp.max(gprob, axis=-1, keepdims=True)
    grp = jnp.min(jnp.where(is_g & (gprob == gtop), lane, big), axis=-1, keepdims=True)

    lo = ROUTER_EXPERT_LANE + grp * EXPERTS_PER_GROUP
    in_grp = (lane >= lo) & (lane < lo + EXPERTS_PER_GROUP)
    el = jnp.where(in_grp, lg, neg)
    v1 = jnp.max(el, axis=-1, keepdims=True)
    i1 = jnp.min(jnp.where(in_grp & (el == v1), lane, big), axis=-1, keepdims=True)
    rest = in_grp & (lane != i1)
    el2 = jnp.where(rest, lg, neg)
    v2 = jnp.max(el2, axis=-1, keepdims=True)
    i2 = jnp.min(jnp.where(rest & (el2 == v2), lane, big), axis=-1, keepdims=True)
    ex = jnp.exp(v2 - v1)
    p1 = 1.0 / (1.0 + ex)
    p2 = ex / (1.0 + ex)

    hit1 = lane == i1
    hit2 = lane == i2
    onehot = jnp.where(hit1 | hit2, 1.0, 0.0)
    r = lax.broadcasted_iota(I32, (tr, tr), 0)
    c = lax.broadcasted_iota(I32, (tr, tr), 1)
    strict = jnp.where(c < r, 1.0, 0.0).astype(BF16)
    before = _dot(strict, onehot.astype(BF16)) + carry_ref[...]
    rank1 = jnp.sum(jnp.where(hit1, before, 0.0), axis=-1, keepdims=True)
    rank2 = jnp.sum(jnp.where(hit2, before, 0.0), axis=-1, keepdims=True)
    carry_ref[...] = carry_ref[...] + jnp.sum(onehot, axis=0, keepdims=True)
    cnt_ref[...] = carry_ref[...]

    e1 = i1 - ROUTER_EXPERT_LANE
    e2 = i2 - ROUTER_EXPERT_LANE
    ri = jnp.where(lane == 0, e1, jnp.where(lane == 1, e2, jnp.where(lane == 2, rank1, jnp.where(lane == 3, rank2, 0.0))))
    ri_ref[...] = ri.astype(I32)
    rf_ref[...] = jnp.where(lane == 0, gtop * p1, jnp.where(lane == 1, gtop * p2, 0.0))


def _router_call(logits):
    T = logits.shape[0]
    tr = ROUTE_TR
    rows = pl.BlockSpec((tr, LANES), lambda i: (i, 0))
    return pl.pallas_call(
        _router_body,
        grid=(T // tr,),
        in_specs=[rows],
        out_specs=[rows, rows, pl.BlockSpec((1, LANES), lambda i: (0, 0))],
        out_shape=[jax.ShapeDtypeStruct((T, LANES), I32),
                   jax.ShapeDtypeStruct((T, LANES), F32),
                   jax.ShapeDtypeStruct((1, LANES), F32)],
        scratch_shapes=[pltpu.VMEM((1, LANES), F32)],
        compiler_params=_cparams(("arbitrary",)),
        name="router",
    )(logits)


def _moe_body(tile_e_ref, tile_n_ref, first_ref, wslot_ref, next_e_ref, src_ref, dst_ref,
              h_hbm, wg_hbm, wu_hbm, wd_hbm, y_hbm,
              xbuf, ybuf, xs, wg_f, wu_f, wd_f, wg_b, wu_b, wd_b, gsem, ssem, wsem, *, expert0):
    k = pl.program_id(0)
    nv = tile_n_ref[k]
    cur = k % 2
    oth = 1 - cur

    def weight_copies(e, ws):
        return (pltpu.make_async_copy(wg_hbm.at[expert0 + e], wg_f.at[ws], wsem.at[ws]),
                pltpu.make_async_copy(wu_hbm.at[expert0 + e], wu_f.at[ws], wsem.at[ws]),
                pltpu.make_async_copy(wd_hbm.at[expert0 + e], wd_f.at[ws], wsem.at[ws]))

    def gather_row(tile, r, buf):
        return pltpu.make_async_copy(h_hbm.at[pl.ds(src_ref[tile * MOE_TM + r], 1)],
                                     xbuf.at[buf, pl.ds(r, 1)], gsem.at[buf])

    def scatter_row(tile, r, buf):
        return pltpu.make_async_copy(ybuf.at[buf, pl.ds(r, 1)],
                                     y_hbm.at[pl.ds(dst_ref[(tile + 1) * MOE_TM + r], 1)], ssem.at[buf])

    def wait_gather(buf):
        pltpu.make_async_copy(h_hbm.at[pl.ds(0, MOE_TM)], xbuf.at[buf], gsem.at[buf]).wait()

    def wait_scatter(buf):
        pltpu.make_async_copy(ybuf.at[buf], y_hbm.at[pl.ds(0, MOE_TM)], ssem.at[buf]).wait()

    @pl.when(k == 0)
    def _():
        ybuf[1] = jnp.zeros((MOE_TM, D_MODEL), F32)
        for c in weight_copies(tile_e_ref[0], 0):
            c.start()

        def issue(r, carry):
            gather_row(0, r, 0).start()
            return carry

        lax.fori_loop(0, MOE_TM, issue, 0, unroll=8)

    @pl.when(nv > 0)
    def _():
        @pl.when(first_ref[k] == 1)
        def _():
            ws = wslot_ref[k]
            for c in weight_copies(tile_e_ref[k], ws):
                c.wait()
            nxt = next_e_ref[k]

            @pl.when(nxt >= 0)
            def _():
                for c in weight_copies(nxt, 1 - ws):
                    c.start()

            wg_b[...] = wg_f[ws].astype(BF16)
            wu_b[...] = wu_f[ws].astype(BF16)
            wd_b[...] = wd_f[ws].astype(BF16)

        def tile_step(cur, oth):
            wait_gather(cur)
            xs[...] = xbuf[cur].astype(BF16)
            for r in range(MOE_TM):
                gather_row(k + 1, r, oth).start()
            for r in range(MOE_TM):
                scatter_row(k - 1, r, oth).start()
            xb = xs[...]
            gate = _dot(xb, wg_b[...])
            up = _dot(xb, wu_b[...])
            mid = (gate * _sigmoid(gate) * up).astype(BF16)
            ybuf[cur] = _dot(mid, wd_b[...])
            wait_scatter(oth)

        for parity in range(2):
            pl.when(cur == parity)(functools.partial(tile_step, parity, 1 - parity))

    prev_active = tile_n_ref[jnp.maximum(k - 1, 0)] > 0

    @pl.when((nv == 0) & (k > 0) & prev_active)
    def _():
        wait_gather(cur)

        def issue(r, carry):
            scatter_row(k - 1, r, oth).start()
            return carry

        lax.fori_loop(0, MOE_TM, issue, 0, unroll=8)
        wait_scatter(oth)


def _moe_call(tables, h, wg, wu, wd, layer, n_tiles):
    T = h.shape[0]
    any_spec = pl.BlockSpec(memory_space=pl.ANY)
    grid_spec = pltpu.PrefetchScalarGridSpec(
        num_scalar_prefetch=len(tables),
        grid=(n_tiles + 1,),
        in_specs=[any_spec, any_spec, any_spec, any_spec],
        out_specs=any_spec,
        scratch_shapes=[pltpu.VMEM((2, MOE_TM, D_MODEL), F32),
                        pltpu.VMEM((2, MOE_TM, D_MODEL), F32),
                        pltpu.VMEM((MOE_TM, D_MODEL), BF16),
                        pltpu.VMEM((2, D_MODEL, D_EXPERT), F32),
                        pltpu.VMEM((2, D_MODEL, D_EXPERT), F32),
                        pltpu.VMEM((2, D_EXPERT, D_MODEL), F32),
                        pltpu.VMEM((D_MODEL, D_EXPERT), BF16),
                        pltpu.VMEM((D_MODEL, D_EXPERT), BF16),
                        pltpu.VMEM((D_EXPERT, D_MODEL), BF16),
                        pltpu.SemaphoreType.DMA((2,)),
                        pltpu.SemaphoreType.DMA((2,)),
                        pltpu.SemaphoreType.DMA((2,))],
    )
    return pl.pallas_call(
        functools.partial(_moe_body, expert0=layer * N_EXPERTS),
        grid_spec=grid_spec,
        out_shape=jax.ShapeDtypeStruct((2 * T + MOE_TM, D_MODEL), F32),
        compiler_params=_cparams(("arbitrary",)),
        name="moe_experts",
    )(*tables, h, wg, wu, wd)


def _route_tables(route_i, counts, T, n_tiles):
    n_e = counts[0, ROUTER_EXPERT_LANE:ROUTER_EXPERT_LANE + N_EXPERTS].astype(I32)
    tiles_e = (n_e + MOE_TM - 1) // MOE_TM
    tile_end = jnp.cumsum(tiles_e)
    tile_start = tile_end - tiles_e
    total = tile_end[-1]
    pos1 = tile_start[route_i[:, 0]] * MOE_TM + route_i[:, 2]
    pos2 = tile_start[route_i[:, 1]] * MOE_TM + route_i[:, 3]
    tok = jnp.arange(T, dtype=I32)
    n_rows = (n_tiles + 1) * MOE_TM
    slot = jnp.full((n_rows,), -1, I32)
    slot = slot.at[jnp.concatenate([pos1, pos2])].set(jnp.concatenate([tok, tok + T]), unique_indices=True)
    spare = 2 * T + jnp.arange(n_rows, dtype=I32) % MOE_TM
    dst = jnp.where(slot >= 0, slot, spare)
    dst = jnp.concatenate([spare[:MOE_TM], dst[:n_tiles * MOE_TM]])
    src = jnp.where(slot >= T, slot - T, jnp.maximum(slot, 0))

    kk = jnp.arange(n_tiles + 1, dtype=I32)
    last_tile = jnp.maximum(total - 1, 0)
    tile_e = jnp.sum((tile_end[None, :] <= jnp.minimum(kk, last_tile)[:, None]).astype(I32), axis=1)
    tile_e = jnp.minimum(tile_e, N_EXPERTS - 1)
    active = kk < total
    tile_n = jnp.clip(n_e[tile_e] - (kk - tile_start[tile_e]) * MOE_TM, 0, MOE_TM)
    tile_n = jnp.where(active, tile_n, 0).astype(I32)
    first = (active & (kk == tile_start[tile_e])).astype(I32)
    has_rows = tiles_e > 0
    wslot = ((jnp.cumsum(has_rows.astype(I32)) - 1) % 2)[tile_e].astype(I32)
    ee = jnp.arange(N_EXPERTS, dtype=I32)
    later = jnp.where((ee[None, :] > ee[:, None]) & has_rows[None, :], ee[None, :], N_EXPERTS)
    next_e = jnp.min(later, axis=1)
    next_e = jnp.where(next_e < N_EXPERTS, next_e, -1)[tile_e].astype(I32)
    return tile_e, tile_n, first, wslot, next_e, src, dst


def _pad_lanes(w, offset=0):
    return jnp.pad(w, ((0, 0), (offset, LANES - offset - w.shape[1])))


def kernel(x, norm1_g, w_in, conv_w, conv_b, conv_ln_g, conv_ln_b, gla_w2, gla_b2, gla_norm_g, fox_f_b, w_out, norm2_g, router_group_w, router_group_b, router_expert_w, router_expert_b, ffn_w_gate, ffn_w_up, ffn_w_down, final_norm_g):
    B, S, D = x.shape
    T = B * S
    depth = w_in.shape[0]
    n_tiles = (2 * T) // MOE_TM + N_EXPERTS
    fox_start = D_MAIN + GLA_RANK

    wg_all = ffn_w_gate.reshape(depth * N_EXPERTS, D_MODEL, D_EXPERT)
    wu_all = ffn_w_up.reshape(depth * N_EXPERTS, D_MODEL, D_EXPERT)
    wd_all = ffn_w_down.reshape(depth * N_EXPERTS, D_EXPERT, D_MODEL)

    def small_weights(l):
        ws = jnp.concatenate([w_in[l][:, D_MAIN:fox_start], w_in[l][:, fox_start + 3 * D_FOX:]], axis=1)
        return _split_bf16(_pad_lanes(ws))

    xt = x.reshape(T, D)
    h, small = _norm_call(xt, norm1_g[0][None, :], small_w=small_weights(0))
    for l in range(depth):
        w_proj = jnp.concatenate([w_in[l][:, :D_MAIN], w_in[l][:, fox_start:fox_start + 3 * D_FOX]], axis=1).astype(BF16)
        proj = _inproj_call(h, w_proj)

        y_conv = _conv_call(proj, jnp.pad(conv_w[l], ((0, CONV_PAD - CONV_WIDTH), (0, 0))), conv_b[l][None, :],
                            conv_ln_g[l][None, :], conv_ln_b[l][None, :], B, S)
        w2_hi, w2_lo = _split_bf16(jnp.pad(gla_w2[l], ((0, LANES - GLA_RANK), (0, 0))))
        y_gla = _gla_call(proj, small, w2_hi, w2_lo, gla_b2[l][None, :], gla_norm_g[l][None, :], B, S)
        fcol, frow = _fgate_call(small, _pad_lanes(fox_f_b[l][None, :], SMALL_FOX_LANE), B, S)
        y_fox = _fox_call(proj, fcol, frow, B, S)

        w_route = jnp.concatenate([router_group_w[l],
                                   router_expert_w[l].transpose(1, 0, 2).reshape(D_MODEL, N_EXPERTS)], axis=1)
        wr_hi, wr_lo = _split_bf16(_pad_lanes(w_route))
        b_route = _pad_lanes(jnp.concatenate([router_group_b[l], router_expert_b[l].reshape(-1)])[None, :])
        xt, h2, logits = _outproj_call(y_conv, y_gla, y_fox, xt, w_out[l].astype(BF16), norm2_g[l][None, :],
                                       wr_hi, wr_lo, b_route)

        route_i, gates, counts = _router_call(logits)
        tables = _route_tables(route_i, counts, T, n_tiles)
        y2 = _moe_call(tables, h2, wg_all, wu_all, wd_all, l, n_tiles)

        if l + 1 < depth:
            xt, h, small = _norm_call(xt, norm1_g[l + 1][None, :], moe=(y2, gates), small_w=small_weights(l + 1))
        else:
            (out,) = _norm_call(xt, final_norm_g[None, :], moe=(y2, gates), out_dtype=F32)
    return out.reshape(B, S, D)
```

```python
import functools

import jax
import jax.numpy as jnp
from jax import lax
from jax.experimental import pallas as pl
from jax.experimental.pallas import tpu as pltpu

F32 = jnp.float32
BF16 = jnp.bfloat16
I32 = jnp.int32

D_MODEL = 2048
EPS = 1e-6
D_CONV = 512
CONV_WIDTH = 31
D_GLA = 1024
GLA_HEADS = 4
GLA_DK = 128
GLA_DV = 256
GLA_KEY = GLA_HEADS * GLA_DK
GLA_RANK = 16
GLA_GATE_NORMALIZER = 16.0
GLA_CHUNK = 64
D_FOX = 512
FOX_HEADS = 4
FOX_DH = 128
N_GROUPS = 4
EXPERTS_PER_GROUP = 8
N_EXPERTS = N_GROUPS * EXPERTS_PER_GROUP
D_EXPERT = 512

LANES = 128
SUBLANES = 8
D_MAIN = 2 * D_CONV + 2 * GLA_KEY + 2 * D_GLA
D_PROJ = D_MAIN + 3 * D_FOX
D_IN = D_MAIN + GLA_RANK + 3 * D_FOX + FOX_HEADS
PREP_ROWS = 256
SMALL_FOX_LANE = GLA_RANK
ROUTER_EXPERT_LANE = N_GROUPS

VMEM_LIMIT = 56 * 1024 * 1024

ROW_TILE = 256
MM_TM = 1024
MM_TN = 512
GLA_TS = 256
FOX_TQ = 256
FOX_TK = 256
FOX_VT_BLK = 512
CONV_RC = 64
ROUTE_TR = 512
MOE_TM = 256
MOE_SLOTS = 3
MOE_DRAIN_STEPS = 2


def _cparams(sem):
    return pltpu.CompilerParams(dimension_semantics=sem, vmem_limit_bytes=VMEM_LIMIT)


def _split_bf16(x):
    hi = x.astype(BF16)
    lo = (x - hi.astype(F32)).astype(BF16)
    return hi, lo


def _dot(a, b):
    return jnp.dot(a, b, preferred_element_type=F32)


def _dot3(a, b_hi, b_lo):
    a_hi, a_lo = _split_bf16(a)
    return _dot(a_hi, b_hi) + _dot(a_lo, b_hi) + _dot(a_hi, b_lo)


def _sigmoid(x):
    return 1.0 / (1.0 + jnp.exp(-x))


def _log_sigmoid(x):
    return jnp.minimum(x, 0.0) - jnp.log(1.0 + jnp.exp(-jnp.abs(x)))


def _norm_body(*refs, combine, project):
    it = iter(refs)
    x_ref = next(it)
    if combine:
        ya_ref, yb_ref, gates_ref = next(it), next(it), next(it)
    g_ref = next(it)
    if project:
        ws_hi_ref, ws_lo_ref = next(it), next(it)
    if combine and project:
        xo_ref = next(it)
    h_ref = next(it)
    if project:
        small_ref = next(it)

    x = x_ref[...]
    if combine:
        gates = gates_ref[...]
        x = x + gates[:, 0:1] * ya_ref[...] + gates[:, 1:2] * yb_ref[...]
        if project:
            xo_ref[...] = x
    y = x * lax.rsqrt(jnp.mean(x * x, axis=-1, keepdims=True) + EPS) * g_ref[...]
    h_ref[...] = y.astype(h_ref.dtype)
    if project:
        small_ref[...] = _dot3(y, ws_hi_ref[...], ws_lo_ref[...])


def _norm_call(x, g_row, *, moe=None, small_w=None, out_dtype=BF16):
    T = x.shape[0]
    tm = ROW_TILE
    combine = moe is not None
    project = small_w is not None
    row_spec = pl.BlockSpec((tm, D_MODEL), lambda i: (i, 0))
    lane_spec = pl.BlockSpec((tm, LANES), lambda i: (i, 0))
    const = lambda shape: pl.BlockSpec(shape, lambda i: (0, 0))
    nblk = T // tm
    ins, in_specs = [x], [row_spec]
    if combine:
        y2, gates = moe
        ins += [y2, y2, gates]
        in_specs += [row_spec, pl.BlockSpec((tm, D_MODEL), lambda i: (i + nblk, 0)), lane_spec]
    ins.append(g_row)
    in_specs.append(const((1, D_MODEL)))
    if project:
        ws_hi, ws_lo, layer = small_w
        ins += [ws_hi, ws_lo]
        in_specs += [pl.BlockSpec((None, D_MODEL, LANES), lambda i: (layer, 0, 0))] * 2
    out_shape, out_specs = [], []
    if combine and project:
        out_shape.append(jax.ShapeDtypeStruct((T, D_MODEL), F32))
        out_specs.append(row_spec)
    out_shape.append(jax.ShapeDtypeStruct((T, D_MODEL), out_dtype))
    out_specs.append(row_spec)
    if project:
        out_shape.append(jax.ShapeDtypeStruct((T, LANES), F32))
        out_specs.append(lane_spec)
    return pl.pallas_call(
        functools.partial(_norm_body, combine=combine, project=project),
        grid=(nblk,),
        in_specs=in_specs,
        out_specs=out_specs,
        out_shape=out_shape,
        compiler_params=_cparams(("parallel",)),
        name="norm",
    )(*ins)


def _prep_body(win_ref, wout_ref, wp_ref, ws_hi_ref, ws_lo_ref, wo_ref):
    rows = win_ref.shape[0]
    wp_ref[:, 0:D_MAIN] = win_ref[:, 0:D_MAIN].astype(BF16)
    tail = win_ref[:, D_MAIN:D_IN]
    wp_ref[:, D_MAIN:D_PROJ] = tail[:, GLA_RANK:GLA_RANK + 3 * D_FOX].astype(BF16)
    small = jnp.concatenate([tail[:, 0:GLA_RANK], tail[:, GLA_RANK + 3 * D_FOX:],
                             jnp.zeros((rows, LANES - GLA_RANK - FOX_HEADS), F32)], axis=1)
    hi, lo = _split_bf16(small)
    ws_hi_ref[...] = hi
    ws_lo_ref[...] = lo
    wo_ref[...] = wout_ref[...].astype(BF16)


def _prep_call(w_in, w_out):
    depth = w_in.shape[0]
    tr = PREP_ROWS
    idx = lambda l, i: (l, i, 0)
    return pl.pallas_call(
        _prep_body,
        grid=(depth, D_MODEL // tr),
        in_specs=[pl.BlockSpec((None, tr, D_IN), idx),
                  pl.BlockSpec((None, tr, D_MODEL), idx)],
        out_specs=[pl.BlockSpec((None, tr, D_PROJ), idx),
                   pl.BlockSpec((None, tr, LANES), idx),
                   pl.BlockSpec((None, tr, LANES), idx),
                   pl.BlockSpec((None, tr, D_MODEL), idx)],
        out_shape=[jax.ShapeDtypeStruct((depth, D_MODEL, D_PROJ), BF16),
                   jax.ShapeDtypeStruct((depth, D_MODEL, LANES), BF16),
                   jax.ShapeDtypeStruct((depth, D_MODEL, LANES), BF16),
                   jax.ShapeDtypeStruct((depth, D_MODEL, D_MODEL), BF16)],
        compiler_params=_cparams(("parallel", "parallel")),
        name="weight_prep",
    )(w_in, w_out)


def _matmul_body(h_ref, w_ref, o_ref):
    o_ref[...] = _dot(h_ref[...], w_ref[...]).astype(o_ref.dtype)


def _inproj_call(h, w_all, layer):
    T = h.shape[0]
    tm = min(MM_TM, T)
    return pl.pallas_call(
        _matmul_body,
        grid=(T // tm, D_PROJ // MM_TN),
        in_specs=[pl.BlockSpec((tm, D_MODEL), lambda i, j: (i, 0)),
                  pl.BlockSpec((None, D_MODEL, MM_TN), lambda i, j: (layer, 0, j))],
        out_specs=pl.BlockSpec((tm, MM_TN), lambda i, j: (i, j)),
        out_shape=jax.ShapeDtypeStruct((T, D_PROJ), BF16),
        compiler_params=_cparams(("parallel", "parallel")),
        name="inproj",
    )(h, w_all)


CONV_PAD = 32


def _conv_body(a_ref, g_ref, w_ref, b_ref, lng_ref, lnb_ref, o_ref, u_ref):
    S = a_ref.shape[0]
    u_ref[0:CONV_PAD, :] = jnp.zeros((CONV_PAD, D_CONV), F32)
    u_ref[CONV_PAD:CONV_PAD + S, :] = a_ref[...].astype(F32) * _sigmoid(g_ref[...].astype(F32))
    bias = b_ref[...]
    lng = lng_ref[...]
    lnb = lnb_ref[...]
    first = CONV_PAD - (CONV_WIDTH - 1)

    def chunk(c, carry):
        r0 = pl.multiple_of(c * CONV_RC, CONV_RC)
        acc = jnp.broadcast_to(bias, (CONV_RC, D_CONV))
        win = u_ref[pl.ds(r0, CONV_RC + CONV_PAD), :]
        for j in range(CONV_WIDTH):
            acc = acc + w_ref[j:j + 1, :] * win[first + j:first + j + CONV_RC, :]
        mu = jnp.mean(acc, axis=-1, keepdims=True)
        d = acc - mu
        var = jnp.mean(d * d, axis=-1, keepdims=True)
        yn = d * lax.rsqrt(var + EPS) * lng + lnb
        o_ref[pl.ds(r0, CONV_RC), :] = (yn * _sigmoid(yn)).astype(o_ref.dtype)
        return carry

    lax.fori_loop(0, S // CONV_RC, chunk, 0)


def _conv_call(proj, w_pad, b_row, lng_row, lnb_row, B, S):
    T = B * S
    const = lambda shape: pl.BlockSpec(shape, lambda b: (0, 0))
    return pl.pallas_call(
        _conv_body,
        grid=(B,),
        in_specs=[pl.BlockSpec((S, D_CONV), lambda b: (b, 0)),
                  pl.BlockSpec((S, D_CONV), lambda b: (b, 1)),
                  const((CONV_PAD, D_CONV)), const((1, D_CONV)), const((1, D_CONV)), const((1, D_CONV))],
        out_specs=pl.BlockSpec((S, D_CONV), lambda b: (b, 0)),
        out_shape=jax.ShapeDtypeStruct((T, D_CONV), BF16),
        scratch_shapes=[pltpu.VMEM((CONV_PAD + S, D_CONV), F32)],
        compiler_params=_cparams(("parallel",)),
        name="conv_mixer",
    )(proj, proj, w_pad, b_row, lng_row, lnb_row)


def _gla_body(q_ref, k_ref, v_ref, g_ref, low_ref, w2hi_ref, w2lo_ref, b2_ref, ng_ref, o_ref, st_ref):
    ts = q_ref.shape[0]
    nchunk = ts // GLA_CHUNK

    @pl.when(pl.program_id(1) == 0)
    def _():
        st_ref[...] = jnp.zeros(st_ref.shape, F32)

    la = _log_sigmoid(_dot3(low_ref[...], w2hi_ref[...], w2lo_ref[...]) + b2_ref[...]) * (1.0 / GLA_GATE_NORMALIZER)
    r = lax.broadcasted_iota(I32, (2 * ts, ts), 0)
    c = lax.broadcasted_iota(I32, (2 * ts, ts), 1)
    rr = jnp.where(r >= ts, r - ts, r)
    same_chunk = (rr // GLA_CHUNK) == (c // GLA_CHUNK)
    sel = jnp.where(same_chunk & ((r >= ts) | (c <= rr)), 1.0, 0.0).astype(BF16)
    la_hi, la_lo = _split_bf16(la)
    sums = _dot(sel, la_hi) + _dot(sel, la_lo)
    cum = sums[0:ts, :]
    last = sums[ts:2 * ts, :]
    e_q = jnp.exp(cum)
    e_inv = jnp.exp(-cum)
    e_end = jnp.exp(last - cum)
    e_last = jnp.exp(last)

    qr = lax.broadcasted_iota(I32, (ts, ts), 0)
    qc = lax.broadcasted_iota(I32, (ts, ts), 1)
    att_mask = ((qr // GLA_CHUNK) == (qc // GLA_CHUNK)) & (qc <= qr)
    ng = ng_ref[...]

    for h in range(GLA_HEADS):
        ks = slice(h * GLA_DK, (h + 1) * GLA_DK)
        vs = slice(h * GLA_DV, (h + 1) * GLA_DV)
        qh = q_ref[:, ks].astype(F32) * (GLA_DK ** -0.5)
        kh = k_ref[:, ks].astype(F32)
        vh = v_ref[:, vs]
        q_dec = (qh * e_q[:, ks]).astype(BF16)
        k_inv = (kh * e_inv[:, ks]).astype(BF16)
        k_end = (kh * e_end[:, ks]).astype(BF16)
        att = lax.dot_general(q_dec, k_inv, (((1,), (1,)), ((), ())), preferred_element_type=F32)
        att = jnp.where(att_mask, att, 0.0).astype(BF16)
        o_intra = _dot(att, vh)
        state = st_ref[h]
        outs = []
        for n in range(nchunk):
            rs = slice(n * GLA_CHUNK, (n + 1) * GLA_CHUNK)
            inter = lax.dot_general(q_dec[rs], state.astype(BF16), (((1,), (1,)), ((), ())),
                                    preferred_element_type=F32)
            outs.append(o_intra[rs] + inter)
            kv_t = lax.dot_general(vh[rs], k_end[rs], (((0,), (0,)), ((), ())), preferred_element_type=F32)
            state = state * e_last[n * GLA_CHUNK:n * GLA_CHUNK + 1, ks] + kv_t
        st_ref[h] = state
        o = jnp.concatenate(outs, axis=0)
        o = o * lax.rsqrt(jnp.mean(o * o, axis=-1, keepdims=True) + EPS) * ng
        gate = g_ref[:, vs].astype(F32)
        o_ref[:, vs] = (o * (gate * _sigmoid(gate))).astype(o_ref.dtype)


def _gla_call(proj, small, w2_hi, w2_lo, b2_row, ng_row, B, S):
    T = B * S
    ts = GLA_TS
    nst = S // ts
    row = lambda b, s: b * nst + s
    const = lambda shape: pl.BlockSpec(shape, lambda b, s: (0, 0))
    return pl.pallas_call(
        _gla_body,
        grid=(B, nst),
        in_specs=[pl.BlockSpec((ts, GLA_KEY), lambda b, s: (row(b, s), 2)),
                  pl.BlockSpec((ts, GLA_KEY), lambda b, s: (row(b, s), 3)),
                  pl.BlockSpec((ts, D_GLA), lambda b, s: (row(b, s), 2)),
                  pl.BlockSpec((ts, D_GLA), lambda b, s: (row(b, s), 3)),
                  pl.BlockSpec((ts, LANES), lambda b, s: (row(b, s), 0)),
                  const((LANES, GLA_KEY)), const((LANES, GLA_KEY)), const((1, GLA_KEY)), const((1, GLA_DV))],
        out_specs=pl.BlockSpec((ts, D_GLA), lambda b, s: (row(b, s), 0)),
        out_shape=jax.ShapeDtypeStruct((T, D_GLA), BF16),
        scratch_shapes=[pltpu.VMEM((GLA_HEADS, GLA_DV, GLA_DK), F32)],
        compiler_params=_cparams(("parallel", "arbitrary")),
        name="gla_mixer",
    )(proj, proj, proj, proj, small, w2_hi, w2_lo, b2_row, ng_row)


FGATE_BLK = 256


def _fgate_body(small_ref, fb_ref, fcol_ref, frow_ref):
    S = small_ref.shape[0]
    r = lax.broadcasted_iota(I32, (FGATE_BLK, FGATE_BLK), 0)
    c = lax.broadcasted_iota(I32, (FGATE_BLK, FGATE_BLK), 1)
    tri = jnp.where(c <= r, 1.0, 0.0).astype(BF16)
    carry = jnp.zeros((1, LANES), F32)
    for n in range(S // FGATE_BLK):
        rs = slice(n * FGATE_BLK, (n + 1) * FGATE_BLK)
        lf = _log_sigmoid(small_ref[rs, :] + fb_ref[...])
        p0 = lf.astype(BF16)
        r1 = lf - p0.astype(F32)
        p1 = r1.astype(BF16)
        p2 = (r1 - p1.astype(F32)).astype(BF16)
        blk = _dot(tri, p0) + _dot(tri, p1) + _dot(tri, p2) + carry
        fcol_ref[rs, :] = blk
        carry = blk[FGATE_BLK - 1:FGATE_BLK, :]
    ft = fcol_ref[...].T
    for h in range(FOX_HEADS):
        frow_ref[0, h] = ft[SMALL_FOX_LANE + h:SMALL_FOX_LANE + h + 1, :]


def _fgate_call(small, fb_row, B, S):
    T = B * S
    return pl.pallas_call(
        _fgate_body,
        grid=(B,),
        in_specs=[pl.BlockSpec((S, LANES), lambda b: (b, 0)),
                  pl.BlockSpec((1, LANES), lambda b: (0, 0))],
        out_specs=[pl.BlockSpec((S, LANES), lambda b: (b, 0)),
                   pl.BlockSpec((1, FOX_HEADS, 1, S), lambda b: (b, 0, 0, 0))],
        out_shape=[jax.ShapeDtypeStruct((T, LANES), F32),
                   jax.ShapeDtypeStruct((B, FOX_HEADS, 1, S), F32)],
        compiler_params=_cparams(("parallel",)),
        name="fox_gate",
    )(small, fb_row)


def _fox_body(q_ref, k_ref, v_ref, fcol_ref, frow_ref, o_ref, vt_ref, fb_ref, acc_ref):
    tq = q_ref.shape[0]
    tk = FOX_TK
    S = k_ref.shape[0]
    i = pl.program_id(1)

    @pl.when(i == 0)
    def _():
        for c in range(S // FOX_VT_BLK):
            cs = slice(c * FOX_VT_BLK, (c + 1) * FOX_VT_BLK)
            vt_ref[:, cs] = v_ref[cs, :].astype(F32).T.astype(BF16)
        for h in range(FOX_HEADS):
            fb_ref[h] = jnp.broadcast_to(fcol_ref[:, SMALL_FOX_LANE + h:SMALL_FOX_LANE + h + 1], (S, LANES))

    q0 = pl.multiple_of(i * tq, tq)
    key = lax.broadcasted_iota(I32, (tk, tq), 0)
    qry = lax.broadcasted_iota(I32, (tk, tq), 1)
    causal = key <= qry

    heads = []
    for h in range(FOX_HEADS):
        hs = slice(h * FOX_DH, (h + 1) * FOX_DH)
        qh = (q_ref[:, hs].astype(F32) * (FOX_DH ** -0.5)).astype(BF16)
        f_t = frow_ref[0, h, :, pl.ds(q0, tq)]
        heads.append((hs, qh, f_t))

    def update(j, h, state, masked):
        hs, qh, f_t = heads[h]
        m, l = state
        k0 = pl.multiple_of(j * tk, tk)
        kt = k_ref[pl.ds(k0, tk), hs]
        f_s = fb_ref[h, pl.ds(k0, tk), :]
        z = lax.dot_general(kt, qh, (((1,), (1,)), ((), ())), preferred_element_type=F32)
        z = z - jnp.concatenate([f_s] * (tq // LANES), axis=1)
        if masked:
            z = jnp.where(causal, z, -jnp.inf)
        m_new = jnp.maximum(m, jnp.max(z, axis=0, keepdims=True) + f_t)
        p = jnp.exp(z + (f_t - m_new))
        alpha = jnp.exp(m - m_new)
        l = alpha * l + jnp.sum(p, axis=0, keepdims=True)
        pv = _dot(vt_ref[hs, pl.ds(k0, tk)], p.astype(BF16))
        acc_ref[h] = alpha * acc_ref[h] + pv
        return m_new, l

    acc_ref[...] = jnp.zeros(acc_ref.shape, F32)
    init = (jnp.full((1, tq), -jnp.inf, F32), jnp.zeros((1, tq), F32))
    states = tuple(update(i, h, init, True) for h in range(FOX_HEADS))

    def step(j, states):
        return tuple(update(j, h, states[h], False) for h in range(FOX_HEADS))

    states = lax.fori_loop(0, i, step, states)
    for h in range(FOX_HEADS):
        m, l = states[h]
        o_ref[:, heads[h][0]] = (acc_ref[h] / l).T.astype(o_ref.dtype)


def _fox_call(proj, fcol, frow, B, S):
    T = B * S
    tq = FOX_TQ
    nq = S // tq
    col0 = D_MAIN // D_FOX
    return pl.pallas_call(
        _fox_body,
        grid=(B, nq),
        in_specs=[pl.BlockSpec((tq, D_FOX), lambda b, i: (b * nq + i, col0)),
                  pl.BlockSpec((S, D_FOX), lambda b, i: (b, col0 + 1)),
                  pl.BlockSpec((S, D_FOX), lambda b, i: (b, col0 + 2)),
                  pl.BlockSpec((S, LANES), lambda b, i: (b, 0)),
                  pl.BlockSpec((1, FOX_HEADS, 1, S), lambda b, i: (b, 0, 0, 0))],
        out_specs=pl.BlockSpec((tq, D_FOX), lambda b, i: (b * nq + i, 0)),
        out_shape=jax.ShapeDtypeStruct((T, D_FOX), BF16),
        scratch_shapes=[pltpu.VMEM((D_FOX, S), BF16),
                        pltpu.VMEM((FOX_HEADS, S, LANES), F32),
                        pltpu.VMEM((FOX_HEADS, FOX_DH, tq), F32)],
        compiler_params=_cparams(("parallel", "arbitrary")),
        name="fox_mixer",
    )(proj, proj, proj, fcol, frow)


def _outproj_body(yc_ref, yg_ref, yf_ref, x_ref, w_ref, g_ref, wr_hi_ref, wr_lo_ref, br_ref,
                  xo_ref, h_ref, lg_ref):
    acc = x_ref[...]
    acc = acc + _dot(yc_ref[...], w_ref[0:D_CONV, :])
    acc = acc + _dot(yg_ref[...], w_ref[D_CONV:D_CONV + D_GLA, :])
    acc = acc + _dot(yf_ref[...], w_ref[D_CONV + D_GLA:D_MODEL, :])
    xo_ref[...] = acc
    hn = acc * lax.rsqrt(jnp.mean(acc * acc, axis=-1, keepdims=True) + EPS) * g_ref[...]
    h_ref[...] = hn
    lg_ref[...] = _dot3(hn, wr_hi_ref[...], wr_lo_ref[...]) + br_ref[...]


def _outproj_call(yc, yg, yf, x, w_all, layer, g_row, wr_hi, wr_lo, br_row):
    T = x.shape[0]
    tm = ROW_TILE
    const = lambda shape: pl.BlockSpec(shape, lambda i: (0, 0))
    rows = lambda width: pl.BlockSpec((tm, width), lambda i: (i, 0))
    return pl.pallas_call(
        _outproj_body,
        grid=(T // tm,),
        in_specs=[rows(D_CONV), rows(D_GLA), rows(D_FOX), rows(D_MODEL),
                  pl.BlockSpec((None, D_MODEL, D_MODEL), lambda i: (layer, 0, 0)), const((1, D_MODEL)),
                  const((D_MODEL, LANES)), const((D_MODEL, LANES)), const((1, LANES))],
        out_specs=[rows(D_MODEL), rows(D_MODEL), rows(LANES)],
        out_shape=[jax.ShapeDtypeStruct((T, D_MODEL), F32),
                   jax.ShapeDtypeStruct((T, D_MODEL), F32),
                   jax.ShapeDtypeStruct((T, LANES), F32)],
        compiler_params=_cparams(("parallel",)),
        name="outproj",
    )(yc, yg, yf, x, w_all, g_row, wr_hi, wr_lo, br_row)


def _router_body(lg_ref, ri_ref, rf_ref, cnt_ref, carry_ref):
    tr = lg_ref.shape[0]

    @pl.when(pl.program_id(0) == 0)
    def _():
        carry_ref[...] = jnp.zeros(carry_ref.shape, F32)

    lg = lg_ref[...]
    lane = lax.broadcasted_iota(I32, (tr, LANES), 1).astype(F32)
    big = float(LANES)
    neg = -jnp.inf

    is_g = lane < N_GROUPS
    gl = jnp.where(is_g, lg, neg)
    gmax = jnp.max(gl, axis=-1, keepdims=True)
    gexp = jnp.where(is_g, jnp.exp(lg - gmax), 0.0)
    gprob = gexp / jnp.sum(gexp, axis=-1, keepdims=True)
    gtop = jnp.max(gprob, axis=-1, keepdims=True)
    grp = jnp.min(jnp.where(is_g & (gprob == gtop), lane, big), axis=-1, keepdims=True)

    lo = ROUTER_EXPERT_LANE + grp * EXPERTS_PER_GROUP
    in_grp = (lane >= lo) & (lane < lo + EXPERTS_PER_GROUP)
    el = jnp.where(in_grp, lg, neg)
    v1 = jnp.max(el, axis=-1, keepdims=True)
    i1 = jnp.min(jnp.where(in_grp & (el == v1), lane, big), axis=-1, keepdims=True)
    rest = in_grp & (lane != i1)
    el2 = jnp.where(rest, lg, neg)
    v2 = jnp.max(el2, axis=-1, keepdims=True)
    i2 = jnp.min(jnp.where(rest & (el2 == v2), lane, big), axis=-1, keepdims=True)
    ex = jnp.exp(v2 - v1)
    p1 = 1.0 / (1.0 + ex)
    p2 = ex / (1.0 + ex)

    hit1 = lane == i1
    hit2 = lane == i2
    onehot = jnp.where(hit1 | hit2, 1.0, 0.0)
    r = lax.broadcasted_iota(I32, (tr, tr), 0)
    c = lax.broadcasted_iota(I32, (tr, tr), 1)
    strict = jnp.where(c < r, 1.0, 0.0).astype(BF16)
    before = _dot(strict, onehot.astype(BF16)) + carry_ref[...]
    rank1 = jnp.sum(jnp.where(hit1, before, 0.0), axis=-1, keepdims=True)
    rank2 = jnp.sum(jnp.where(hit2, before, 0.0), axis=-1, keepdims=True)
    carry_ref[...] = carry_ref[...] + jnp.sum(onehot, axis=0, keepdims=True)
    cnt_ref[...] = carry_ref[...]

    e1 = i1 - ROUTER_EXPERT_LANE
    e2 = i2 - ROUTER_EXPERT_LANE
    ri = jnp.where(lane == 0, e1, jnp.where(lane == 1, e2, jnp.where(lane == 2, rank1, jnp.where(lane == 3, rank2, 0.0))))
    ri_ref[...] = ri.astype(I32)
    rf_ref[...] = jnp.where(lane == 0, gtop * p1, jnp.where(lane == 1, gtop * p2, 0.0))


def _router_call(logits):
    T = logits.shape[0]
    tr = ROUTE_TR
    rows = pl.BlockSpec((tr, LANES), lambda i: (i, 0))
    return pl.pallas_call(
        _router_body,
        grid=(T // tr,),
        in_specs=[rows],
        out_specs=[rows, rows, pl.BlockSpec((1, LANES), lambda i: (0, 0))],
        out_shape=[jax.ShapeDtypeStruct((T, LANES), I32),
                   jax.ShapeDtypeStruct((T, LANES), F32),
                   jax.ShapeDtypeStruct((1, LANES), F32)],
        scratch_shapes=[pltpu.VMEM((1, LANES), F32)],
        compiler_params=_cparams(("arbitrary",)),
        name="router",
    )(logits)


def _moe_body(n_act_ref, tile_e_ref, tile_n_ref, first_ref, wslot_ref, next_e_ref, src_ref, dst_ref,
              h_hbm, wg_hbm, wu_hbm, wd_hbm, y_hbm,
              xbuf, ybuf, xs, wg_f, wu_f, wd_f, wg_b, wu_b, wd_b, gsem, ssem, wsem, *, expert0):
    k = pl.program_id(0)
    nv = tile_n_ref[k]
    n_act = n_act_ref[0]

    def slot_of(tile):
        return lax.rem(tile + MOE_SLOTS, MOE_SLOTS)

    def weight_copies(e, ws):
        return (pltpu.make_async_copy(wg_hbm.at[expert0 + e], wg_f.at[ws], wsem.at[ws]),
                pltpu.make_async_copy(wu_hbm.at[expert0 + e], wu_f.at[ws], wsem.at[ws]),
                pltpu.make_async_copy(wd_hbm.at[expert0 + e], wd_f.at[ws], wsem.at[ws]))

    def gather_row(tile, r, buf):
        return pltpu.make_async_copy(h_hbm.at[pl.ds(src_ref[tile * MOE_TM + r], 1)],
                                     xbuf.at[buf, pl.ds(r, 1)], gsem.at[buf])

    def scatter_row(tile, r, buf):
        return pltpu.make_async_copy(ybuf.at[buf, pl.ds(r, 1)],
                                     y_hbm.at[pl.ds(dst_ref[(tile + 1) * MOE_TM + r], 1)], ssem.at[buf])

    def wait_gather(buf):
        pltpu.make_async_copy(h_hbm.at[pl.ds(0, MOE_TM)], xbuf.at[buf], gsem.at[buf]).wait()

    def wait_scatter(buf):
        pltpu.make_async_copy(ybuf.at[buf], y_hbm.at[pl.ds(0, MOE_TM)], ssem.at[buf]).wait()

    @pl.when(k == 0)
    def _():
        ybuf[MOE_SLOTS - 1] = jnp.zeros((MOE_TM, D_MODEL), F32)
        for c in weight_copies(tile_e_ref[0], 0):
            c.start()

        def issue(r, carry):
            gather_row(0, r, 0).start()
            gather_row(1, r, 1).start()
            return carry

        lax.fori_loop(0, MOE_TM, issue, 0, unroll=8)

    @pl.when(nv > 0)
    def _():
        @pl.when(first_ref[k] == 1)
        def _():
            ws = wslot_ref[k]
            for c in weight_copies(tile_e_ref[k], ws):
                c.wait()
            nxt = next_e_ref[k]

            @pl.when(nxt >= 0)
            def _():
                for c in weight_copies(nxt, 1 - ws):
                    c.start()

            wg_b[...] = wg_f[ws].astype(BF16)
            wu_b[...] = wu_f[ws].astype(BF16)
            wd_b[...] = wd_f[ws].astype(BF16)

        def tile_step(cur):
            prv = (cur + MOE_SLOTS - 1) % MOE_SLOTS
            wait_gather(cur)
            xs[...] = xbuf[cur].astype(BF16)
            for r in range(MOE_TM):
                gather_row(k + 2, r, prv).start()
            for r in range(MOE_TM):
                scatter_row(k - 1, r, prv).start()
            xb = xs[...]
            gate = _dot(xb, wg_b[...])
            up = _dot(xb, wu_b[...])
            mid = (gate * _sigmoid(gate) * up).astype(BF16)
            ybuf[cur] = _dot(mid, wd_b[...])

        for s in range(MOE_SLOTS):
            pl.when(slot_of(k) == s)(functools.partial(tile_step, s))

        @pl.when(k > 0)
        def _():
            wait_scatter(slot_of(k - 2))

    @pl.when((k == n_act) | (k == n_act + 1))
    def _():
        wait_gather(slot_of(k))

        @pl.when(k == n_act)
        def _():
            def issue(r, carry):
                scatter_row(k - 1, r, slot_of(k - 1)).start()
                return carry

            lax.fori_loop(0, MOE_TM, issue, 0, unroll=8)

        wait_scatter(slot_of(k - 2))


def _moe_call(tables, h, wg, wu, wd, layer, n_tiles):
    T = h.shape[0]
    any_spec = pl.BlockSpec(memory_space=pl.ANY)
    grid_spec = pltpu.PrefetchScalarGridSpec(
        num_scalar_prefetch=len(tables),
        grid=(n_tiles + MOE_DRAIN_STEPS,),
        in_specs=[any_spec, any_spec, any_spec, any_spec],
        out_specs=any_spec,
        scratch_shapes=[pltpu.VMEM((MOE_SLOTS, MOE_TM, D_MODEL), F32),
                        pltpu.VMEM((MOE_SLOTS, MOE_TM, D_MODEL), F32),
                        pltpu.VMEM((MOE_TM, D_MODEL), BF16),
                        pltpu.VMEM((2, D_MODEL, D_EXPERT), F32),
                        pltpu.VMEM((2, D_MODEL, D_EXPERT), F32),
                        pltpu.VMEM((2, D_EXPERT, D_MODEL), F32),
                        pltpu.VMEM((D_MODEL, D_EXPERT), BF16),
                        pltpu.VMEM((D_MODEL, D_EXPERT), BF16),
                        pltpu.VMEM((D_EXPERT, D_MODEL), BF16),
                        pltpu.SemaphoreType.DMA((MOE_SLOTS,)),
                        pltpu.SemaphoreType.DMA((MOE_SLOTS,)),
                        pltpu.SemaphoreType.DMA((2,))],
    )
    return pl.pallas_call(
        functools.partial(_moe_body, expert0=layer * N_EXPERTS),
        grid_spec=grid_spec,
        out_shape=jax.ShapeDtypeStruct((2 * T + MOE_TM, D_MODEL), F32),
        compiler_params=_cparams(("arbitrary",)),
        name="moe_experts",
    )(*tables, h, wg, wu, wd)


def _tables_body(e1_ref, e2_ref, r1_ref, r2_ref, cnt_ref,
                 n_act_ref, tile_e_ref, tile_n_ref, first_ref, wslot_ref, next_e_ref, src_ref, dst_ref,
                 row0_ref, after_ref):
    T = e1_ref.shape[0]
    n_steps = tile_e_ref.shape[0]
    n_rows = src_ref.shape[0]

    nxt = jnp.int32(-1)
    for e in reversed(range(N_EXPERTS)):
        after_ref[e] = nxt
        nxt = jnp.where(cnt_ref[e] > 0, jnp.int32(e), nxt)

    def padding(t, carry):
        base = t * MOE_TM
        for r in range(MOE_TM):
            src_ref[base + r] = 0
            dst_ref[base + r] = 2 * T + r
        return carry

    lax.fori_loop(0, n_rows // MOE_TM, padding, 0)

    k = jnp.int32(0)
    order = jnp.int32(0)
    for e in range(N_EXPERTS):
        n = cnt_ref[e]
        nt = (n + (MOE_TM - 1)) // MOE_TM
        row0_ref[e] = k * MOE_TM

        def tile(i, carry, e=e, n=n, k=k, order=order):
            tile_e_ref[k + i] = e
            tile_n_ref[k + i] = jnp.minimum(n - i * MOE_TM, MOE_TM)
            first_ref[k + i] = (i == 0).astype(I32)
            wslot_ref[k + i] = order & 1
            next_e_ref[k + i] = after_ref[e]
            return carry

        lax.fori_loop(0, nt, tile, 0)
        k = k + nt
        order = order + (nt > 0).astype(I32)
    n_act_ref[0] = k

    def idle(i, carry):
        tile_e_ref[i] = 0
        tile_n_ref[i] = 0
        first_ref[i] = 0
        wslot_ref[i] = 0
        next_e_ref[i] = -1
        return carry

    lax.fori_loop(k, n_steps, idle, 0)

    def assign(t, carry):
        p1 = row0_ref[e1_ref[t]] + r1_ref[t]
        p2 = row0_ref[e2_ref[t]] + r2_ref[t]
        src_ref[p1] = t
        src_ref[p2] = t
        dst_ref[p1 + MOE_TM] = t
        dst_ref[p2 + MOE_TM] = T + t
        return carry

    lax.fori_loop(0, T, assign, 0, unroll=8)


def _route_tables(route_i, counts, T, n_tiles):
    n_steps = n_tiles + MOE_DRAIN_STEPS
    n_rows = n_steps * MOE_TM
    cnt = counts[0, ROUTER_EXPERT_LANE:ROUTER_EXPERT_LANE + N_EXPERTS].astype(I32)
    smem = pl.BlockSpec(memory_space=pltpu.SMEM)
    vec = lambda n: jax.ShapeDtypeStruct((n,), I32)
    return pl.pallas_call(
        _tables_body,
        in_specs=[smem] * 5,
        out_specs=[smem] * 8,
        out_shape=[vec(1)] + [vec(n_steps)] * 5 + [vec(n_rows)] * 2,
        scratch_shapes=[pltpu.SMEM((N_EXPERTS,), I32), pltpu.SMEM((N_EXPERTS,), I32)],
        name="route_tables",
    )(route_i[:, 0], route_i[:, 1], route_i[:, 2], route_i[:, 3], cnt)


def _pad_lanes(w, offset=0):
    return jnp.pad(w, ((0, 0), (offset, LANES - offset - w.shape[1])))


def kernel(x, norm1_g, w_in, conv_w, conv_b, conv_ln_g, conv_ln_b, gla_w2, gla_b2, gla_norm_g, fox_f_b, w_out, norm2_g, router_group_w, router_group_b, router_expert_w, router_expert_b, ffn_w_gate, ffn_w_up, ffn_w_down, final_norm_g):
    B, S, D = x.shape
    T = B * S
    depth = w_in.shape[0]
    n_tiles = (2 * T) // MOE_TM + N_EXPERTS

    wg_all = ffn_w_gate.reshape(depth * N_EXPERTS, D_MODEL, D_EXPERT)
    wu_all = ffn_w_up.reshape(depth * N_EXPERTS, D_MODEL, D_EXPERT)
    wd_all = ffn_w_down.reshape(depth * N_EXPERTS, D_EXPERT, D_MODEL)
    w_proj, ws_hi, ws_lo, w_o = _prep_call(w_in, w_out)

    xt = x.reshape(T, D)
    h, small = _norm_call(xt, norm1_g[0][None, :], small_w=(ws_hi, ws_lo, 0))
    for l in range(depth):
        proj = _inproj_call(h, w_proj, l)

        y_conv = _conv_call(proj, jnp.pad(conv_w[l], ((0, CONV_PAD - CONV_WIDTH), (0, 0))), conv_b[l][None, :],
                            conv_ln_g[l][None, :], conv_ln_b[l][None, :], B, S)
        w2_hi, w2_lo = _split_bf16(jnp.pad(gla_w2[l], ((0, LANES - GLA_RANK), (0, 0))))
        y_gla = _gla_call(proj, small, w2_hi, w2_lo, gla_b2[l][None, :], gla_norm_g[l][None, :], B, S)
        fcol, frow = _fgate_call(small, _pad_lanes(fox_f_b[l][None, :], SMALL_FOX_LANE), B, S)
        y_fox = _fox_call(proj, fcol, frow, B, S)

        w_route = jnp.concatenate([router_group_w[l],
                                   router_expert_w[l].transpose(1, 0, 2).reshape(D_MODEL, N_EXPERTS)], axis=1)
        wr_hi, wr_lo = _split_bf16(_pad_lanes(w_route))
        b_route = _pad_lanes(jnp.concatenate([router_group_b[l], router_expert_b[l].reshape(-1)])[None, :])
        xt, h2, logits = _outproj_call(y_conv, y_gla, y_fox, xt, w_o, l, norm2_g[l][None, :],
                                       wr_hi, wr_lo, b_route)

        route_i, gates, counts = _router_call(logits)
        tables = _route_tables(route_i, counts, T, n_tiles)
        y2 = _moe_call(tables, h2, wg_all, wu_all, wd_all, l, n_tiles)

        if l + 1 < depth:
            xt, h, small = _norm_call(xt, norm1_g[l + 1][None, :], moe=(y2, gates), small_w=(ws_hi, ws_lo, l + 1))
        else:
            (out,) = _norm_call(xt, final_norm_g[None, :], moe=(y2, gates), out_dtype=F32)
    return out.reshape(B, S, D)
```

```python
import functools

import jax
import jax.numpy as jnp
from jax import lax
from jax.experimental import pallas as pl
from jax.experimental.pallas import tpu as pltpu

F32 = jnp.float32
BF16 = jnp.bfloat16
I32 = jnp.int32

D_MODEL = 2048
EPS = 1e-6
D_CONV = 512
CONV_WIDTH = 31
D_GLA = 1024
GLA_HEADS = 4
GLA_DK = 128
GLA_DV = 256
GLA_KEY = GLA_HEADS * GLA_DK
GLA_RANK = 16
GLA_GATE_NORMALIZER = 16.0
GLA_CHUNK = 64
D_FOX = 512
FOX_HEADS = 4
FOX_DH = 128
N_GROUPS = 4
EXPERTS_PER_GROUP = 8
N_EXPERTS = N_GROUPS * EXPERTS_PER_GROUP
D_EXPERT = 512

LANES = 128
SUBLANES = 8
D_MAIN = 2 * D_CONV + 2 * GLA_KEY + 2 * D_GLA
D_PROJ = D_MAIN + 3 * D_FOX
D_IN = D_MAIN + GLA_RANK + 3 * D_FOX + FOX_HEADS
PREP_ROWS = 256
SMALL_FOX_LANE = GLA_RANK
ROUTER_EXPERT_LANE = N_GROUPS

VMEM_LIMIT = 56 * 1024 * 1024

ROW_TILE = 256
MM_TM = 1024
MM_TN = 512
GLA_TS = 256
FOX_TQ = 256
FOX_TK = 256
FOX_VT_BLK = 512
CONV_RC = 64
ROUTE_TR = 512
MOE_TM = 256
MOE_PITCH = 24
MOE_SLOTS = 3
MOE_DRAIN_STEPS = 2


def _cparams(sem):
    return pltpu.CompilerParams(dimension_semantics=sem, vmem_limit_bytes=VMEM_LIMIT)


def _split_bf16(x):
    hi = x.astype(BF16)
    lo = (x - hi.astype(F32)).astype(BF16)
    return hi, lo


def _dot(a, b):
    return jnp.dot(a, b, preferred_element_type=F32)


def _dot3(a, b_hi, b_lo):
    a_hi, a_lo = _split_bf16(a)
    return _dot(a_hi, b_hi) + _dot(a_lo, b_hi) + _dot(a_hi, b_lo)


def _sigmoid(x):
    return 1.0 / (1.0 + jnp.exp(-x))


def _log_sigmoid(x):
    return jnp.minimum(x, 0.0) - jnp.log(1.0 + jnp.exp(-jnp.abs(x)))


N_SLABS = D_MODEL // LANES


def _from_slabs(ref, *lead):
    return jnp.concatenate([ref[lead + (slice(None), c, slice(None))] for c in range(N_SLABS)], axis=1)


def _to_slabs(ref, value, *lead):
    for c in range(N_SLABS):
        ref[lead + (slice(None), c, slice(None))] = value[:, c * LANES:(c + 1) * LANES]


def _norm_body(*refs, combine, project):
    it = iter(refs)
    x_ref = next(it)
    if combine:
        ya_ref, yb_ref, gates_ref = next(it), next(it), next(it)
    g_ref = next(it)
    if project:
        ws_hi_ref, ws_lo_ref = next(it), next(it)
    if combine and project:
        xo_ref = next(it)
    h_ref = next(it)
    if project:
        small_ref = next(it)

    x = x_ref[...]
    if combine:
        gates = gates_ref[...]
        x = x + gates[:, 0:1] * _from_slabs(ya_ref) + gates[:, 1:2] * _from_slabs(yb_ref)
        if project:
            xo_ref[...] = x
    y = x * lax.rsqrt(jnp.mean(x * x, axis=-1, keepdims=True) + EPS) * g_ref[...]
    h_ref[...] = y.astype(h_ref.dtype)
    if project:
        small_ref[...] = _dot3(y, ws_hi_ref[...], ws_lo_ref[...])


def _norm_call(x, g_row, *, moe=None, small_w=None, out_dtype=BF16):
    T = x.shape[0]
    tm = ROW_TILE
    combine = moe is not None
    project = small_w is not None
    row_spec = pl.BlockSpec((tm, D_MODEL), lambda i: (i, 0))
    lane_spec = pl.BlockSpec((tm, LANES), lambda i: (i, 0))
    const = lambda shape: pl.BlockSpec(shape, lambda i: (0, 0))
    nblk = T // tm
    ins, in_specs = [x], [row_spec]
    if combine:
        y2, gates = moe
        ins += [y2, y2, gates]
        in_specs += [pl.BlockSpec((tm, N_SLABS, LANES), lambda i: (i, 0, 0)),
                     pl.BlockSpec((tm, N_SLABS, LANES), lambda i: (i + nblk, 0, 0)), lane_spec]
    ins.append(g_row)
    in_specs.append(const((1, D_MODEL)))
    if project:
        ws_hi, ws_lo, layer = small_w
        ins += [ws_hi, ws_lo]
        in_specs += [pl.BlockSpec((None, D_MODEL, LANES), lambda i: (layer, 0, 0))] * 2
    out_shape, out_specs = [], []
    if combine and project:
        out_shape.append(jax.ShapeDtypeStruct((T, D_MODEL), F32))
        out_specs.append(row_spec)
    out_shape.append(jax.ShapeDtypeStruct((T, D_MODEL), out_dtype))
    out_specs.append(row_spec)
    if project:
        out_shape.append(jax.ShapeDtypeStruct((T, LANES), F32))
        out_specs.append(lane_spec)
    return pl.pallas_call(
        functools.partial(_norm_body, combine=combine, project=project),
        grid=(nblk,),
        in_specs=in_specs,
        out_specs=out_specs,
        out_shape=out_shape,
        compiler_params=_cparams(("parallel",)),
        name="norm",
    )(*ins)


def _prep_body(win_ref, wout_ref, wp_ref, ws_hi_ref, ws_lo_ref, wo_ref):
    rows = win_ref.shape[0]
    wp_ref[:, 0:D_MAIN] = win_ref[:, 0:D_MAIN].astype(BF16)
    tail = win_ref[:, D_MAIN:D_IN]
    wp_ref[:, D_MAIN:D_PROJ] = tail[:, GLA_RANK:GLA_RANK + 3 * D_FOX].astype(BF16)
    small = jnp.concatenate([tail[:, 0:GLA_RANK], tail[:, GLA_RANK + 3 * D_FOX:],
                             jnp.zeros((rows, LANES - GLA_RANK - FOX_HEADS), F32)], axis=1)
    hi, lo = _split_bf16(small)
    ws_hi_ref[...] = hi
    ws_lo_ref[...] = lo
    wo_ref[...] = wout_ref[...].astype(BF16)


def _prep_call(w_in, w_out):
    depth = w_in.shape[0]
    tr = PREP_ROWS
    idx = lambda l, i: (l, i, 0)
    return pl.pallas_call(
        _prep_body,
        grid=(depth, D_MODEL // tr),
        in_specs=[pl.BlockSpec((None, tr, D_IN), idx),
                  pl.BlockSpec((None, tr, D_MODEL), idx)],
        out_specs=[pl.BlockSpec((None, tr, D_PROJ), idx),
                   pl.BlockSpec((None, tr, LANES), idx),
                   pl.BlockSpec((None, tr, LANES), idx),
                   pl.BlockSpec((None, tr, D_MODEL), idx)],
        out_shape=[jax.ShapeDtypeStruct((depth, D_MODEL, D_PROJ), BF16),
                   jax.ShapeDtypeStruct((depth, D_MODEL, LANES), BF16),
                   jax.ShapeDtypeStruct((depth, D_MODEL, LANES), BF16),
                   jax.ShapeDtypeStruct((depth, D_MODEL, D_MODEL), BF16)],
        compiler_params=_cparams(("parallel", "parallel")),
        name="weight_prep",
    )(w_in, w_out)


def _matmul_body(h_ref, w_ref, o_ref):
    o_ref[...] = _dot(h_ref[...], w_ref[...]).astype(o_ref.dtype)


def _inproj_call(h, w_all, layer):
    T = h.shape[0]
    tm = min(MM_TM, T)
    return pl.pallas_call(
        _matmul_body,
        grid=(T // tm, D_PROJ // MM_TN),
        in_specs=[pl.BlockSpec((tm, D_MODEL), lambda i, j: (i, 0)),
                  pl.BlockSpec((None, D_MODEL, MM_TN), lambda i, j: (layer, 0, j))],
        out_specs=pl.BlockSpec((tm, MM_TN), lambda i, j: (i, j)),
        out_shape=jax.ShapeDtypeStruct((T, D_PROJ), BF16),
        compiler_params=_cparams(("parallel", "parallel")),
        name="inproj",
    )(h, w_all)


CONV_PAD = 32


def _conv_body(a_ref, g_ref, w_ref, b_ref, lng_ref, lnb_ref, o_ref, u_ref):
    S = a_ref.shape[0]
    u_ref[0:CONV_PAD, :] = jnp.zeros((CONV_PAD, D_CONV), F32)
    u_ref[CONV_PAD:CONV_PAD + S, :] = a_ref[...].astype(F32) * _sigmoid(g_ref[...].astype(F32))
    bias = b_ref[...]
    lng = lng_ref[...]
    lnb = lnb_ref[...]
    first = CONV_PAD - (CONV_WIDTH - 1)

    def chunk(c, carry):
        r0 = pl.multiple_of(c * CONV_RC, CONV_RC)
        acc = jnp.broadcast_to(bias, (CONV_RC, D_CONV))
        win = u_ref[pl.ds(r0, CONV_RC + CONV_PAD), :]
        for j in range(CONV_WIDTH):
            acc = acc + w_ref[j:j + 1, :] * win[first + j:first + j + CONV_RC, :]
        mu = jnp.mean(acc, axis=-1, keepdims=True)
        d = acc - mu
        var = jnp.mean(d * d, axis=-1, keepdims=True)
        yn = d * lax.rsqrt(var + EPS) * lng + lnb
        o_ref[pl.ds(r0, CONV_RC), :] = (yn * _sigmoid(yn)).astype(o_ref.dtype)
        return carry

    lax.fori_loop(0, S // CONV_RC, chunk, 0)


def _conv_call(proj, w_pad, b_row, lng_row, lnb_row, B, S):
    T = B * S
    const = lambda shape: pl.BlockSpec(shape, lambda b: (0, 0))
    return pl.pallas_call(
        _conv_body,
        grid=(B,),
        in_specs=[pl.BlockSpec((S, D_CONV), lambda b: (b, 0)),
                  pl.BlockSpec((S, D_CONV), lambda b: (b, 1)),
                  const((CONV_PAD, D_CONV)), const((1, D_CONV)), const((1, D_CONV)), const((1, D_CONV))],
        out_specs=pl.BlockSpec((S, D_CONV), lambda b: (b, 0)),
        out_shape=jax.ShapeDtypeStruct((T, D_CONV), BF16),
        scratch_shapes=[pltpu.VMEM((CONV_PAD + S, D_CONV), F32)],
        compiler_params=_cparams(("parallel",)),
        name="conv_mixer",
    )(proj, proj, w_pad, b_row, lng_row, lnb_row)


def _gla_body(q_ref, k_ref, v_ref, g_ref, low_ref, w2hi_ref, w2lo_ref, b2_ref, ng_ref, o_ref, st_ref):
    ts = q_ref.shape[0]
    nchunk = ts // GLA_CHUNK

    @pl.when(pl.program_id(1) == 0)
    def _():
        st_ref[...] = jnp.zeros(st_ref.shape, F32)

    la = _log_sigmoid(_dot3(low_ref[...], w2hi_ref[...], w2lo_ref[...]) + b2_ref[...]) * (1.0 / GLA_GATE_NORMALIZER)
    r = lax.broadcasted_iota(I32, (2 * ts, ts), 0)
    c = lax.broadcasted_iota(I32, (2 * ts, ts), 1)
    rr = jnp.where(r >= ts, r - ts, r)
    same_chunk = (rr // GLA_CHUNK) == (c // GLA_CHUNK)
    sel = jnp.where(same_chunk & ((r >= ts) | (c <= rr)), 1.0, 0.0).astype(BF16)
    la_hi, la_lo = _split_bf16(la)
    sums = _dot(sel, la_hi) + _dot(sel, la_lo)
    cum = sums[0:ts, :]
    last = sums[ts:2 * ts, :]
    e_q = jnp.exp(cum)
    e_inv = jnp.exp(-cum)
    e_end = jnp.exp(last - cum)
    e_last = jnp.exp(last)

    qr = lax.broadcasted_iota(I32, (ts, ts), 0)
    qc = lax.broadcasted_iota(I32, (ts, ts), 1)
    att_mask = ((qr // GLA_CHUNK) == (qc // GLA_CHUNK)) & (qc <= qr)
    ng = ng_ref[...]

    for h in range(GLA_HEADS):
        ks = slice(h * GLA_DK, (h + 1) * GLA_DK)
        vs = slice(h * GLA_DV, (h + 1) * GLA_DV)
        qh = q_ref[:, ks].astype(F32) * (GLA_DK ** -0.5)
        kh = k_ref[:, ks].astype(F32)
        vh = v_ref[:, vs]
        q_dec = (qh * e_q[:, ks]).astype(BF16)
        k_inv = (kh * e_inv[:, ks]).astype(BF16)
        k_end = (kh * e_end[:, ks]).astype(BF16)
        att = lax.dot_general(q_dec, k_inv, (((1,), (1,)), ((), ())), preferred_element_type=F32)
        att = jnp.where(att_mask, att, 0.0).astype(BF16)
        o_intra = _dot(att, vh)
        state = st_ref[h]
        outs = []
        for n in range(nchunk):
            rs = slice(n * GLA_CHUNK, (n + 1) * GLA_CHUNK)
            inter = lax.dot_general(q_dec[rs], state.astype(BF16), (((1,), (1,)), ((), ())),
                                    preferred_element_type=F32)
            outs.append(o_intra[rs] + inter)
            kv_t = lax.dot_general(vh[rs], k_end[rs], (((0,), (0,)), ((), ())), preferred_element_type=F32)
            state = state * e_last[n * GLA_CHUNK:n * GLA_CHUNK + 1, ks] + kv_t
        st_ref[h] = state
        o = jnp.concatenate(outs, axis=0)
        o = o * lax.rsqrt(jnp.mean(o * o, axis=-1, keepdims=True) + EPS) * ng
        gate = g_ref[:, vs].astype(F32)
        o_ref[:, vs] = (o * (gate * _sigmoid(gate))).astype(o_ref.dtype)


def _gla_call(proj, small, w2_hi, w2_lo, b2_row, ng_row, B, S):
    T = B * S
    ts = GLA_TS
    nst = S // ts
    row = lambda b, s: b * nst + s
    const = lambda shape: pl.BlockSpec(shape, lambda b, s: (0, 0))
    return pl.pallas_call(
        _gla_body,
        grid=(B, nst),
        in_specs=[pl.BlockSpec((ts, GLA_KEY), lambda b, s: (row(b, s), 2)),
                  pl.BlockSpec((ts, GLA_KEY), lambda b, s: (row(b, s), 3)),
                  pl.BlockSpec((ts, D_GLA), lambda b, s: (row(b, s), 2)),
                  pl.BlockSpec((ts, D_GLA), lambda b, s: (row(b, s), 3)),
                  pl.BlockSpec((ts, LANES), lambda b, s: (row(b, s), 0)),
                  const((LANES, GLA_KEY)), const((LANES, GLA_KEY)), const((1, GLA_KEY)), const((1, GLA_DV))],
        out_specs=pl.BlockSpec((ts, D_GLA), lambda b, s: (row(b, s), 0)),
        out_shape=jax.ShapeDtypeStruct((T, D_GLA), BF16),
        scratch_shapes=[pltpu.VMEM((GLA_HEADS, GLA_DV, GLA_DK), F32)],
        compiler_params=_cparams(("parallel", "arbitrary")),
        name="gla_mixer",
    )(proj, proj, proj, proj, small, w2_hi, w2_lo, b2_row, ng_row)


FGATE_BLK = 256


def _fgate_body(small_ref, fb_ref, fcol_ref, frow_ref):
    S = small_ref.shape[0]
    r = lax.broadcasted_iota(I32, (FGATE_BLK, FGATE_BLK), 0)
    c = lax.broadcasted_iota(I32, (FGATE_BLK, FGATE_BLK), 1)
    tri = jnp.where(c <= r, 1.0, 0.0).astype(BF16)
    carry = jnp.zeros((1, LANES), F32)
    for n in range(S // FGATE_BLK):
        rs = slice(n * FGATE_BLK, (n + 1) * FGATE_BLK)
        lf = _log_sigmoid(small_ref[rs, :] + fb_ref[...])
        p0 = lf.astype(BF16)
        r1 = lf - p0.astype(F32)
        p1 = r1.astype(BF16)
        p2 = (r1 - p1.astype(F32)).astype(BF16)
        blk = _dot(tri, p0) + _dot(tri, p1) + _dot(tri, p2) + carry
        fcol_ref[rs, :] = blk
        carry = blk[FGATE_BLK - 1:FGATE_BLK, :]
    ft = fcol_ref[...].T
    for h in range(FOX_HEADS):
        frow_ref[0, h] = ft[SMALL_FOX_LANE + h:SMALL_FOX_LANE + h + 1, :]


def _fgate_call(small, fb_row, B, S):
    T = B * S
    return pl.pallas_call(
        _fgate_body,
        grid=(B,),
        in_specs=[pl.BlockSpec((S, LANES), lambda b: (b, 0)),
                  pl.BlockSpec((1, LANES), lambda b: (0, 0))],
        out_specs=[pl.BlockSpec((S, LANES), lambda b: (b, 0)),
                   pl.BlockSpec((1, FOX_HEADS, 1, S), lambda b: (b, 0, 0, 0))],
        out_shape=[jax.ShapeDtypeStruct((T, LANES), F32),
                   jax.ShapeDtypeStruct((B, FOX_HEADS, 1, S), F32)],
        compiler_params=_cparams(("parallel",)),
        name="fox_gate",
    )(small, fb_row)


def _fox_body(q_ref, k_ref, v_ref, fcol_ref, frow_ref, o_ref, vt_ref, fb_ref, acc_ref):
    tq = q_ref.shape[0]
    tk = FOX_TK
    S = k_ref.shape[0]
    i = pl.program_id(1)

    @pl.when(i == 0)
    def _():
        for c in range(S // FOX_VT_BLK):
            cs = slice(c * FOX_VT_BLK, (c + 1) * FOX_VT_BLK)
            vt_ref[:, cs] = v_ref[cs, :].astype(F32).T.astype(BF16)
        for h in range(FOX_HEADS):
            fb_ref[h] = jnp.broadcast_to(fcol_ref[:, SMALL_FOX_LANE + h:SMALL_FOX_LANE + h + 1], (S, LANES))

    q0 = pl.multiple_of(i * tq, tq)
    key = lax.broadcasted_iota(I32, (tk, tq), 0)
    qry = lax.broadcasted_iota(I32, (tk, tq), 1)
    causal = key <= qry

    heads = []
    for h in range(FOX_HEADS):
        hs = slice(h * FOX_DH, (h + 1) * FOX_DH)
        qh = (q_ref[:, hs].astype(F32) * (FOX_DH ** -0.5)).astype(BF16)
        f_t = frow_ref[0, h, :, pl.ds(q0, tq)]
        heads.append((hs, qh, f_t))

    def update(j, h, state, masked):
        hs, qh, f_t = heads[h]
        m, l = state
        k0 = pl.multiple_of(j * tk, tk)
        kt = k_ref[pl.ds(k0, tk), hs]
        f_s = fb_ref[h, pl.ds(k0, tk), :]
        z = lax.dot_general(kt, qh, (((1,), (1,)), ((), ())), preferred_element_type=F32)
        z = z - jnp.concatenate([f_s] * (tq // LANES), axis=1)
        if masked:
            z = jnp.where(causal, z, -jnp.inf)
        m_new = jnp.maximum(m, jnp.max(z, axis=0, keepdims=True) + f_t)
        p = jnp.exp(z + (f_t - m_new))
        alpha = jnp.exp(m - m_new)
        l = alpha * l + jnp.sum(p, axis=0, keepdims=True)
        pv = _dot(vt_ref[hs, pl.ds(k0, tk)], p.astype(BF16))
        acc_ref[h] = alpha * acc_ref[h] + pv
        return m_new, l

    acc_ref[...] = jnp.zeros(acc_ref.shape, F32)
    init = (jnp.full((1, tq), -jnp.inf, F32), jnp.zeros((1, tq), F32))
    states = tuple(update(i, h, init, True) for h in range(FOX_HEADS))

    def step(j, states):
        return tuple(update(j, h, states[h], False) for h in range(FOX_HEADS))

    states = lax.fori_loop(0, i, step, states)
    for h in range(FOX_HEADS):
        m, l = states[h]
        o_ref[:, heads[h][0]] = (acc_ref[h] / l).T.astype(o_ref.dtype)


def _fox_call(proj, fcol, frow, B, S):
    T = B * S
    tq = FOX_TQ
    nq = S // tq
    col0 = D_MAIN // D_FOX
    return pl.pallas_call(
        _fox_body,
        grid=(B, nq),
        in_specs=[pl.BlockSpec((tq, D_FOX), lambda b, i: (b * nq + i, col0)),
                  pl.BlockSpec((S, D_FOX), lambda b, i: (b, col0 + 1)),
                  pl.BlockSpec((S, D_FOX), lambda b, i: (b, col0 + 2)),
                  pl.BlockSpec((S, LANES), lambda b, i: (b, 0)),
                  pl.BlockSpec((1, FOX_HEADS, 1, S), lambda b, i: (b, 0, 0, 0))],
        out_specs=pl.BlockSpec((tq, D_FOX), lambda b, i: (b * nq + i, 0)),
        out_shape=jax.ShapeDtypeStruct((T, D_FOX), BF16),
        scratch_shapes=[pltpu.VMEM((D_FOX, S), BF16),
                        pltpu.VMEM((FOX_HEADS, S, LANES), F32),
                        pltpu.VMEM((FOX_HEADS, FOX_DH, tq), F32)],
        compiler_params=_cparams(("parallel", "arbitrary")),
        name="fox_mixer",
    )(proj, proj, proj, fcol, frow)


def _outproj_body(yc_ref, yg_ref, yf_ref, x_ref, w_ref, g_ref, wr_hi_ref, wr_lo_ref, br_ref,
                  xo_ref, h_ref, lg_ref):
    acc = x_ref[...]
    acc = acc + _dot(yc_ref[...], w_ref[0:D_CONV, :])
    acc = acc + _dot(yg_ref[...], w_ref[D_CONV:D_CONV + D_GLA, :])
    acc = acc + _dot(yf_ref[...], w_ref[D_CONV + D_GLA:D_MODEL, :])
    xo_ref[...] = acc
    hn = acc * lax.rsqrt(jnp.mean(acc * acc, axis=-1, keepdims=True) + EPS) * g_ref[...]
    _to_slabs(h_ref, hn)
    lg_ref[...] = _dot3(hn, wr_hi_ref[...], wr_lo_ref[...]) + br_ref[...]


def _outproj_call(yc, yg, yf, x, w_all, layer, g_row, wr_hi, wr_lo, br_row):
    T = x.shape[0]
    tm = ROW_TILE
    const = lambda shape: pl.BlockSpec(shape, lambda i: (0, 0))
    rows = lambda width: pl.BlockSpec((tm, width), lambda i: (i, 0))
    return pl.pallas_call(
        _outproj_body,
        grid=(T // tm,),
        in_specs=[rows(D_CONV), rows(D_GLA), rows(D_FOX), rows(D_MODEL),
                  pl.BlockSpec((None, D_MODEL, D_MODEL), lambda i: (layer, 0, 0)), const((1, D_MODEL)),
                  const((D_MODEL, LANES)), const((D_MODEL, LANES)), const((1, LANES))],
        out_specs=[rows(D_MODEL), pl.BlockSpec((tm, N_SLABS, LANES), lambda i: (i, 0, 0)), rows(LANES)],
        out_shape=[jax.ShapeDtypeStruct((T, D_MODEL), F32),
                   jax.ShapeDtypeStruct((T, N_SLABS, LANES), F32),
                   jax.ShapeDtypeStruct((T, LANES), F32)],
        compiler_params=_cparams(("parallel",)),
        name="outproj",
    )(yc, yg, yf, x, w_all, g_row, wr_hi, wr_lo, br_row)


def _router_body(lg_ref, ri_ref, rf_ref, cnt_ref, carry_ref):
    tr = lg_ref.shape[0]

    @pl.when(pl.program_id(0) == 0)
    def _():
        carry_ref[...] = jnp.zeros(carry_ref.shape, F32)

    lg = lg_ref[...]
    lane = lax.broadcasted_iota(I32, (tr, LANES), 1).astype(F32)
    big = float(LANES)
    neg = -jnp.inf

    is_g = lane < N_GROUPS
    gl = jnp.where(is_g, lg, neg)
    gmax = jnp.max(gl, axis=-1, keepdims=True)
    gexp = jnp.where(is_g, jnp.exp(lg - gmax), 0.0)
    gprob = gexp / jnp.sum(gexp, axis=-1, keepdims=True)
    gtop = jnp.max(gprob, axis=-1, keepdims=True)
    grp = jnp.min(jnp.where(is_g & (gprob == gtop), lane, big), axis=-1, keepdims=True)

    lo = ROUTER_EXPERT_LANE + grp * EXPERTS_PER_GROUP
    in_grp = (lane >= lo) & (lane < lo + EXPERTS_PER_GROUP)
    el = jnp.where(in_grp, lg, neg)
    v1 = jnp.max(el, axis=-1, keepdims=True)
    i1 = jnp.min(jnp.where(in_grp & (el == v1), lane, big), axis=-1, keepdims=True)
    rest = in_grp & (lane != i1)
    el2 = jnp.where(rest, lg, neg)
    v2 = jnp.max(el2, axis=-1, keepdims=True)
    i2 = jnp.min(jnp.where(rest & (el2 == v2), lane, big), axis=-1, keepdims=True)
    ex = jnp.exp(v2 - v1)
    p1 = 1.0 / (1.0 + ex)
    p2 = ex / (1.0 + ex)

    hit1 = lane == i1
    hit2 = lane == i2
    onehot = jnp.where(hit1 | hit2, 1.0, 0.0)
    r = lax.broadcasted_iota(I32, (tr, tr), 0)
    c = lax.broadcasted_iota(I32, (tr, tr), 1)
    strict = jnp.where(c < r, 1.0, 0.0).astype(BF16)
    before = _dot(strict, onehot.astype(BF16)) + carry_ref[...]
    rank1 = jnp.sum(jnp.where(hit1, before, 0.0), axis=-1, keepdims=True)
    rank2 = jnp.sum(jnp.where(hit2, before, 0.0), axis=-1, keepdims=True)
    carry_ref[...] = carry_ref[...] + jnp.sum(onehot, axis=0, keepdims=True)
    cnt_ref[...] = carry_ref[...]

    e1 = i1 - ROUTER_EXPERT_LANE
    e2 = i2 - ROUTER_EXPERT_LANE
    ri = jnp.where(lane == 0, e1, jnp.where(lane == 1, e2, jnp.where(lane == 2, rank1, jnp.where(lane == 3, rank2, 0.0))))
    ri_ref[...] = ri.astype(I32)
    rf_ref[...] = jnp.where(lane == 0, gtop * p1, jnp.where(lane == 1, gtop * p2, 0.0))


def _router_call(logits):
    T = logits.shape[0]
    tr = ROUTE_TR
    rows = pl.BlockSpec((tr, LANES), lambda i: (i, 0))
    return pl.pallas_call(
        _router_body,
        grid=(T // tr,),
        in_specs=[rows],
        out_specs=[rows, rows, pl.BlockSpec((1, LANES), lambda i: (0, 0))],
        out_shape=[jax.ShapeDtypeStruct((T, LANES), I32),
                   jax.ShapeDtypeStruct((T, LANES), F32),
                   jax.ShapeDtypeStruct((1, LANES), F32)],
        scratch_shapes=[pltpu.VMEM((1, LANES), F32)],
        compiler_params=_cparams(("arbitrary",)),
        name="router",
    )(logits)


def _moe_body(n_act_ref, tile_e_ref, tile_n_ref, first_ref, wslot_ref, next_e_ref, src_ref, dst_ref,
              h_hbm, wg_hbm, wu_hbm, wd_hbm, y_hbm,
              xbuf, ybuf, xs, wg_f, wu_f, wd_f, wg_b, wu_b, wd_b, gsem, ssem, wsem, *, expert0):
    k = pl.program_id(0)
    nv = tile_n_ref[k]
    n_act = n_act_ref[0]

    def slot_of(tile):
        return lax.rem(tile + MOE_SLOTS, MOE_SLOTS)

    def weight_copies(e, ws):
        return (pltpu.make_async_copy(wg_hbm.at[expert0 + e], wg_f.at[ws], wsem.at[ws]),
                pltpu.make_async_copy(wu_hbm.at[expert0 + e], wu_f.at[ws], wsem.at[ws]),
                pltpu.make_async_copy(wd_hbm.at[expert0 + e], wd_f.at[ws], wsem.at[ws]))

    def row_window(buf_ref, buf, r):
        return buf_ref.at[buf, pl.ds(r * MOE_PITCH, N_SLABS)]

    def gather_row(tile, r, buf):
        return pltpu.make_async_copy(h_hbm.at[src_ref[tile * MOE_TM + r]], row_window(xbuf, buf, r), gsem.at[buf])

    def scatter_row(tile, r, buf):
        return pltpu.make_async_copy(row_window(ybuf, buf, r), y_hbm.at[dst_ref[(tile + 1) * MOE_TM + r]],
                                     ssem.at[buf])

    def wait_rows(sem):
        pltpu.make_async_copy(y_hbm.at[pl.ds(0, MOE_TM)], y_hbm.at[pl.ds(0, MOE_TM)], sem).wait()

    def wait_gather(buf):
        wait_rows(gsem.at[buf])

    def wait_scatter(buf):
        wait_rows(ssem.at[buf])

    def slab(c):
        return pl.ds(c, MOE_TM, stride=MOE_PITCH)

    @pl.when(k == 0)
    def _():
        ybuf[MOE_SLOTS - 1] = jnp.zeros((MOE_TM * MOE_PITCH, LANES), F32)
        for c in weight_copies(tile_e_ref[0], 0):
            c.start()

        def issue(r, carry):
            gather_row(0, r, 0).start()
            gather_row(1, r, 1).start()
            return carry

        lax.fori_loop(0, MOE_TM, issue, 0, unroll=8)

    @pl.when(nv > 0)
    def _():
        @pl.when(first_ref[k] == 1)
        def _():
            ws = wslot_ref[k]
            for c in weight_copies(tile_e_ref[k], ws):
                c.wait()
            nxt = next_e_ref[k]

            @pl.when(nxt >= 0)
            def _():
                for c in weight_copies(nxt, 1 - ws):
                    c.start()

            wg_b[...] = wg_f[ws].astype(BF16)
            wu_b[...] = wu_f[ws].astype(BF16)
            wd_b[...] = wd_f[ws].astype(BF16)

        def tile_step(cur):
            prv = (cur + MOE_SLOTS - 1) % MOE_SLOTS
            wait_gather(cur)
            for c in range(N_SLABS):
                xs[:, c * LANES:(c + 1) * LANES] = xbuf[cur, slab(c), :].astype(BF16)
            for r in range(MOE_TM):
                gather_row(k + 2, r, prv).start()
            for r in range(MOE_TM):
                scatter_row(k - 1, r, prv).start()
            xb = xs[...]
            gate = _dot(xb, wg_b[...])
            up = _dot(xb, wu_b[...])
            mid = (gate * _sigmoid(gate) * up).astype(BF16)
            y = _dot(mid, wd_b[...])
            for c in range(N_SLABS):
                ybuf[cur, slab(c), :] = y[:, c * LANES:(c + 1) * LANES]

        for s in range(MOE_SLOTS):
            pl.when(slot_of(k) == s)(functools.partial(tile_step, s))

        @pl.when(k > 0)
        def _():
            wait_scatter(slot_of(k - 2))

    @pl.when((k == n_act) | (k == n_act + 1))
    def _():
        wait_gather(slot_of(k))

        @pl.when(k == n_act)
        def _():
            def issue(r, carry):
                scatter_row(k - 1, r, slot_of(k - 1)).start()
                return carry

            lax.fori_loop(0, MOE_TM, issue, 0, unroll=8)

        wait_scatter(slot_of(k - 2))


def _moe_call(tables, h, wg, wu, wd, layer, n_tiles):
    T = h.shape[0]
    any_spec = pl.BlockSpec(memory_space=pl.ANY)
    grid_spec = pltpu.PrefetchScalarGridSpec(
        num_scalar_prefetch=len(tables),
        grid=(n_tiles + MOE_DRAIN_STEPS,),
        in_specs=[any_spec, any_spec, any_spec, any_spec],
        out_specs=any_spec,
        scratch_shapes=[pltpu.VMEM((MOE_SLOTS, MOE_TM * MOE_PITCH, LANES), F32),
                        pltpu.VMEM((MOE_SLOTS, MOE_TM * MOE_PITCH, LANES), F32),
                        pltpu.VMEM((MOE_TM, D_MODEL), BF16),
                        pltpu.VMEM((2, D_MODEL, D_EXPERT), F32),
                        pltpu.VMEM((2, D_MODEL, D_EXPERT), F32),
                        pltpu.VMEM((2, D_EXPERT, D_MODEL), F32),
                        pltpu.VMEM((D_MODEL, D_EXPERT), BF16),
                        pltpu.VMEM((D_MODEL, D_EXPERT), BF16),
                        pltpu.VMEM((D_EXPERT, D_MODEL), BF16),
                        pltpu.SemaphoreType.DMA((MOE_SLOTS,)),
                        pltpu.SemaphoreType.DMA((MOE_SLOTS,)),
                        pltpu.SemaphoreType.DMA((2,))],
    )
    return pl.pallas_call(
        functools.partial(_moe_body, expert0=layer * N_EXPERTS),
        grid_spec=grid_spec,
        out_shape=jax.ShapeDtypeStruct((2 * T + MOE_TM, N_SLABS, LANES), F32),
        compiler_params=_cparams(("arbitrary",)),
        name="moe_experts",
    )(*tables, h, wg, wu, wd)


def _tables_body(e1_ref, e2_ref, r1_ref, r2_ref, cnt_ref,
                 n_act_ref, tile_e_ref, tile_n_ref, first_ref, wslot_ref, next_e_ref, src_ref, dst_ref,
                 row0_ref, after_ref):
    T = e1_ref.shape[0]
    n_steps = tile_e_ref.shape[0]
    n_rows = src_ref.shape[0]

    nxt = jnp.int32(-1)
    for e in reversed(range(N_EXPERTS)):
        after_ref[e] = nxt
        nxt = jnp.where(cnt_ref[e] > 0, jnp.int32(e), nxt)

    def padding(t, carry):
        base = t * MOE_TM
        for r in range(MOE_TM):
            src_ref[base + r] = 0
            dst_ref[base + r] = 2 * T + r
        return carry

    lax.fori_loop(0, n_rows // MOE_TM, padding, 0)

    k = jnp.int32(0)
    order = jnp.int32(0)
    for e in range(N_EXPERTS):
        n = cnt_ref[e]
        nt = (n + (MOE_TM - 1)) // MOE_TM
        row0_ref[e] = k * MOE_TM

        def tile(i, carry, e=e, n=n, k=k, order=order):
            tile_e_ref[k + i] = e
            tile_n_ref[k + i] = jnp.minimum(n - i * MOE_TM, MOE_TM)
            first_ref[k + i] = (i == 0).astype(I32)
            wslot_ref[k + i] = order & 1
            next_e_ref[k + i] = after_ref[e]
            return carry

        lax.fori_loop(0, nt, tile, 0)
        k = k + nt
        order = order + (nt > 0).astype(I32)
    n_act_ref[0] = k

    def idle(i, carry):
        tile_e_ref[i] = 0
        tile_n_ref[i] = 0
        first_ref[i] = 0
        wslot_ref[i] = 0
        next_e_ref[i] = -1
        return carry

    lax.fori_loop(k, n_steps, idle, 0)

    def assign(t, carry):
        p1 = row0_ref[e1_ref[t]] + r1_ref[t]
        p2 = row0_ref[e2_ref[t]] + r2_ref[t]
        src_ref[p1] = t
        src_ref[p2] = t
        dst_ref[p1 + MOE_TM] = t
        dst_ref[p2 + MOE_TM] = T + t
        return carry

    lax.fori_loop(0, T, assign, 0, unroll=8)


def _route_tables(route_i, counts, T, n_tiles):
    n_steps = n_tiles + MOE_DRAIN_STEPS
    n_rows = n_steps * MOE_TM
    cnt = counts[0, ROUTER_EXPERT_LANE:ROUTER_EXPERT_LANE + N_EXPERTS].astype(I32)
    smem = pl.BlockSpec(memory_space=pltpu.SMEM)
    vec = lambda n: jax.ShapeDtypeStruct((n,), I32)
    return pl.pallas_call(
        _tables_body,
        in_specs=[smem] * 5,
        out_specs=[smem] * 8,
        out_shape=[vec(1)] + [vec(n_steps)] * 5 + [vec(n_rows)] * 2,
        scratch_shapes=[pltpu.SMEM((N_EXPERTS,), I32), pltpu.SMEM((N_EXPERTS,), I32)],
        name="route_tables",
    )(route_i[:, 0], route_i[:, 1], route_i[:, 2], route_i[:, 3], cnt)


def _pad_lanes(w, offset=0):
    return jnp.pad(w, ((0, 0), (offset, LANES - offset - w.shape[1])))


def kernel(x, norm1_g, w_in, conv_w, conv_b, conv_ln_g, conv_ln_b, gla_w2, gla_b2, gla_norm_g, fox_f_b, w_out, norm2_g, router_group_w, router_group_b, router_expert_w, router_expert_b, ffn_w_gate, ffn_w_up, ffn_w_down, final_norm_g):
    B, S, D = x.shape
    T = B * S
    depth = w_in.shape[0]
    n_tiles = (2 * T) // MOE_TM + N_EXPERTS

    wg_all = ffn_w_gate.reshape(depth * N_EXPERTS, D_MODEL, D_EXPERT)
    wu_all = ffn_w_up.reshape(depth * N_EXPERTS, D_MODEL, D_EXPERT)
    wd_all = ffn_w_down.reshape(depth * N_EXPERTS, D_EXPERT, D_MODEL)
    w_proj, ws_hi, ws_lo, w_o = _prep_call(w_in, w_out)

    xt = x.reshape(T, D)
    h, small = _norm_call(xt, norm1_g[0][None, :], small_w=(ws_hi, ws_lo, 0))
    for l in range(depth):
        proj = _inproj_call(h, w_proj, l)

        y_conv = _conv_call(proj, jnp.pad(conv_w[l], ((0, CONV_PAD - CONV_WIDTH), (0, 0))), conv_b[l][None, :],
                            conv_ln_g[l][None, :], conv_ln_b[l][None, :], B, S)
        w2_hi, w2_lo = _split_bf16(jnp.pad(gla_w2[l], ((0, LANES - GLA_RANK), (0, 0))))
        y_gla = _gla_call(proj, small, w2_hi, w2_lo, gla_b2[l][None, :], gla_norm_g[l][None, :], B, S)
        fcol, frow = _fgate_call(small, _pad_lanes(fox_f_b[l][None, :], SMALL_FOX_LANE), B, S)
        y_fox = _fox_call(proj, fcol, frow, B, S)

        w_route = jnp.concatenate([router_group_w[l],
                                   router_expert_w[l].transpose(1, 0, 2).reshape(D_MODEL, N_EXPERTS)], axis=1)
        wr_hi, wr_lo = _split_bf16(_pad_lanes(w_route))
        b_route = _pad_lanes(jnp.concatenate([router_group_b[l], router_expert_b[l].reshape(-1)])[None, :])
        xt, h2, logits = _outproj_call(y_conv, y_gla, y_fox, xt, w_o, l, norm2_g[l][None, :],
                                       wr_hi, wr_lo, b_route)

        route_i, gates, counts = _router_call(logits)
        tables = _route_tables(route_i, counts, T, n_tiles)
        y2 = _moe_call(tables, h2, wg_all, wu_all, wd_all, l, n_tiles)

        if l + 1 < depth:
            xt, h, small = _norm_call(xt, norm1_g[l + 1][None, :], moe=(y2, gates), small_w=(ws_hi, ws_lo, l + 1))
        else:
            (out,) = _norm_call(xt, final_norm_g[None, :], moe=(y2, gates), out_dtype=F32)
    return out.reshape(B, S, D)
```

```python
import functools

import jax
import jax.numpy as jnp
from jax import lax
from jax.experimental import pallas as pl
from jax.experimental.pallas import tpu as pltpu

F32 = jnp.float32
BF16 = jnp.bfloat16
I32 = jnp.int32

D_MODEL = 2048
EPS = 1e-6
D_CONV = 512
CONV_WIDTH = 31
D_GLA = 1024
GLA_HEADS = 4
GLA_DK = 128
GLA_DV = 256
GLA_KEY = GLA_HEADS * GLA_DK
GLA_RANK = 16
GLA_GATE_NORMALIZER = 16.0
GLA_CHUNK = 64
D_FOX = 512
FOX_HEADS = 4
FOX_DH = 128
N_GROUPS = 4
EXPERTS_PER_GROUP = 8
N_EXPERTS = N_GROUPS * EXPERTS_PER_GROUP
D_EXPERT = 512

LANES = 128
SUBLANES = 8
D_MAIN = 2 * D_CONV + 2 * GLA_KEY + 2 * D_GLA
D_PROJ = D_MAIN + 3 * D_FOX
D_IN = D_MAIN + GLA_RANK + 3 * D_FOX + FOX_HEADS
PREP_ROWS = 256
SMALL_FOX_LANE = GLA_RANK
ROUTER_EXPERT_LANE = N_GROUPS

VMEM_LIMIT = 56 * 1024 * 1024

ROW_TILE = 256
MM_TM = 1024
MM_TN = 1408
GLA_TS = 256
FOX_TQ = 256
FOX_TK = 256
FOX_VT_BLK = 512
CONV_RC = 64
ROUTE_TR = 512
MOE_TM = 256
MOE_SLOTS = 3
MOE_DRAIN_STEPS = 2


def _cparams(sem):
    return pltpu.CompilerParams(dimension_semantics=sem, vmem_limit_bytes=VMEM_LIMIT)


def _split_bf16(x):
    hi = x.astype(BF16)
    lo = (x - hi.astype(F32)).astype(BF16)
    return hi, lo


def _dot(a, b):
    return jnp.dot(a, b, preferred_element_type=F32)


def _dot3(a, b_hi, b_lo):
    a_hi, a_lo = _split_bf16(a)
    return _dot(a_hi, b_hi) + _dot(a_lo, b_hi) + _dot(a_hi, b_lo)


def _sigmoid(x):
    return 1.0 / (1.0 + jnp.exp(-x))


def _log_sigmoid(x):
    return jnp.minimum(x, 0.0) - jnp.log(1.0 + jnp.exp(-jnp.abs(x)))


def _norm_body(*refs, combine, project):
    it = iter(refs)
    x_ref = next(it)
    if combine:
        ya_ref, yb_ref, gates_ref = next(it), next(it), next(it)
    g_ref = next(it)
    if project:
        ws_hi_ref, ws_lo_ref = next(it), next(it)
    if combine and project:
        xo_ref = next(it)
    h_ref = next(it)
    if project:
        small_ref = next(it)

    x = x_ref[...]
    if combine:
        gates = gates_ref[...]
        x = x + gates[:, 0:1] * ya_ref[...] + gates[:, 1:2] * yb_ref[...]
        if project:
            xo_ref[...] = x
    y = x * lax.rsqrt(jnp.mean(x * x, axis=-1, keepdims=True) + EPS) * g_ref[...]
    h_ref[...] = y.astype(h_ref.dtype)
    if project:
        small_ref[...] = _dot3(y, ws_hi_ref[...], ws_lo_ref[...])


def _norm_call(x, g_row, *, moe=None, small_w=None, out_dtype=BF16):
    T = x.shape[0]
    tm = ROW_TILE
    combine = moe is not None
    project = small_w is not None
    row_spec = pl.BlockSpec((tm, D_MODEL), lambda i: (i, 0))
    lane_spec = pl.BlockSpec((tm, LANES), lambda i: (i, 0))
    const = lambda shape: pl.BlockSpec(shape, lambda i: (0, 0))
    nblk = T // tm
    ins, in_specs = [x], [row_spec]
    if combine:
        y2, gates = moe
        ins += [y2, y2, gates]
        in_specs += [row_spec, pl.BlockSpec((tm, D_MODEL), lambda i: (i + nblk, 0)), lane_spec]
    ins.append(g_row)
    in_specs.append(const((1, D_MODEL)))
    if project:
        ws_hi, ws_lo, layer = small_w
        ins += [ws_hi, ws_lo]
        in_specs += [pl.BlockSpec((None, D_MODEL, LANES), lambda i: (layer, 0, 0))] * 2
    out_shape, out_specs = [], []
    if combine and project:
        out_shape.append(jax.ShapeDtypeStruct((T, D_MODEL), F32))
        out_specs.append(row_spec)
    out_shape.append(jax.ShapeDtypeStruct((T, D_MODEL), out_dtype))
    out_specs.append(row_spec)
    if project:
        out_shape.append(jax.ShapeDtypeStruct((T, LANES), F32))
        out_specs.append(lane_spec)
    return pl.pallas_call(
        functools.partial(_norm_body, combine=combine, project=project),
        grid=(nblk,),
        in_specs=in_specs,
        out_specs=out_specs,
        out_shape=out_shape,
        compiler_params=_cparams(("parallel",)),
        name="norm",
    )(*ins)


def _prep_body(win_ref, wout_ref, wp_ref, ws_hi_ref, ws_lo_ref, wo_ref):
    rows = win_ref.shape[0]
    wp_ref[:, 0:D_MAIN] = win_ref[:, 0:D_MAIN].astype(BF16)
    tail = win_ref[:, D_MAIN:D_IN]
    wp_ref[:, D_MAIN:D_PROJ] = tail[:, GLA_RANK:GLA_RANK + 3 * D_FOX].astype(BF16)
    small = jnp.concatenate([tail[:, 0:GLA_RANK], tail[:, GLA_RANK + 3 * D_FOX:],
                             jnp.zeros((rows, LANES - GLA_RANK - FOX_HEADS), F32)], axis=1)
    hi, lo = _split_bf16(small)
    ws_hi_ref[...] = hi
    ws_lo_ref[...] = lo
    wo_ref[...] = wout_ref[...].astype(BF16)


def _prep_call(w_in, w_out):
    depth = w_in.shape[0]
    tr = PREP_ROWS
    idx = lambda l, i: (l, i, 0)
    return pl.pallas_call(
        _prep_body,
        grid=(depth, D_MODEL // tr),
        in_specs=[pl.BlockSpec((None, tr, D_IN), idx),
                  pl.BlockSpec((None, tr, D_MODEL), idx)],
        out_specs=[pl.BlockSpec((None, tr, D_PROJ), idx),
                   pl.BlockSpec((None, tr, LANES), idx),
                   pl.BlockSpec((None, tr, LANES), idx),
                   pl.BlockSpec((None, tr, D_MODEL), idx)],
        out_shape=[jax.ShapeDtypeStruct((depth, D_MODEL, D_PROJ), BF16),
                   jax.ShapeDtypeStruct((depth, D_MODEL, LANES), BF16),
                   jax.ShapeDtypeStruct((depth, D_MODEL, LANES), BF16),
                   jax.ShapeDtypeStruct((depth, D_MODEL, D_MODEL), BF16)],
        compiler_params=_cparams(("parallel", "parallel")),
        name="weight_prep",
    )(w_in, w_out)


def _matmul_body(h_ref, w_ref, o_ref):
    o_ref[...] = _dot(h_ref[...], w_ref[...]).astype(o_ref.dtype)


def _inproj_call(h, w_all, layer):
    T = h.shape[0]
    tm = min(MM_TM, T)
    return pl.pallas_call(
        _matmul_body,
        grid=(T // tm, D_PROJ // MM_TN),
        in_specs=[pl.BlockSpec((tm, D_MODEL), lambda i, j: (i, 0)),
                  pl.BlockSpec((None, D_MODEL, MM_TN), lambda i, j: (layer, 0, j))],
        out_specs=pl.BlockSpec((tm, MM_TN), lambda i, j: (i, j)),
        out_shape=jax.ShapeDtypeStruct((T, D_PROJ), BF16),
        compiler_params=_cparams(("parallel", "parallel")),
        name="inproj",
    )(h, w_all)


CONV_PAD = 32


def _conv_body(a_ref, g_ref, w_ref, b_ref, lng_ref, lnb_ref, o_ref, u_ref, sh_ref):
    S = a_ref.shape[0]
    u_ref[0:CONV_PAD, :] = jnp.zeros((CONV_PAD, D_CONV), F32)
    u_ref[CONV_PAD:CONV_PAD + S, :] = a_ref[...].astype(F32) * _sigmoid(g_ref[...].astype(F32))
    bias = b_ref[...]
    lng = lng_ref[...]
    lnb = lnb_ref[...]
    first = CONV_PAD - (CONV_WIDTH - 1)

    def chunk(c, carry):
        r0 = pl.multiple_of(c * CONV_RC, CONV_RC)
        acc = jnp.broadcast_to(bias, (CONV_RC, D_CONV))
        win = u_ref[pl.ds(r0, CONV_RC + CONV_PAD), :]
        for s in range(1, SUBLANES):
            sh_ref[s - 1] = win[s:s + CONV_RC + CONV_PAD - SUBLANES, :]
        for j in range(CONV_WIDTH):
            s = (first + j) % SUBLANES
            a = first + j - s
            tap = win[a:a + CONV_RC, :] if s == 0 else sh_ref[s - 1, a:a + CONV_RC, :]
            acc = acc + w_ref[j:j + 1, :] * tap
        mu = jnp.mean(acc, axis=-1, keepdims=True)
        d = acc - mu
        var = jnp.mean(d * d, axis=-1, keepdims=True)
        yn = d * lax.rsqrt(var + EPS) * lng + lnb
        o_ref[pl.ds(r0, CONV_RC), :] = (yn * _sigmoid(yn)).astype(o_ref.dtype)
        return carry

    lax.fori_loop(0, S // CONV_RC, chunk, 0)


def _conv_call(proj, w_pad, b_row, lng_row, lnb_row, B, S):
    T = B * S
    const = lambda shape: pl.BlockSpec(shape, lambda b: (0, 0))
    return pl.pallas_call(
        _conv_body,
        grid=(B,),
        in_specs=[pl.BlockSpec((S, D_CONV), lambda b: (b, 0)),
                  pl.BlockSpec((S, D_CONV), lambda b: (b, 1)),
                  const((CONV_PAD, D_CONV)), const((1, D_CONV)), const((1, D_CONV)), const((1, D_CONV))],
        out_specs=pl.BlockSpec((S, D_CONV), lambda b: (b, 0)),
        out_shape=jax.ShapeDtypeStruct((T, D_CONV), BF16),
        scratch_shapes=[pltpu.VMEM((CONV_PAD + S, D_CONV), F32),
                        pltpu.VMEM((SUBLANES - 1, CONV_RC + CONV_PAD - SUBLANES, D_CONV), F32)],
        compiler_params=_cparams(("parallel",)),
        name="conv_mixer",
    )(proj, proj, w_pad, b_row, lng_row, lnb_row)


def _gla_body(q_ref, k_ref, v_ref, g_ref, low_ref, w2hi_ref, w2lo_ref, b2_ref, ng_ref, o_ref, st_ref):
    ts = q_ref.shape[0]
    nchunk = ts // GLA_CHUNK

    @pl.when(pl.program_id(1) == 0)
    def _():
        st_ref[...] = jnp.zeros(st_ref.shape, F32)

    la = _log_sigmoid(_dot3(low_ref[...], w2hi_ref[...], w2lo_ref[...]) + b2_ref[...]) * (1.0 / GLA_GATE_NORMALIZER)
    r = lax.broadcasted_iota(I32, (2 * ts, ts), 0)
    c = lax.broadcasted_iota(I32, (2 * ts, ts), 1)
    rr = jnp.where(r >= ts, r - ts, r)
    same_chunk = (rr // GLA_CHUNK) == (c // GLA_CHUNK)
    sel = jnp.where(same_chunk & ((r >= ts) | (c <= rr)), 1.0, 0.0).astype(BF16)
    la_hi, la_lo = _split_bf16(la)
    sums = _dot(sel, la_hi) + _dot(sel, la_lo)
    cum = sums[0:ts, :]
    last = sums[ts:2 * ts, :]
    e_q = jnp.exp(cum)
    e_inv = jnp.exp(-cum)
    e_end = jnp.exp(last - cum)
    e_last = jnp.exp(last)

    qr = lax.broadcasted_iota(I32, (ts, ts), 0)
    qc = lax.broadcasted_iota(I32, (ts, ts), 1)
    att_mask = ((qr // GLA_CHUNK) == (qc // GLA_CHUNK)) & (qc <= qr)
    ng = ng_ref[...]

    for h in range(GLA_HEADS):
        ks = slice(h * GLA_DK, (h + 1) * GLA_DK)
        vs = slice(h * GLA_DV, (h + 1) * GLA_DV)
        qh = q_ref[:, ks].astype(F32) * (GLA_DK ** -0.5)
        kh = k_ref[:, ks].astype(F32)
        vh = v_ref[:, vs]
        q_dec = (qh * e_q[:, ks]).astype(BF16)
        k_inv = (kh * e_inv[:, ks]).astype(BF16)
        k_end = (kh * e_end[:, ks]).astype(BF16)
        att = lax.dot_general(q_dec, k_inv, (((1,), (1,)), ((), ())), preferred_element_type=F32)
        att = jnp.where(att_mask, att, 0.0).astype(BF16)
        o_intra = _dot(att, vh)
        state = st_ref[h]
        outs = []
        for n in range(nchunk):
            rs = slice(n * GLA_CHUNK, (n + 1) * GLA_CHUNK)
            inter = lax.dot_general(q_dec[rs], state.astype(BF16), (((1,), (1,)), ((), ())),
                                    preferred_element_type=F32)
            outs.append(o_intra[rs] + inter)
            kv_t = lax.dot_general(vh[rs], k_end[rs], (((0,), (0,)), ((), ())), preferred_element_type=F32)
            state = state * e_last[n * GLA_CHUNK:n * GLA_CHUNK + 1, ks] + kv_t
        st_ref[h] = state
        o = jnp.concatenate(outs, axis=0)
        o = o * lax.rsqrt(jnp.mean(o * o, axis=-1, keepdims=True) + EPS) * ng
        gate = g_ref[:, vs].astype(F32)
        o_ref[:, vs] = (o * (gate * _sigmoid(gate))).astype(o_ref.dtype)


def _gla_call(proj, small, w2_hi, w2_lo, b2_row, ng_row, B, S):
    T = B * S
    ts = GLA_TS
    nst = S // ts
    row = lambda b, s: b * nst + s
    const = lambda shape: pl.BlockSpec(shape, lambda b, s: (0, 0))
    return pl.pallas_call(
        _gla_body,
        grid=(B, nst),
        in_specs=[pl.BlockSpec((ts, GLA_KEY), lambda b, s: (row(b, s), 2)),
                  pl.BlockSpec((ts, GLA_KEY), lambda b, s: (row(b, s), 3)),
                  pl.BlockSpec((ts, D_GLA), lambda b, s: (row(b, s), 2)),
                  pl.BlockSpec((ts, D_GLA), lambda b, s: (row(b, s), 3)),
                  pl.BlockSpec((ts, LANES), lambda b, s: (row(b, s), 0)),
                  const((LANES, GLA_KEY)), const((LANES, GLA_KEY)), const((1, GLA_KEY)), const((1, GLA_DV))],
        out_specs=pl.BlockSpec((ts, D_GLA), lambda b, s: (row(b, s), 0)),
        out_shape=jax.ShapeDtypeStruct((T, D_GLA), BF16),
        scratch_shapes=[pltpu.VMEM((GLA_HEADS, GLA_DV, GLA_DK), F32)],
        compiler_params=_cparams(("parallel", "arbitrary")),
        name="gla_mixer",
    )(proj, proj, proj, proj, small, w2_hi, w2_lo, b2_row, ng_row)


FGATE_BLK = 256


def _fgate_body(small_ref, fb_ref, fcol_ref, frow_ref):
    S = small_ref.shape[0]
    r = lax.broadcasted_iota(I32, (FGATE_BLK, FGATE_BLK), 0)
    c = lax.broadcasted_iota(I32, (FGATE_BLK, FGATE_BLK), 1)
    tri = jnp.where(c <= r, 1.0, 0.0).astype(BF16)
    carry = jnp.zeros((1, LANES), F32)
    for n in range(S // FGATE_BLK):
        rs = slice(n * FGATE_BLK, (n + 1) * FGATE_BLK)
        lf = _log_sigmoid(small_ref[rs, :] + fb_ref[...])
        p0 = lf.astype(BF16)
        r1 = lf - p0.astype(F32)
        p1 = r1.astype(BF16)
        p2 = (r1 - p1.astype(F32)).astype(BF16)
        blk = _dot(tri, p0) + _dot(tri, p1) + _dot(tri, p2) + carry
        fcol_ref[rs, :] = blk
        carry = blk[FGATE_BLK - 1:FGATE_BLK, :]
    ft = fcol_ref[...].T
    for h in range(FOX_HEADS):
        frow_ref[0, h] = ft[SMALL_FOX_LANE + h:SMALL_FOX_LANE + h + 1, :]


def _fgate_call(small, fb_row, B, S):
    T = B * S
    return pl.pallas_call(
        _fgate_body,
        grid=(B,),
        in_specs=[pl.BlockSpec((S, LANES), lambda b: (b, 0)),
                  pl.BlockSpec((1, LANES), lambda b: (0, 0))],
        out_specs=[pl.BlockSpec((S, LANES), lambda b: (b, 0)),
                   pl.BlockSpec((1, FOX_HEADS, 1, S), lambda b: (b, 0, 0, 0))],
        out_shape=[jax.ShapeDtypeStruct((T, LANES), F32),
                   jax.ShapeDtypeStruct((B, FOX_HEADS, 1, S), F32)],
        compiler_params=_cparams(("parallel",)),
        name="fox_gate",
    )(small, fb_row)


def _fox_body(q_ref, k_ref, v_ref, fcol_ref, frow_ref, o_ref, vt_ref, fb_ref, acc_ref):
    tq = q_ref.shape[0]
    tk = FOX_TK
    S = k_ref.shape[0]
    i = pl.program_id(1)

    @pl.when(i == 0)
    def _():
        for c in range(S // FOX_VT_BLK):
            cs = slice(c * FOX_VT_BLK, (c + 1) * FOX_VT_BLK)
            vt_ref[:, cs] = v_ref[cs, :].astype(F32).T.astype(BF16)
        for h in range(FOX_HEADS):
            fb_ref[h] = jnp.broadcast_to(fcol_ref[:, SMALL_FOX_LANE + h:SMALL_FOX_LANE + h + 1], (S, LANES))

    q0 = pl.multiple_of(i * tq, tq)
    key = lax.broadcasted_iota(I32, (tk, tq), 0)
    qry = lax.broadcasted_iota(I32, (tk, tq), 1)
    causal = key <= qry

    heads = []
    for h in range(FOX_HEADS):
        hs = slice(h * FOX_DH, (h + 1) * FOX_DH)
        qh = (q_ref[:, hs].astype(F32) * (FOX_DH ** -0.5)).astype(BF16)
        f_t = frow_ref[0, h, :, pl.ds(q0, tq)]
        heads.append((hs, qh, f_t))

    def update(j, h, state, masked):
        hs, qh, f_t = heads[h]
        m, l = state
        k0 = pl.multiple_of(j * tk, tk)
        kt = k_ref[pl.ds(k0, tk), hs]
        f_s = fb_ref[h, pl.ds(k0, tk), :]
        z = lax.dot_general(kt, qh, (((1,), (1,)), ((), ())), preferred_element_type=F32)
        z = z - jnp.concatenate([f_s] * (tq // LANES), axis=1)
        if masked:
            z = jnp.where(causal, z, -jnp.inf)
        m_new = jnp.maximum(m, jnp.max(z, axis=0, keepdims=True) + f_t)
        p = jnp.exp(z + (f_t - m_new))
        alpha = jnp.exp(m - m_new)
        l = alpha * l + jnp.sum(p, axis=0, keepdims=True)
        pv = _dot(vt_ref[hs, pl.ds(k0, tk)], p.astype(BF16))
        acc_ref[h] = alpha * acc_ref[h] + pv
        return m_new, l

    acc_ref[...] = jnp.zeros(acc_ref.shape, F32)
    init = (jnp.full((1, tq), -jnp.inf, F32), jnp.zeros((1, tq), F32))
    states = tuple(update(i, h, init, True) for h in range(FOX_HEADS))

    def step(j, states):
        return tuple(update(j, h, states[h], False) for h in range(FOX_HEADS))

    states = lax.fori_loop(0, i, step, states)
    for h in range(FOX_HEADS):
        m, l = states[h]
        o_ref[:, heads[h][0]] = (acc_ref[h] / l).T.astype(o_ref.dtype)


def _fox_call(proj, fcol, frow, B, S):
    T = B * S
    tq = FOX_TQ
    nq = S // tq
    col0 = D_MAIN // D_FOX
    return pl.pallas_call(
        _fox_body,
        grid=(B, nq),
        in_specs=[pl.BlockSpec((tq, D_FOX), lambda b, i: (b * nq + i, col0)),
                  pl.BlockSpec((S, D_FOX), lambda b, i: (b, col0 + 1)),
                  pl.BlockSpec((S, D_FOX), lambda b, i: (b, col0 + 2)),
                  pl.BlockSpec((S, LANES), lambda b, i: (b, 0)),
                  pl.BlockSpec((1, FOX_HEADS, 1, S), lambda b, i: (b, 0, 0, 0))],
        out_specs=pl.BlockSpec((tq, D_FOX), lambda b, i: (b * nq + i, 0)),
        out_shape=jax.ShapeDtypeStruct((T, D_FOX), BF16),
        scratch_shapes=[pltpu.VMEM((D_FOX, S), BF16),
                        pltpu.VMEM((FOX_HEADS, S, LANES), F32),
                        pltpu.VMEM((FOX_HEADS, FOX_DH, tq), F32)],
        compiler_params=_cparams(("parallel", "arbitrary")),
        name="fox_mixer",
    )(proj, proj, proj, fcol, frow)


def _outproj_body(yc_ref, yg_ref, yf_ref, x_ref, w_ref, g_ref, wr_hi_ref, wr_lo_ref, br_ref,
                  xo_ref, h_ref, lg_ref):
    acc = x_ref[...]
    acc = acc + _dot(yc_ref[...], w_ref[0:D_CONV, :])
    acc = acc + _dot(yg_ref[...], w_ref[D_CONV:D_CONV + D_GLA, :])
    acc = acc + _dot(yf_ref[...], w_ref[D_CONV + D_GLA:D_MODEL, :])
    xo_ref[...] = acc
    hn = acc * lax.rsqrt(jnp.mean(acc * acc, axis=-1, keepdims=True) + EPS) * g_ref[...]
    h_ref[...] = hn
    lg_ref[...] = _dot3(hn, wr_hi_ref[...], wr_lo_ref[...]) + br_ref[...]


def _outproj_call(yc, yg, yf, x, w_all, layer, g_row, wr_hi, wr_lo, br_row):
    T = x.shape[0]
    tm = ROW_TILE
    const = lambda shape: pl.BlockSpec(shape, lambda i: (0, 0))
    rows = lambda width: pl.BlockSpec((tm, width), lambda i: (i, 0))
    return pl.pallas_call(
        _outproj_body,
        grid=(T // tm,),
        in_specs=[rows(D_CONV), rows(D_GLA), rows(D_FOX), rows(D_MODEL),
                  pl.BlockSpec((None, D_MODEL, D_MODEL), lambda i: (layer, 0, 0)), const((1, D_MODEL)),
                  const((D_MODEL, LANES)), const((D_MODEL, LANES)), const((1, LANES))],
        out_specs=[rows(D_MODEL), rows(D_MODEL), rows(LANES)],
        out_shape=[jax.ShapeDtypeStruct((T, D_MODEL), F32),
                   jax.ShapeDtypeStruct((T, D_MODEL), F32),
                   jax.ShapeDtypeStruct((T, LANES), F32)],
        compiler_params=_cparams(("parallel",)),
        name="outproj",
    )(yc, yg, yf, x, w_all, g_row, wr_hi, wr_lo, br_row)


def _router_body(lg_ref, ri_ref, rf_ref, cnt_ref, carry_ref):
    tr = lg_ref.shape[0]

    @pl.when(pl.program_id(0) == 0)
    def _():
        carry_ref[...] = jnp.zeros(carry_ref.shape, F32)

    lg = lg_ref[...]
    lane = lax.broadcasted_iota(I32, (tr, LANES), 1).astype(F32)
    big = float(LANES)
    neg = -jnp.inf

    is_g = lane < N_GROUPS
    gl = jnp.where(is_g, lg, neg)
    gmax = jnp.max(gl, axis=-1, keepdims=True)
    gexp = jnp.where(is_g, jnp.exp(lg - gmax), 0.0)
    gprob = gexp / jnp.sum(gexp, axis=-1, keepdims=True)
    gtop = jnp.max(gprob, axis=-1, keepdims=True)
    grp = jnp.min(jnp.where(is_g & (gprob == gtop), lane, big), axis=-1, keepdims=True)

    lo = ROUTER_EXPERT_LANE + grp * EXPERTS_PER_GROUP
    in_grp = (lane >= lo) & (lane < lo + EXPERTS_PER_GROUP)
    el = jnp.where(in_grp, lg, neg)
    v1 = jnp.max(el, axis=-1, keepdims=True)
    i1 = jnp.min(jnp.where(in_grp & (el == v1), lane, big), axis=-1, keepdims=True)
    rest = in_grp & (lane != i1)
    el2 = jnp.where(rest, lg, neg)
    v2 = jnp.max(el2, axis=-1, keepdims=True)
    i2 = jnp.min(jnp.where(rest & (el2 == v2), lane, big), axis=-1, keepdims=True)
    ex = jnp.exp(v2 - v1)
    p1 = 1.0 / (1.0 + ex)
    p2 = ex / (1.0 + ex)

    hit1 = lane == i1
    hit2 = lane == i2
    onehot = jnp.where(hit1 | hit2, 1.0, 0.0)
    r = lax.broadcasted_iota(I32, (tr, tr), 0)
    c = lax.broadcasted_iota(I32, (tr, tr), 1)
    strict = jnp.where(c < r, 1.0, 0.0).astype(BF16)
    before = _dot(strict, onehot.astype(BF16)) + carry_ref[...]
    rank1 = jnp.sum(jnp.where(hit1, before, 0.0), axis=-1, keepdims=True)
    rank2 = jnp.sum(jnp.where(hit2, before, 0.0), axis=-1, keepdims=True)
    carry_ref[...] = carry_ref[...] + jnp.sum(onehot, axis=0, keepdims=True)
    cnt_ref[...] = carry_ref[...]

    e1 = i1 - ROUTER_EXPERT_LANE
    e2 = i2 - ROUTER_EXPERT_LANE
    ri = jnp.where(lane == 0, e1, jnp.where(lane == 1, e2, jnp.where(lane == 2, rank1, jnp.where(lane == 3, rank2, 0.0))))
    ri_ref[...] = ri.astype(I32)
    rf_ref[...] = jnp.where(lane == 0, gtop * p1, jnp.where(lane == 1, gtop * p2, 0.0))


def _router_call(logits):
    T = logits.shape[0]
    tr = ROUTE_TR
    rows = pl.BlockSpec((tr, LANES), lambda i: (i, 0))
    return pl.pallas_call(
        _router_body,
        grid=(T // tr,),
        in_specs=[rows],
        out_specs=[rows, rows, pl.BlockSpec((1, LANES), lambda i: (0, 0))],
        out_shape=[jax.ShapeDtypeStruct((T, LANES), I32),
                   jax.ShapeDtypeStruct((T, LANES), F32),
                   jax.ShapeDtypeStruct((1, LANES), F32)],
        scratch_shapes=[pltpu.VMEM((1, LANES), F32)],
        compiler_params=_cparams(("arbitrary",)),
        name="router",
    )(logits)


def _moe_body(n_act_ref, tile_e_ref, tile_n_ref, first_ref, wslot_ref, next_e_ref, src_ref, dst_ref,
              h_hbm, wg_hbm, wu_hbm, wd_hbm, y_hbm,
              xbuf, ybuf, xs, wg_f, wu_f, wd_f, wg_b, wu_b, wd_b, gsem, ssem, wsem, *, expert0):
    k = pl.program_id(0)
    nv = tile_n_ref[k]
    n_act = n_act_ref[0]

    def slot_of(tile):
        return lax.rem(tile + MOE_SLOTS, MOE_SLOTS)

    def weight_copies(e, ws):
        return (pltpu.make_async_copy(wg_hbm.at[expert0 + e], wg_f.at[ws], wsem.at[ws]),
                pltpu.make_async_copy(wu_hbm.at[expert0 + e], wu_f.at[ws], wsem.at[ws]),
                pltpu.make_async_copy(wd_hbm.at[expert0 + e], wd_f.at[ws], wsem.at[ws]))

    def gather_row(tile, r, buf):
        return pltpu.make_async_copy(h_hbm.at[pl.ds(src_ref[tile * MOE_TM + r], 1)],
                                     xbuf.at[buf, pl.ds(r, 1)], gsem.at[buf])

    def scatter_row(tile, r, buf):
        return pltpu.make_async_copy(ybuf.at[buf, pl.ds(r, 1)],
                                     y_hbm.at[pl.ds(dst_ref[(tile + 1) * MOE_TM + r], 1)], ssem.at[buf])

    def wait_gather(buf):
        pltpu.make_async_copy(h_hbm.at[pl.ds(0, MOE_TM)], xbuf.at[buf], gsem.at[buf]).wait()

    def wait_scatter(buf):
        pltpu.make_async_copy(ybuf.at[buf], y_hbm.at[pl.ds(0, MOE_TM)], ssem.at[buf]).wait()

    @pl.when(k == 0)
    def _():
        ybuf[MOE_SLOTS - 1] = jnp.zeros((MOE_TM, D_MODEL), F32)
        for c in weight_copies(tile_e_ref[0], 0):
            c.start()

        def issue(r, carry):
            gather_row(0, r, 0).start()
            gather_row(1, r, 1).start()
            return carry

        lax.fori_loop(0, MOE_TM, issue, 0, unroll=8)

    @pl.when(nv > 0)
    def _():
        @pl.when(first_ref[k] == 1)
        def _():
            ws = wslot_ref[k]
            for c in weight_copies(tile_e_ref[k], ws):
                c.wait()
            nxt = next_e_ref[k]

            @pl.when(nxt >= 0)
            def _():
                for c in weight_copies(nxt, 1 - ws):
                    c.start(priority=1)

            wg_b[...] = wg_f[ws].astype(BF16)
            wu_b[...] = wu_f[ws].astype(BF16)
            wd_b[...] = wd_f[ws].astype(BF16)

        def tile_step(cur):
            prv = (cur + MOE_SLOTS - 1) % MOE_SLOTS
            wait_gather(cur)
            xs[...] = xbuf[cur].astype(BF16)
            for r in range(MOE_TM):
                gather_row(k + 2, r, prv).start(priority=0)
            for r in range(MOE_TM):
                scatter_row(k - 1, r, prv).start(priority=1)
            xb = xs[...]
            gate = _dot(xb, wg_b[...])
            up = _dot(xb, wu_b[...])
            mid = (gate * _sigmoid(gate) * up).astype(BF16)
            ybuf[cur] = _dot(mid, wd_b[...])

        for s in range(MOE_SLOTS):
            pl.when(slot_of(k) == s)(functools.partial(tile_step, s))

        @pl.when(k > 0)
        def _():
            wait_scatter(slot_of(k - 2))

    @pl.when((k == n_act) | (k == n_act + 1))
    def _():
        wait_gather(slot_of(k))

        @pl.when(k == n_act)
        def _():
            def issue(r, carry):
                scatter_row(k - 1, r, slot_of(k - 1)).start()
                return carry

            lax.fori_loop(0, MOE_TM, issue, 0, unroll=8)

        wait_scatter(slot_of(k - 2))


def _moe_call(tables, h, wg, wu, wd, layer, n_tiles):
    T = h.shape[0]
    any_spec = pl.BlockSpec(memory_space=pl.ANY)
    grid_spec = pltpu.PrefetchScalarGridSpec(
        num_scalar_prefetch=len(tables),
        grid=(n_tiles + MOE_DRAIN_STEPS,),
        in_specs=[any_spec, any_spec, any_spec, any_spec],
        out_specs=any_spec,
        scratch_shapes=[pltpu.VMEM((MOE_SLOTS, MOE_TM, D_MODEL), F32),
                        pltpu.VMEM((MOE_SLOTS, MOE_TM, D_MODEL), F32),
                        pltpu.VMEM((MOE_TM, D_MODEL), BF16),
                        pltpu.VMEM((2, D_MODEL, D_EXPERT), F32),
                        pltpu.VMEM((2, D_MODEL, D_EXPERT), F32),
                        pltpu.VMEM((2, D_EXPERT, D_MODEL), F32),
                        pltpu.VMEM((D_MODEL, D_EXPERT), BF16),
                        pltpu.VMEM((D_MODEL, D_EXPERT), BF16),
                        pltpu.VMEM((D_EXPERT, D_MODEL), BF16),
                        pltpu.SemaphoreType.DMA((MOE_SLOTS,)),
                        pltpu.SemaphoreType.DMA((MOE_SLOTS,)),
                        pltpu.SemaphoreType.DMA((2,))],
    )
    return pl.pallas_call(
        functools.partial(_moe_body, expert0=layer * N_EXPERTS),
        grid_spec=grid_spec,
        out_shape=jax.ShapeDtypeStruct((2 * T + MOE_TM, D_MODEL), F32),
        compiler_params=_cparams(("arbitrary",)),
        name="moe_experts",
    )(*tables, h, wg, wu, wd)


def _tables_body(e1_ref, e2_ref, r1_ref, r2_ref, cnt_ref,
                 n_act_ref, tile_e_ref, tile_n_ref, first_ref, wslot_ref, next_e_ref, src_ref, dst_ref,
                 row0_ref, after_ref):
    T = e1_ref.shape[0]
    n_steps = tile_e_ref.shape[0]
    n_rows = src_ref.shape[0]

    nxt = jnp.int32(-1)
    for e in reversed(range(N_EXPERTS)):
        after_ref[e] = nxt
        nxt = jnp.where(cnt_ref[e] > 0, jnp.int32(e), nxt)

    def padding(t, carry):
        base = t * MOE_TM
        for r in range(MOE_TM):
            src_ref[base + r] = 0
            dst_ref[base + r] = 2 * T + r
        return carry

    lax.fori_loop(0, n_rows // MOE_TM, padding, 0)

    k = jnp.int32(0)
    order = jnp.int32(0)
    for e in range(N_EXPERTS):
        n = cnt_ref[e]
        nt = (n + (MOE_TM - 1)) // MOE_TM
        row0_ref[e] = k * MOE_TM

        def tile(i, carry, e=e, n=n, k=k, order=order):
            tile_e_ref[k + i] = e
            tile_n_ref[k + i] = jnp.minimum(n - i * MOE_TM, MOE_TM)
            first_ref[k + i] = (i == 0).astype(I32)
            wslot_ref[k + i] = order & 1
            next_e_ref[k + i] = after_ref[e]
            return carry

        lax.fori_loop(0, nt, tile, 0)
        k = k + nt
        order = order + (nt > 0).astype(I32)
    n_act_ref[0] = k

    def idle(i, carry):
        tile_e_ref[i] = 0
        tile_n_ref[i] = 0
        first_ref[i] = 0
        wslot_ref[i] = 0
        next_e_ref[i] = -1
        return carry

    lax.fori_loop(k, n_steps, idle, 0)

    def assign(t, carry):
        p1 = row0_ref[e1_ref[t]] + r1_ref[t]
        p2 = row0_ref[e2_ref[t]] + r2_ref[t]
        src_ref[p1] = t
        src_ref[p2] = t
        dst_ref[p1 + MOE_TM] = t
        dst_ref[p2 + MOE_TM] = T + t
        return carry

    lax.fori_loop(0, T, assign, 0, unroll=8)


def _route_tables(route_i, counts, T, n_tiles):
    n_steps = n_tiles + MOE_DRAIN_STEPS
    n_rows = n_steps * MOE_TM
    cnt = counts[0, ROUTER_EXPERT_LANE:ROUTER_EXPERT_LANE + N_EXPERTS].astype(I32)
    smem = pl.BlockSpec(memory_space=pltpu.SMEM)
    vec = lambda n: jax.ShapeDtypeStruct((n,), I32)
    return pl.pallas_call(
        _tables_body,
        in_specs=[smem] * 5,
        out_specs=[smem] * 8,
        out_shape=[vec(1)] + [vec(n_steps)] * 5 + [vec(n_rows)] * 2,
        scratch_shapes=[pltpu.SMEM((N_EXPERTS,), I32), pltpu.SMEM((N_EXPERTS,), I32)],
        name="route_tables",
    )(route_i[:, 0], route_i[:, 1], route_i[:, 2], route_i[:, 3], cnt)


def _pad_lanes(w, offset=0):
    return jnp.pad(w, ((0, 0), (offset, LANES - offset - w.shape[1])))


def kernel(x, norm1_g, w_in, conv_w, conv_b, conv_ln_g, conv_ln_b, gla_w2, gla_b2, gla_norm_g, fox_f_b, w_out, norm2_g, router_group_w, router_group_b, router_expert_w, router_expert_b, ffn_w_gate, ffn_w_up, ffn_w_down, final_norm_g):
    B, S, D = x.shape
    T = B * S
    depth = w_in.shape[0]
    n_tiles = (2 * T) // MOE_TM + N_EXPERTS

    wg_all = ffn_w_gate.reshape(depth * N_EXPERTS, D_MODEL, D_EXPERT)
    wu_all = ffn_w_up.reshape(depth * N_EXPERTS, D_MODEL, D_EXPERT)
    wd_all = ffn_w_down.reshape(depth * N_EXPERTS, D_EXPERT, D_MODEL)
    w_proj, ws_hi, ws_lo, w_o = _prep_call(w_in, w_out)

    xt = x.reshape(T, D)
    h, small = _norm_call(xt, norm1_g[0][None, :], small_w=(ws_hi, ws_lo, 0))
    for l in range(depth):
        proj = _inproj_call(h, w_proj, l)

        y_conv = _conv_call(proj, jnp.pad(conv_w[l], ((0, CONV_PAD - CONV_WIDTH), (0, 0))), conv_b[l][None, :],
                            conv_ln_g[l][None, :], conv_ln_b[l][None, :], B, S)
        w2_hi, w2_lo = _split_bf16(jnp.pad(gla_w2[l], ((0, LANES - GLA_RANK), (0, 0))))
        y_gla = _gla_call(proj, small, w2_hi, w2_lo, gla_b2[l][None, :], gla_norm_g[l][None, :], B, S)
        fcol, frow = _fgate_call(small, _pad_lanes(fox_f_b[l][None, :], SMALL_FOX_LANE), B, S)
        y_fox = _fox_call(proj, fcol, frow, B, S)

        w_route = jnp.concatenate([router_group_w[l],
                                   router_expert_w[l].transpose(1, 0, 2).reshape(D_MODEL, N_EXPERTS)], axis=1)
        wr_hi, wr_lo = _split_bf16(_pad_lanes(w_route))
        b_route = _pad_lanes(jnp.concatenate([router_group_b[l], router_expert_b[l].reshape(-1)])[None, :])
        xt, h2, logits = _outproj_call(y_conv, y_gla, y_fox, xt, w_o, l, norm2_g[l][None, :],
                                       wr_hi, wr_lo, b_route)

        route_i, gates, counts = _router_call(logits)
        tables = _route_tables(route_i, counts, T, n_tiles)
        y2 = _moe_call(tables, h2, wg_all, wu_all, wd_all, l, n_tiles)

        if l + 1 < depth:
            xt, h, small = _norm_call(xt, norm1_g[l + 1][None, :], moe=(y2, gates), small_w=(ws_hi, ws_lo, l + 1))
        else:
            (out,) = _norm_call(xt, final_norm_g[None, :], moe=(y2, gates), out_dtype=F32)
    return out.reshape(B, S, D)
```

```python
import functools

import jax
import jax.numpy as jnp
from jax import lax
from jax.experimental import pallas as pl
from jax.experimental.pallas import tpu as pltpu

F32 = jnp.float32
BF16 = jnp.bfloat16
I32 = jnp.int32

D_MODEL = 2048
EPS = 1e-6
D_CONV = 512
CONV_WIDTH = 31
D_GLA = 1024
GLA_HEADS = 4
GLA_DK = 128
GLA_DV = 256
GLA_KEY = GLA_HEADS * GLA_DK
GLA_RANK = 16
GLA_GATE_NORMALIZER = 16.0
GLA_CHUNK = 64
D_FOX = 512
FOX_HEADS = 4
FOX_DH = 128
N_GROUPS = 4
EXPERTS_PER_GROUP = 8
N_EXPERTS = N_GROUPS * EXPERTS_PER_GROUP
D_EXPERT = 512

LANES = 128
SUBLANES = 8
D_MAIN = 2 * D_CONV + 2 * GLA_KEY + 2 * D_GLA
D_PROJ = D_MAIN + 3 * D_FOX
D_IN = D_MAIN + GLA_RANK + 3 * D_FOX + FOX_HEADS
PREP_ROWS = 256
SMALL_FOX_LANE = GLA_RANK
ROUTER_EXPERT_LANE = N_GROUPS

VMEM_LIMIT = 56 * 1024 * 1024

ROW_TILE = 256
MM_TM = 1024
MM_TN = 1408
GLA_TS = 256
FOX_TQ = 256
FOX_TK = 256
FOX_VT_BLK = 512
CONV_RC = 64
ROUTE_TR = 512
MOE_TM = 256
MOE_SLOTS = 3
MOE_DRAIN_STEPS = 2


def _cparams(sem):
    return pltpu.CompilerParams(dimension_semantics=sem, vmem_limit_bytes=VMEM_LIMIT)


def _split_bf16(x):
    hi = x.astype(BF16)
    lo = (x - hi.astype(F32)).astype(BF16)
    return hi, lo


def _dot(a, b):
    return jnp.dot(a, b, preferred_element_type=F32)


def _dot3(a, b_hi, b_lo):
    a_hi, a_lo = _split_bf16(a)
    return _dot(a_hi, b_hi) + _dot(a_lo, b_hi) + _dot(a_hi, b_lo)


def _sigmoid(x):
    return 1.0 / (1.0 + jnp.exp(-x))


def _log_sigmoid(x):
    return jnp.minimum(x, 0.0) - jnp.log(1.0 + jnp.exp(-jnp.abs(x)))


def _norm_body(*refs, combine, project):
    it = iter(refs)
    x_ref = next(it)
    if combine:
        ya_ref, yb_ref, gates_ref = next(it), next(it), next(it)
    g_ref = next(it)
    if project:
        ws_hi_ref, ws_lo_ref = next(it), next(it)
    if combine and project:
        xo_ref = next(it)
    h_ref = next(it)
    if project:
        small_ref = next(it)

    x = x_ref[...]
    if combine:
        gates = gates_ref[...]
        x = x + gates[:, 0:1] * ya_ref[...] + gates[:, 1:2] * yb_ref[...]
        if project:
            xo_ref[...] = x
    y = x * lax.rsqrt(jnp.mean(x * x, axis=-1, keepdims=True) + EPS) * g_ref[...]
    h_ref[...] = y.astype(h_ref.dtype)
    if project:
        small_ref[...] = _dot3(y, ws_hi_ref[...], ws_lo_ref[...])


def _norm_call(x, g_row, *, moe=None, small_w=None, out_dtype=BF16):
    T = x.shape[0]
    tm = ROW_TILE
    combine = moe is not None
    project = small_w is not None
    row_spec = pl.BlockSpec((tm, D_MODEL), lambda i: (i, 0))
    lane_spec = pl.BlockSpec((tm, LANES), lambda i: (i, 0))
    const = lambda shape: pl.BlockSpec(shape, lambda i: (0, 0))
    nblk = T // tm
    ins, in_specs = [x], [row_spec]
    if combine:
        y2, gates = moe
        ins += [y2, y2, gates]
        in_specs += [row_spec, pl.BlockSpec((tm, D_MODEL), lambda i: (i + nblk, 0)), lane_spec]
    ins.append(g_row)
    in_specs.append(const((1, D_MODEL)))
    if project:
        ws_hi, ws_lo, layer = small_w
        ins += [ws_hi, ws_lo]
        in_specs += [pl.BlockSpec((None, D_MODEL, LANES), lambda i: (layer, 0, 0))] * 2
    out_shape, out_specs = [], []
    if combine and project:
        out_shape.append(jax.ShapeDtypeStruct((T, D_MODEL), F32))
        out_specs.append(row_spec)
    out_shape.append(jax.ShapeDtypeStruct((T, D_MODEL), out_dtype))
    out_specs.append(row_spec)
    if project:
        out_shape.append(jax.ShapeDtypeStruct((T, LANES), F32))
        out_specs.append(lane_spec)
    return pl.pallas_call(
        functools.partial(_norm_body, combine=combine, project=project),
        grid=(nblk,),
        in_specs=in_specs,
        out_specs=out_specs,
        out_shape=out_shape,
        compiler_params=_cparams(("parallel",)),
        name="norm",
    )(*ins)


def _prep_body(win_ref, wout_ref, wp_ref, ws_hi_ref, ws_lo_ref, wo_ref):
    rows = win_ref.shape[0]
    wp_ref[:, 0:D_MAIN] = win_ref[:, 0:D_MAIN].astype(BF16)
    tail = win_ref[:, D_MAIN:D_IN]
    wp_ref[:, D_MAIN:D_PROJ] = tail[:, GLA_RANK:GLA_RANK + 3 * D_FOX].astype(BF16)
    small = jnp.concatenate([tail[:, 0:GLA_RANK], tail[:, GLA_RANK + 3 * D_FOX:],
                             jnp.zeros((rows, LANES - GLA_RANK - FOX_HEADS), F32)], axis=1)
    hi, lo = _split_bf16(small)
    ws_hi_ref[...] = hi
    ws_lo_ref[...] = lo
    wo_ref[...] = wout_ref[...].astype(BF16)


def _prep_call(w_in, w_out):
    depth = w_in.shape[0]
    tr = PREP_ROWS
    idx = lambda l, i: (l, i, 0)
    return pl.pallas_call(
        _prep_body,
        grid=(depth, D_MODEL // tr),
        in_specs=[pl.BlockSpec((None, tr, D_IN), idx),
                  pl.BlockSpec((None, tr, D_MODEL), idx)],
        out_specs=[pl.BlockSpec((None, tr, D_PROJ), idx),
                   pl.BlockSpec((None, tr, LANES), idx),
                   pl.BlockSpec((None, tr, LANES), idx),
                   pl.BlockSpec((None, tr, D_MODEL), idx)],
        out_shape=[jax.ShapeDtypeStruct((depth, D_MODEL, D_PROJ), BF16),
                   jax.ShapeDtypeStruct((depth, D_MODEL, LANES), BF16),
                   jax.ShapeDtypeStruct((depth, D_MODEL, LANES), BF16),
                   jax.ShapeDtypeStruct((depth, D_MODEL, D_MODEL), BF16)],
        compiler_params=_cparams(("parallel", "parallel")),
        name="weight_prep",
    )(w_in, w_out)


def _matmul_body(h_ref, w_ref, o_ref):
    o_ref[...] = _dot(h_ref[...], w_ref[...]).astype(o_ref.dtype)


def _inproj_call(h, w_all, layer):
    T = h.shape[0]
    tm = min(MM_TM, T)
    return pl.pallas_call(
        _matmul_body,
        grid=(T // tm, D_PROJ // MM_TN),
        in_specs=[pl.BlockSpec((tm, D_MODEL), lambda i, j: (i, 0)),
                  pl.BlockSpec((None, D_MODEL, MM_TN), lambda i, j: (layer, 0, j))],
        out_specs=pl.BlockSpec((tm, MM_TN), lambda i, j: (i, j)),
        out_shape=jax.ShapeDtypeStruct((T, D_PROJ), BF16),
        compiler_params=_cparams(("parallel", "parallel")),
        name="inproj",
    )(h, w_all)


CONV_PAD = 32


def _conv_body(a_ref, g_ref, w_ref, b_ref, lng_ref, lnb_ref, o_ref, u_ref, sh_ref):
    S = a_ref.shape[0]
    u_ref[0:CONV_PAD, :] = jnp.zeros((CONV_PAD, D_CONV), F32)
    u_ref[CONV_PAD:CONV_PAD + S, :] = a_ref[...].astype(F32) * _sigmoid(g_ref[...].astype(F32))
    bias = b_ref[...]
    lng = lng_ref[...]
    lnb = lnb_ref[...]
    first = CONV_PAD - (CONV_WIDTH - 1)

    def chunk(c, carry):
        r0 = pl.multiple_of(c * CONV_RC, CONV_RC)
        acc = jnp.broadcast_to(bias, (CONV_RC, D_CONV))
        win = u_ref[pl.ds(r0, CONV_RC + CONV_PAD), :]
        for s in range(1, SUBLANES):
            sh_ref[s - 1] = win[s:s + CONV_RC + CONV_PAD - SUBLANES, :]
        for j in range(CONV_WIDTH):
            s = (first + j) % SUBLANES
            a = first + j - s
            tap = win[a:a + CONV_RC, :] if s == 0 else sh_ref[s - 1, a:a + CONV_RC, :]
            acc = acc + w_ref[j:j + 1, :] * tap
        mu = jnp.mean(acc, axis=-1, keepdims=True)
        d = acc - mu
        var = jnp.mean(d * d, axis=-1, keepdims=True)
        yn = d * lax.rsqrt(var + EPS) * lng + lnb
        o_ref[pl.ds(r0, CONV_RC), :] = (yn * _sigmoid(yn)).astype(o_ref.dtype)
        return carry

    lax.fori_loop(0, S // CONV_RC, chunk, 0)


def _conv_call(proj, w_pad, b_row, lng_row, lnb_row, B, S):
    T = B * S
    const = lambda shape: pl.BlockSpec(shape, lambda b: (0, 0))
    return pl.pallas_call(
        _conv_body,
        grid=(B,),
        in_specs=[pl.BlockSpec((S, D_CONV), lambda b: (b, 0)),
                  pl.BlockSpec((S, D_CONV), lambda b: (b, 1)),
                  const((CONV_PAD, D_CONV)), const((1, D_CONV)), const((1, D_CONV)), const((1, D_CONV))],
        out_specs=pl.BlockSpec((S, D_CONV), lambda b: (b, 0)),
        out_shape=jax.ShapeDtypeStruct((T, D_CONV), BF16),
        scratch_shapes=[pltpu.VMEM((CONV_PAD + S, D_CONV), F32),
                        pltpu.VMEM((SUBLANES - 1, CONV_RC + CONV_PAD - SUBLANES, D_CONV), F32)],
        compiler_params=_cparams(("parallel",)),
        name="conv_mixer",
    )(proj, proj, w_pad, b_row, lng_row, lnb_row)


def _gla_body(q_ref, k_ref, v_ref, g_ref, low_ref, w2hi_ref, w2lo_ref, b2_ref, ng_ref, o_ref, st_ref):
    ts = q_ref.shape[0]
    nchunk = ts // GLA_CHUNK

    @pl.when(pl.program_id(1) == 0)
    def _():
        st_ref[...] = jnp.zeros(st_ref.shape, F32)

    la = _log_sigmoid(_dot3(low_ref[...], w2hi_ref[...], w2lo_ref[...]) + b2_ref[...]) * (1.0 / GLA_GATE_NORMALIZER)
    r = lax.broadcasted_iota(I32, (2 * ts, ts), 0)
    c = lax.broadcasted_iota(I32, (2 * ts, ts), 1)
    rr = jnp.where(r >= ts, r - ts, r)
    same_chunk = (rr // GLA_CHUNK) == (c // GLA_CHUNK)
    sel = jnp.where(same_chunk & ((r >= ts) | (c <= rr)), 1.0, 0.0).astype(BF16)
    la_hi, la_lo = _split_bf16(la)
    sums = _dot(sel, la_hi) + _dot(sel, la_lo)
    cum = sums[0:ts, :]
    last = sums[ts:2 * ts, :]
    e_q = jnp.exp(cum)
    e_inv = jnp.exp(-cum)
    e_end = jnp.exp(last - cum)
    e_last = jnp.exp(last)

    qr = lax.broadcasted_iota(I32, (ts, ts), 0)
    qc = lax.broadcasted_iota(I32, (ts, ts), 1)
    att_mask = ((qr // GLA_CHUNK) == (qc // GLA_CHUNK)) & (qc <= qr)
    ng = ng_ref[...]

    for h in range(GLA_HEADS):
        ks = slice(h * GLA_DK, (h + 1) * GLA_DK)
        vs = slice(h * GLA_DV, (h + 1) * GLA_DV)
        qh = q_ref[:, ks].astype(F32) * (GLA_DK ** -0.5)
        kh = k_ref[:, ks].astype(F32)
        vh = v_ref[:, vs]
        q_dec = (qh * e_q[:, ks]).astype(BF16)
        k_inv = (kh * e_inv[:, ks]).astype(BF16)
        k_end = (kh * e_end[:, ks]).astype(BF16)
        att = lax.dot_general(q_dec, k_inv, (((1,), (1,)), ((), ())), preferred_element_type=F32)
        att = jnp.where(att_mask, att, 0.0).astype(BF16)
        o_intra = _dot(att, vh)
        state = st_ref[h]
        outs = []
        for n in range(nchunk):
            rs = slice(n * GLA_CHUNK, (n + 1) * GLA_CHUNK)
            inter = lax.dot_general(q_dec[rs], state.astype(BF16), (((1,), (1,)), ((), ())),
                                    preferred_element_type=F32)
            outs.append(o_intra[rs] + inter)
            kv_t = lax.dot_general(vh[rs], k_end[rs], (((0,), (0,)), ((), ())), preferred_element_type=F32)
            state = state * e_last[n * GLA_CHUNK:n * GLA_CHUNK + 1, ks] + kv_t
        st_ref[h] = state
        o = jnp.concatenate(outs, axis=0)
        o = o * lax.rsqrt(jnp.mean(o * o, axis=-1, keepdims=True) + EPS) * ng
        gate = g_ref[:, vs].astype(F32)
        o_ref[:, vs] = (o * (gate * _sigmoid(gate))).astype(o_ref.dtype)


def _gla_call(proj, small, w2_hi, w2_lo, b2_row, ng_row, B, S):
    T = B * S
    ts = GLA_TS
    nst = S // ts
    row = lambda b, s: b * nst + s
    const = lambda shape: pl.BlockSpec(shape, lambda b, s: (0, 0))
    return pl.pallas_call(
        _gla_body,
        grid=(B, nst),
        in_specs=[pl.BlockSpec((ts, GLA_KEY), lambda b, s: (row(b, s), 2)),
                  pl.BlockSpec((ts, GLA_KEY), lambda b, s: (row(b, s), 3)),
                  pl.BlockSpec((ts, D_GLA), lambda b, s: (row(b, s), 2)),
                  pl.BlockSpec((ts, D_GLA), lambda b, s: (row(b, s), 3)),
                  pl.BlockSpec((ts, LANES), lambda b, s: (row(b, s), 0)),
                  const((LANES, GLA_KEY)), const((LANES, GLA_KEY)), const((1, GLA_KEY)), const((1, GLA_DV))],
        out_specs=pl.BlockSpec((ts, D_GLA), lambda b, s: (row(b, s), 0)),
        out_shape=jax.ShapeDtypeStruct((T, D_GLA), BF16),
        scratch_shapes=[pltpu.VMEM((GLA_HEADS, GLA_DV, GLA_DK), F32)],
        compiler_params=_cparams(("parallel", "arbitrary")),
        name="gla_mixer",
    )(proj, proj, proj, proj, small, w2_hi, w2_lo, b2_row, ng_row)


FGATE_BLK = 256


def _fgate_body(small_ref, fb_ref, fcol_ref, frow_ref):
    S = small_ref.shape[0]
    r = lax.broadcasted_iota(I32, (FGATE_BLK, FGATE_BLK), 0)
    c = lax.broadcasted_iota(I32, (FGATE_BLK, FGATE_BLK), 1)
    tri = jnp.where(c <= r, 1.0, 0.0).astype(BF16)
    carry = jnp.zeros((1, LANES), F32)
    for n in range(S // FGATE_BLK):
        rs = slice(n * FGATE_BLK, (n + 1) * FGATE_BLK)
        lf = _log_sigmoid(small_ref[rs, :] + fb_ref[...])
        p0 = lf.astype(BF16)
        r1 = lf - p0.astype(F32)
        p1 = r1.astype(BF16)
        p2 = (r1 - p1.astype(F32)).astype(BF16)
        blk = _dot(tri, p0) + _dot(tri, p1) + _dot(tri, p2) + carry
        fcol_ref[rs, :] = blk
        carry = blk[FGATE_BLK - 1:FGATE_BLK, :]
    ft = fcol_ref[...].T
    for h in range(FOX_HEADS):
        frow_ref[0, h] = ft[SMALL_FOX_LANE + h:SMALL_FOX_LANE + h + 1, :]


def _fgate_call(small, fb_row, B, S):
    T = B * S
    return pl.pallas_call(
        _fgate_body,
        grid=(B,),
        in_specs=[pl.BlockSpec((S, LANES), lambda b: (b, 0)),
                  pl.BlockSpec((1, LANES), lambda b: (0, 0))],
        out_specs=[pl.BlockSpec((S, LANES), lambda b: (b, 0)),
                   pl.BlockSpec((1, FOX_HEADS, 1, S), lambda b: (b, 0, 0, 0))],
        out_shape=[jax.ShapeDtypeStruct((T, LANES), F32),
                   jax.ShapeDtypeStruct((B, FOX_HEADS, 1, S), F32)],
        compiler_params=_cparams(("parallel",)),
        name="fox_gate",
    )(small, fb_row)


def _fox_body(q_ref, k_ref, v_ref, fcol_ref, frow_ref, o_ref, vt_ref, fb_ref, acc_ref):
    tq = q_ref.shape[0]
    tk = FOX_TK
    S = k_ref.shape[0]
    i = pl.program_id(1)

    @pl.when(i == 0)
    def _():
        for c in range(S // FOX_VT_BLK):
            cs = slice(c * FOX_VT_BLK, (c + 1) * FOX_VT_BLK)
            vt_ref[:, cs] = v_ref[cs, :].astype(F32).T.astype(BF16)
        for h in range(FOX_HEADS):
            fb_ref[h] = jnp.broadcast_to(fcol_ref[:, SMALL_FOX_LANE + h:SMALL_FOX_LANE + h + 1], (S, LANES))

    q0 = pl.multiple_of(i * tq, tq)
    key = lax.broadcasted_iota(I32, (tk, tq), 0)
    qry = lax.broadcasted_iota(I32, (tk, tq), 1)
    n_diag = tq // tk
    n_full = i * n_diag

    heads = []
    for h in range(FOX_HEADS):
        hs = slice(h * FOX_DH, (h + 1) * FOX_DH)
        qh = (q_ref[:, hs].astype(F32) * (FOX_DH ** -0.5)).astype(BF16)
        f_t = frow_ref[0, h, :, pl.ds(q0, tq)]
        heads.append((hs, qh, f_t))

    def update(j, states, diag=None):
        k0 = pl.multiple_of(j * tk, tk)
        zs = []
        for hs, qh, f_t in heads:
            kt = k_ref[pl.ds(k0, tk), hs]
            zs.append(lax.dot_general(kt, qh, (((1,), (1,)), ((), ())), preferred_element_type=F32))
        ps, alphas, new_states = [], [], []
        for h, (hs, qh, f_t) in enumerate(heads):
            m, l = states[h]
            f_s = fb_ref[h, pl.ds(k0, tk), :]
            z = zs[h] - jnp.concatenate([f_s] * (tq // LANES), axis=1)
            if diag is not None:
                z = jnp.where(key + diag * tk <= qry, z, -jnp.inf)
            m_new = jnp.maximum(m, jnp.max(z, axis=0, keepdims=True) + f_t)
            p = jnp.exp(z + (f_t - m_new))
            alpha = jnp.exp(m - m_new)
            new_states.append((m_new, alpha * l + jnp.sum(p, axis=0, keepdims=True)))
            ps.append(p.astype(BF16))
            alphas.append(alpha)
        for h, (hs, qh, f_t) in enumerate(heads):
            pv = _dot(vt_ref[hs, pl.ds(k0, tk)], ps[h])
            acc_ref[h] = alphas[h] * acc_ref[h] + pv
        return tuple(new_states)

    acc_ref[...] = jnp.zeros(acc_ref.shape, F32)
    states = ((jnp.full((1, tq), -jnp.inf, F32), jnp.zeros((1, tq), F32)),) * FOX_HEADS
    for d in range(n_diag):
        states = update(n_full + d, states, diag=d)

    states = lax.fori_loop(0, n_full, update, states)
    for h in range(FOX_HEADS):
        m, l = states[h]
        o_ref[:, heads[h][0]] = (acc_ref[h] / l).T.astype(o_ref.dtype)


def _fox_call(proj, fcol, frow, B, S):
    T = B * S
    tq = FOX_TQ
    nq = S // tq
    col0 = D_MAIN // D_FOX
    return pl.pallas_call(
        _fox_body,
        grid=(B, nq),
        in_specs=[pl.BlockSpec((tq, D_FOX), lambda b, i: (b * nq + i, col0)),
                  pl.BlockSpec((S, D_FOX), lambda b, i: (b, col0 + 1)),
                  pl.BlockSpec((S, D_FOX), lambda b, i: (b, col0 + 2)),
                  pl.BlockSpec((S, LANES), lambda b, i: (b, 0)),
                  pl.BlockSpec((1, FOX_HEADS, 1, S), lambda b, i: (b, 0, 0, 0))],
        out_specs=pl.BlockSpec((tq, D_FOX), lambda b, i: (b * nq + i, 0)),
        out_shape=jax.ShapeDtypeStruct((T, D_FOX), BF16),
        scratch_shapes=[pltpu.VMEM((D_FOX, S), BF16),
                        pltpu.VMEM((FOX_HEADS, S, LANES), F32),
                        pltpu.VMEM((FOX_HEADS, FOX_DH, tq), F32)],
        compiler_params=_cparams(("parallel", "arbitrary")),
        name="fox_mixer",
    )(proj, proj, proj, fcol, frow)


def _outproj_body(yc_ref, yg_ref, yf_ref, x_ref, w_ref, g_ref, wr_hi_ref, wr_lo_ref, br_ref,
                  xo_ref, h_ref, lg_ref):
    acc = x_ref[...]
    acc = acc + _dot(yc_ref[...], w_ref[0:D_CONV, :])
    acc = acc + _dot(yg_ref[...], w_ref[D_CONV:D_CONV + D_GLA, :])
    acc = acc + _dot(yf_ref[...], w_ref[D_CONV + D_GLA:D_MODEL, :])
    xo_ref[...] = acc
    hn = acc * lax.rsqrt(jnp.mean(acc * acc, axis=-1, keepdims=True) + EPS) * g_ref[...]
    h_ref[...] = hn
    lg_ref[...] = _dot3(hn, wr_hi_ref[...], wr_lo_ref[...]) + br_ref[...]


def _outproj_call(yc, yg, yf, x, w_all, layer, g_row, wr_hi, wr_lo, br_row):
    T = x.shape[0]
    tm = ROW_TILE
    const = lambda shape: pl.BlockSpec(shape, lambda i: (0, 0))
    rows = lambda width: pl.BlockSpec((tm, width), lambda i: (i, 0))
    return pl.pallas_call(
        _outproj_body,
        grid=(T // tm,),
        in_specs=[rows(D_CONV), rows(D_GLA), rows(D_FOX), rows(D_MODEL),
                  pl.BlockSpec((None, D_MODEL, D_MODEL), lambda i: (layer, 0, 0)), const((1, D_MODEL)),
                  const((D_MODEL, LANES)), const((D_MODEL, LANES)), const((1, LANES))],
        out_specs=[rows(D_MODEL), rows(D_MODEL), rows(LANES)],
        out_shape=[jax.ShapeDtypeStruct((T, D_MODEL), F32),
                   jax.ShapeDtypeStruct((T, D_MODEL), F32),
                   jax.ShapeDtypeStruct((T, LANES), F32)],
        compiler_params=_cparams(("parallel",)),
        name="outproj",
    )(yc, yg, yf, x, w_all, g_row, wr_hi, wr_lo, br_row)


def _router_body(lg_ref, ri_ref, rf_ref, cnt_ref, carry_ref):
    tr = lg_ref.shape[0]

    @pl.when(pl.program_id(0) == 0)
    def _():
        carry_ref[...] = jnp.zeros(carry_ref.shape, F32)

    lg = lg_ref[...]
    lane = lax.broadcasted_iota(I32, (tr, LANES), 1).astype(F32)
    big = float(LANES)
    neg = -jnp.inf

    is_g = lane < N_GROUPS
    gl = jnp.where(is_g, lg, neg)
    gmax = jnp.max(gl, axis=-1, keepdims=True)
    gexp = jnp.where(is_g, jnp.exp(lg - gmax), 0.0)
    gprob = gexp / jnp.sum(gexp, axis=-1, keepdims=True)
    gtop = jnp.max(gprob, axis=-1, keepdims=True)
    grp = jnp.min(jnp.where(is_g & (gprob == gtop), lane, big), axis=-1, keepdims=True)

    lo = ROUTER_EXPERT_LANE + grp * EXPERTS_PER_GROUP
    in_grp = (lane >= lo) & (lane < lo + EXPERTS_PER_GROUP)
    el = jnp.where(in_grp, lg, neg)
    v1 = jnp.max(el, axis=-1, keepdims=True)
    i1 = jnp.min(jnp.where(in_grp & (el == v1), lane, big), axis=-1, keepdims=True)
    rest = in_grp & (lane != i1)
    el2 = jnp.where(rest, lg, neg)
    v2 = jnp.max(el2, axis=-1, keepdims=True)
    i2 = jnp.min(jnp.where(rest & (el2 == v2), lane, big), axis=-1, keepdims=True)
    ex = jnp.exp(v2 - v1)
    p1 = 1.0 / (1.0 + ex)
    p2 = ex / (1.0 + ex)

    hit1 = lane == i1
    hit2 = lane == i2
    onehot = jnp.where(hit1 | hit2, 1.0, 0.0)
    r = lax.broadcasted_iota(I32, (tr, tr), 0)
    c = lax.broadcasted_iota(I32, (tr, tr), 1)
    strict = jnp.where(c < r, 1.0, 0.0).astype(BF16)
    before = _dot(strict, onehot.astype(BF16)) + carry_ref[...]
    rank1 = jnp.sum(jnp.where(hit1, before, 0.0), axis=-1, keepdims=True)
    rank2 = jnp.sum(jnp.where(hit2, before, 0.0), axis=-1, keepdims=True)
    carry_ref[...] = carry_ref[...] + jnp.sum(onehot, axis=0, keepdims=True)
    cnt_ref[...] = carry_ref[...]

    e1 = i1 - ROUTER_EXPERT_LANE
    e2 = i2 - ROUTER_EXPERT_LANE
    ri = jnp.where(lane == 0, e1, jnp.where(lane == 1, e2, jnp.where(lane == 2, rank1, jnp.where(lane == 3, rank2, 0.0))))
    ri_ref[...] = ri.astype(I32)
    rf_ref[...] = jnp.where(lane == 0, gtop * p1, jnp.where(lane == 1, gtop * p2, 0.0))


def _router_call(logits):
    T = logits.shape[0]
    tr = ROUTE_TR
    rows = pl.BlockSpec((tr, LANES), lambda i: (i, 0))
    return pl.pallas_call(
        _router_body,
        grid=(T // tr,),
        in_specs=[rows],
        out_specs=[rows, rows, pl.BlockSpec((1, LANES), lambda i: (0, 0))],
        out_shape=[jax.ShapeDtypeStruct((T, LANES), I32),
                   jax.ShapeDtypeStruct((T, LANES), F32),
                   jax.ShapeDtypeStruct((1, LANES), F32)],
        scratch_shapes=[pltpu.VMEM((1, LANES), F32)],
        compiler_params=_cparams(("arbitrary",)),
        name="router",
    )(logits)


def _moe_body(n_act_ref, tile_e_ref, tile_n_ref, first_ref, wslot_ref, next_e_ref, src_ref, dst_ref,
              h_hbm, wg_hbm, wu_hbm, wd_hbm, y_hbm,
              xbuf, ybuf, xs, wg_f, wu_f, wd_f, wg_b, wu_b, wd_b, gsem, ssem, wsem, *, expert0):
    k = pl.program_id(0)
    nv = tile_n_ref[k]
    n_act = n_act_ref[0]

    def slot_of(tile):
        return lax.rem(tile + MOE_SLOTS, MOE_SLOTS)

    def rows_moved(tile):
        nv_t = tile_n_ref[jnp.maximum(tile, 0)]
        nv_t = jnp.where(tile < 0, MOE_TM, nv_t)
        return pl.multiple_of(((nv_t + SUBLANES - 1) // SUBLANES) * SUBLANES, SUBLANES)

    def weight_copies(e, ws):
        return (pltpu.make_async_copy(wg_hbm.at[expert0 + e], wg_f.at[ws], wsem.at[ws]),
                pltpu.make_async_copy(wu_hbm.at[expert0 + e], wu_f.at[ws], wsem.at[ws]),
                pltpu.make_async_copy(wd_hbm.at[expert0 + e], wd_f.at[ws], wsem.at[ws]))

    def gather_row(tile, r, buf):
        return pltpu.make_async_copy(h_hbm.at[pl.ds(src_ref[tile * MOE_TM + r], 1)],
                                     xbuf.at[buf, pl.ds(r, 1)], gsem.at[buf])

    def scatter_row(tile, r, buf):
        return pltpu.make_async_copy(ybuf.at[buf, pl.ds(r, 1)],
                                     y_hbm.at[pl.ds(dst_ref[(tile + 1) * MOE_TM + r], 1)], ssem.at[buf])

    def wait_gather(tile, buf):
        n = rows_moved(tile)

        @pl.when(n > 0)
        def _():
            pltpu.make_async_copy(h_hbm.at[pl.ds(0, n)], xbuf.at[buf, pl.ds(0, n)], gsem.at[buf]).wait()

    def wait_scatter(tile, buf):
        n = rows_moved(tile)

        @pl.when(n > 0)
        def _():
            pltpu.make_async_copy(ybuf.at[buf, pl.ds(0, n)], y_hbm.at[pl.ds(0, n)], ssem.at[buf]).wait()

    def issue_loop(make_copy, tile, buf):
        def issue(r, carry):
            make_copy(tile, r, buf).start()
            return carry

        lax.fori_loop(0, rows_moved(tile), issue, 0)

    @pl.when(k == 0)
    def _():
        xbuf[...] = jnp.zeros(xbuf.shape, F32)
        ybuf[MOE_SLOTS - 1] = jnp.zeros((MOE_TM, D_MODEL), F32)
        for c in weight_copies(tile_e_ref[0], 0):
            c.start()
        issue_loop(gather_row, 0, 0)
        issue_loop(gather_row, 1, 1)

    @pl.when(nv > 0)
    def _():
        @pl.when(first_ref[k] == 1)
        def _():
            ws = wslot_ref[k]
            for c in weight_copies(tile_e_ref[k], ws):
                c.wait()
            nxt = next_e_ref[k]

            @pl.when(nxt >= 0)
            def _():
                for c in weight_copies(nxt, 1 - ws):
                    c.start(priority=1)

            wg_b[...] = wg_f[ws].astype(BF16)
            wu_b[...] = wu_f[ws].astype(BF16)
            wd_b[...] = wd_f[ws].astype(BF16)

        def tile_step(cur):
            prv = (cur + MOE_SLOTS - 1) % MOE_SLOTS
            wait_gather(k, cur)
            xs[...] = xbuf[cur].astype(BF16)
            n_in = rows_moved(k + 2)
            n_out = rows_moved(k - 1)
            for g in range(0, MOE_TM, SUBLANES):
                @pl.when(g < n_in)
                def _(g=g):
                    for r in range(g, g + SUBLANES):
                        gather_row(k + 2, r, prv).start(priority=0)
            for g in range(0, MOE_TM, SUBLANES):
                @pl.when(g < n_out)
                def _(g=g):
                    for r in range(g, g + SUBLANES):
                        scatter_row(k - 1, r, prv).start(priority=1)
            xb = xs[...]
            gate = _dot(xb, wg_b[...])
            up = _dot(xb, wu_b[...])
            mid = (gate * _sigmoid(gate) * up).astype(BF16)
            ybuf[cur] = _dot(mid, wd_b[...])

        for s in range(MOE_SLOTS):
            pl.when(slot_of(k) == s)(functools.partial(tile_step, s))

        @pl.when(k > 0)
        def _():
            wait_scatter(k - 2, slot_of(k - 2))

    @pl.when(k == n_act)
    def _():
        issue_loop(scatter_row, k - 1, slot_of(k - 1))
        wait_scatter(k - 2, slot_of(k - 2))

    @pl.when(k == n_act + 1)
    def _():
        wait_scatter(k - 2, slot_of(k - 2))


def _moe_call(tables, h, wg, wu, wd, layer, n_tiles):
    T = h.shape[0]
    any_spec = pl.BlockSpec(memory_space=pl.ANY)
    grid_spec = pltpu.PrefetchScalarGridSpec(
        num_scalar_prefetch=len(tables),
        grid=(n_tiles + MOE_DRAIN_STEPS,),
        in_specs=[any_spec, any_spec, any_spec, any_spec],
        out_specs=any_spec,
        scratch_shapes=[pltpu.VMEM((MOE_SLOTS, MOE_TM, D_MODEL), F32),
                        pltpu.VMEM((MOE_SLOTS, MOE_TM, D_MODEL), F32),
                        pltpu.VMEM((MOE_TM, D_MODEL), BF16),
                        pltpu.VMEM((2, D_MODEL, D_EXPERT), F32),
                        pltpu.VMEM((2, D_MODEL, D_EXPERT), F32),
                        pltpu.VMEM((2, D_EXPERT, D_MODEL), F32),
                        pltpu.VMEM((D_MODEL, D_EXPERT), BF16),
                        pltpu.VMEM((D_MODEL, D_EXPERT), BF16),
                        pltpu.VMEM((D_EXPERT, D_MODEL), BF16),
                        pltpu.SemaphoreType.DMA((MOE_SLOTS,)),
                        pltpu.SemaphoreType.DMA((MOE_SLOTS,)),
                        pltpu.SemaphoreType.DMA((2,))],
    )
    return pl.pallas_call(
        functools.partial(_moe_body, expert0=layer * N_EXPERTS),
        grid_spec=grid_spec,
        out_shape=jax.ShapeDtypeStruct((2 * T + MOE_TM, D_MODEL), F32),
        compiler_params=_cparams(("arbitrary",)),
        name="moe_experts",
    )(*tables, h, wg, wu, wd)


def _tables_body(e1_ref, e2_ref, r1_ref, r2_ref, cnt_ref,
                 n_act_ref, tile_e_ref, tile_n_ref, first_ref, wslot_ref, next_e_ref, src_ref, dst_ref,
                 row0_ref, after_ref):
    T = e1_ref.shape[0]
    n_steps = tile_e_ref.shape[0]
    n_rows = src_ref.shape[0]

    nxt = jnp.int32(-1)
    for e in reversed(range(N_EXPERTS)):
        after_ref[e] = nxt
        nxt = jnp.where(cnt_ref[e] > 0, jnp.int32(e), nxt)

    def padding(t, carry):
        base = t * MOE_TM
        for r in range(MOE_TM):
            src_ref[base + r] = 0
            dst_ref[base + r] = 2 * T + r
        return carry

    lax.fori_loop(0, n_rows // MOE_TM, padding, 0)

    k = jnp.int32(0)
    order = jnp.int32(0)
    for e in range(N_EXPERTS):
        n = cnt_ref[e]
        nt = (n + (MOE_TM - 1)) // MOE_TM
        row0_ref[e] = k * MOE_TM

        def tile(i, carry, e=e, n=n, k=k, order=order):
            tile_e_ref[k + i] = e
            tile_n_ref[k + i] = jnp.minimum(n - i * MOE_TM, MOE_TM)
            first_ref[k + i] = (i == 0).astype(I32)
            wslot_ref[k + i] = order & 1
            next_e_ref[k + i] = after_ref[e]
            return carry

        lax.fori_loop(0, nt, tile, 0)
        k = k + nt
        order = order + (nt > 0).astype(I32)
    n_act_ref[0] = k

    def idle(i, carry):
        tile_e_ref[i] = 0
        tile_n_ref[i] = 0
        first_ref[i] = 0
        wslot_ref[i] = 0
        next_e_ref[i] = -1
        return carry

    lax.fori_loop(k, n_steps, idle, 0)

    def assign(t, carry):
        p1 = row0_ref[e1_ref[t]] + r1_ref[t]
        p2 = row0_ref[e2_ref[t]] + r2_ref[t]
        src_ref[p1] = t
        src_ref[p2] = t
        dst_ref[p1 + MOE_TM] = t
        dst_ref[p2 + MOE_TM] = T + t
        return carry

    lax.fori_loop(0, T, assign, 0, unroll=8)


def _route_tables(route_i, counts, T, n_tiles):
    n_steps = n_tiles + MOE_DRAIN_STEPS
    n_rows = n_steps * MOE_TM
    cnt = counts[0, ROUTER_EXPERT_LANE:ROUTER_EXPERT_LANE + N_EXPERTS].astype(I32)
    smem = pl.BlockSpec(memory_space=pltpu.SMEM)
    vec = lambda n: jax.ShapeDtypeStruct((n,), I32)
    return pl.pallas_call(
        _tables_body,
        in_specs=[smem] * 5,
        out_specs=[smem] * 8,
        out_shape=[vec(1)] + [vec(n_steps)] * 5 + [vec(n_rows)] * 2,
        scratch_shapes=[pltpu.SMEM((N_EXPERTS,), I32), pltpu.SMEM((N_EXPERTS,), I32)],
        name="route_tables",
    )(route_i[:, 0], route_i[:, 1], route_i[:, 2], route_i[:, 3], cnt)


def _pad_lanes(w, offset=0):
    return jnp.pad(w, ((0, 0), (offset, LANES - offset - w.shape[1])))


def kernel(x, norm1_g, w_in, conv_w, conv_b, conv_ln_g, conv_ln_b, gla_w2, gla_b2, gla_norm_g, fox_f_b, w_out, norm2_g, router_group_w, router_group_b, router_expert_w, router_expert_b, ffn_w_gate, ffn_w_up, ffn_w_down, final_norm_g):
    B, S, D = x.shape
    T = B * S
    depth = w_in.shape[0]
    n_tiles = (2 * T) // MOE_TM + N_EXPERTS

    wg_all = ffn_w_gate.reshape(depth * N_EXPERTS, D_MODEL, D_EXPERT)
    wu_all = ffn_w_up.reshape(depth * N_EXPERTS, D_MODEL, D_EXPERT)
    wd_all = ffn_w_down.reshape(depth * N_EXPERTS, D_EXPERT, D_MODEL)
    w_proj, ws_hi, ws_lo, w_o = _prep_call(w_in, w_out)

    xt = x.reshape(T, D)
    h, small = _norm_call(xt, norm1_g[0][None, :], small_w=(ws_hi, ws_lo, 0))
    for l in range(depth):
        proj = _inproj_call(h, w_proj, l)

        y_conv = _conv_call(proj, jnp.pad(conv_w[l], ((0, CONV_PAD - CONV_WIDTH), (0, 0))), conv_b[l][None, :],
                            conv_ln_g[l][None, :], conv_ln_b[l][None, :], B, S)
        w2_hi, w2_lo = _split_bf16(jnp.pad(gla_w2[l], ((0, LANES - GLA_RANK), (0, 0))))
        y_gla = _gla_call(proj, small, w2_hi, w2_lo, gla_b2[l][None, :], gla_norm_g[l][None, :], B, S)
        fcol, frow = _fgate_call(small, _pad_lanes(fox_f_b[l][None, :], SMALL_FOX_LANE), B, S)
        y_fox = _fox_call(proj, fcol, frow, B, S)

        w_route = jnp.concatenate([router_group_w[l],
                                   router_expert_w[l].transpose(1, 0, 2).reshape(D_MODEL, N_EXPERTS)], axis=1)
        wr_hi, wr_lo = _split_bf16(_pad_lanes(w_route))
        b_route = _pad_lanes(jnp.concatenate([router_group_b[l], router_expert_b[l].reshape(-1)])[None, :])
        xt, h2, logits = _outproj_call(y_conv, y_gla, y_fox, xt, w_o, l, norm2_g[l][None, :],
                                       wr_hi, wr_lo, b_route)

        route_i, gates, counts = _router_call(logits)
        tables = _route_tables(route_i, counts, T, n_tiles)
        y2 = _moe_call(tables, h2, wg_all, wu_all, wd_all, l, n_tiles)

        if l + 1 < depth:
            xt, h, small = _norm_call(xt, norm1_g[l + 1][None, :], moe=(y2, gates), small_w=(ws_hi, ws_lo, l + 1))
        else:
            (out,) = _norm_call(xt, final_norm_g[None, :], moe=(y2, gates), out_dtype=F32)
    return out.reshape(B, S, D)
```

```python
import functools

import jax
import jax.numpy as jnp
from jax import lax
from jax.experimental import pallas as pl
from jax.experimental.pallas import tpu as pltpu

F32 = jnp.float32
BF16 = jnp.bfloat16
I32 = jnp.int32

D_MODEL = 2048
EPS = 1e-6
D_CONV = 512
CONV_WIDTH = 31
D_GLA = 1024
GLA_HEADS = 4
GLA_DK = 128
GLA_DV = 256
GLA_KEY = GLA_HEADS * GLA_DK
GLA_RANK = 16
GLA_GATE_NORMALIZER = 16.0
GLA_CHUNK = 64
D_FOX = 512
FOX_HEADS = 4
FOX_DH = 128
N_GROUPS = 4
EXPERTS_PER_GROUP = 8
N_EXPERTS = N_GROUPS * EXPERTS_PER_GROUP
D_EXPERT = 512

LANES = 128
SUBLANES = 8
D_MAIN = 2 * D_CONV + 2 * GLA_KEY + 2 * D_GLA
D_PROJ = D_MAIN + 3 * D_FOX
D_IN = D_MAIN + GLA_RANK + 3 * D_FOX + FOX_HEADS
PREP_N = 512
PREP_LAST = 24
SMALL_FOX_LANE = GLA_RANK
ROUTER_EXPERT_LANE = N_GROUPS

VMEM_LIMIT = 56 * 1024 * 1024

ROW_TILE = 256
MM_TM = 1024
MM_TN = 1408
GLA_TS = 256
FOX_TQ = 256
FOX_TK = 256
FOX_VT_BLK = 512
CONV_RC = 64
ROUTE_TR = 512
MOE_TM = 256
MOE_SLOTS = 3
MOE_DRAIN_STEPS = 2


def _cparams(sem):
    return pltpu.CompilerParams(dimension_semantics=sem, vmem_limit_bytes=VMEM_LIMIT)


def _split_bf16(x):
    hi = x.astype(BF16)
    lo = (x - hi.astype(F32)).astype(BF16)
    return hi, lo


def _dot(a, b):
    return jnp.dot(a, b, preferred_element_type=F32)


def _dot3(a, b_hi, b_lo):
    a_hi, a_lo = _split_bf16(a)
    return _dot(a_hi, b_hi) + _dot(a_lo, b_hi) + _dot(a_hi, b_lo)


def _sigmoid(x):
    return 1.0 / (1.0 + jnp.exp(-x))


def _log_sigmoid(x):
    return jnp.minimum(x, 0.0) - jnp.log(1.0 + jnp.exp(-jnp.abs(x)))


def _norm_body(*refs, combine, project):
    it = iter(refs)
    x_ref = next(it)
    if combine:
        ya_ref, yb_ref, gates_ref = next(it), next(it), next(it)
    g_ref = next(it)
    if project:
        ws_hi_ref, ws_lo_ref = next(it), next(it)
    if combine and project:
        xo_ref = next(it)
    h_ref = next(it)
    if project:
        small_ref = next(it)

    x = x_ref[...]
    if combine:
        gates = gates_ref[...]
        x = x + gates[:, 0:1] * ya_ref[...] + gates[:, 1:2] * yb_ref[...]
        if project:
            xo_ref[...] = x
    y = x * lax.rsqrt(jnp.mean(x * x, axis=-1, keepdims=True) + EPS) * g_ref[...]
    h_ref[...] = y.astype(h_ref.dtype)
    if project:
        y_hi, y_lo = _split_bf16(y)
        ws_hi = ws_hi_ref[...]
        small_ref[...] = _dot_nt(y_hi, ws_hi) + _dot_nt(y_lo, ws_hi) + _dot_nt(y_hi, ws_lo_ref[...])


def _norm_call(x, g_row, *, moe=None, small_w=None, out_dtype=BF16):
    T = x.shape[0]
    tm = ROW_TILE
    combine = moe is not None
    project = small_w is not None
    row_spec = pl.BlockSpec((tm, D_MODEL), lambda i: (i, 0))
    lane_spec = pl.BlockSpec((tm, LANES), lambda i: (i, 0))
    const = lambda shape: pl.BlockSpec(shape, lambda i: (0, 0))
    nblk = T // tm
    ins, in_specs = [x], [row_spec]
    if combine:
        y2, gates = moe
        ins += [y2, y2, gates]
        in_specs += [row_spec, pl.BlockSpec((tm, D_MODEL), lambda i: (i + nblk, 0)), lane_spec]
    ins.append(g_row)
    in_specs.append(const((1, D_MODEL)))
    if project:
        ws_hi, ws_lo, layer = small_w
        ins += [ws_hi, ws_lo]
        in_specs += [pl.BlockSpec((None, LANES, D_MODEL), lambda i: (layer, 0, 0))] * 2
    out_shape, out_specs = [], []
    if combine and project:
        out_shape.append(jax.ShapeDtypeStruct((T, D_MODEL), F32))
        out_specs.append(row_spec)
    out_shape.append(jax.ShapeDtypeStruct((T, D_MODEL), out_dtype))
    out_specs.append(row_spec)
    if project:
        out_shape.append(jax.ShapeDtypeStruct((T, LANES), F32))
        out_specs.append(lane_spec)
    return pl.pallas_call(
        functools.partial(_norm_body, combine=combine, project=project),
        grid=(nblk,),
        in_specs=in_specs,
        out_specs=out_specs,
        out_shape=out_shape,
        compiler_params=_cparams(("parallel",)),
        name="norm",
    )(*ins)


def _prep_in_body(w_hbm, wp_ref, ws_hi_ref, ws_lo_ref, buf, tail_ref, sem):
    j = pl.program_id(0)
    depth = buf.shape[1]
    n_main = D_MAIN // PREP_N
    n_last = D_IN // PREP_N
    keep = PREP_N - GLA_RANK
    slot = j % 2

    def full_block(jj, s, l):
        return pltpu.make_async_copy(w_hbm.at[pl.ds(jj * PREP_N, PREP_N), l, :], buf.at[s, l], sem.at[s])

    def last_block(s, l):
        return pltpu.make_async_copy(w_hbm.at[pl.ds(D_IN - PREP_LAST, PREP_LAST), l, :],
                                     buf.at[s, l, pl.ds(0, PREP_LAST)], sem.at[s])

    def for_block(jj, s, action):
        @pl.when(jj < n_last)
        def _():
            for l in range(depth):
                action(full_block(jj, s, l))

        @pl.when(jj == n_last)
        def _():
            for l in range(depth):
                action(last_block(s, l))

    @pl.when(j == 0)
    def _():
        for_block(j, slot, lambda c: c.start())
        ws_hi_ref[...] = jnp.zeros(ws_hi_ref.shape, BF16)
        ws_lo_ref[...] = jnp.zeros(ws_lo_ref.shape, BF16)

    @pl.when(j < n_last)
    def _():
        for_block(j + 1, 1 - slot, lambda c: c.start())

    for_block(j, slot, lambda c: c.wait())

    head0 = PREP_N * n_last - (D_IN - PREP_LAST)
    for l in range(depth):
        @pl.when(j < n_main)
        def _(l=l):
            wp_ref[l] = buf[slot, l].astype(BF16)

        @pl.when(j == n_main)
        def _(l=l):
            hi, lo = _split_bf16(buf[slot, l, 0:GLA_RANK, :])
            ws_hi_ref[l, 0:GLA_RANK, :] = hi
            ws_lo_ref[l, 0:GLA_RANK, :] = lo

        @pl.when(j > n_main)
        def _(l=l):
            wp_ref[l, 0:keep, :] = tail_ref[l, 0:keep, :]

        @pl.when((j > n_main) & (j < n_last))
        def _(l=l):
            wp_ref[l, keep:PREP_N, :] = buf[slot, l, 0:GLA_RANK, :].astype(BF16)

        @pl.when((j >= n_main) & (j < n_last))
        def _(l=l):
            tail_ref[l, 0:keep, :] = buf[slot, l, GLA_RANK:PREP_N, :].astype(BF16)

        @pl.when(j == n_last)
        def _(l=l):
            w = buf[slot, l, 0:2 * PREP_LAST, :]
            wp_ref[l, keep:PREP_N, :] = w[head0:head0 + GLA_RANK].astype(BF16)
            row = lax.broadcasted_iota(I32, (GLA_RANK, D_MODEL), 0)
            logits = jnp.where(row < FOX_HEADS, w[head0 + GLA_RANK:head0 + 2 * GLA_RANK], 0.0)
            hi, lo = _split_bf16(logits)
            ws_hi_ref[l, GLA_RANK:2 * GLA_RANK, :] = hi
            ws_lo_ref[l, GLA_RANK:2 * GLA_RANK, :] = lo


def _prep_in_call(w_in):
    depth = w_in.shape[0]
    n_main = D_MAIN // PREP_N
    w_t = jnp.transpose(w_in, (2, 0, 1))
    out_block = lambda j: (0, jnp.where(j <= n_main, jnp.minimum(j, n_main - 1), j - 1), 0)
    const = lambda j: (0, 0, 0)
    return pl.pallas_call(
        _prep_in_body,
        grid=(D_IN // PREP_N + 1,),
        in_specs=[pl.BlockSpec(memory_space=pl.ANY)],
        out_specs=[pl.BlockSpec((depth, PREP_N, D_MODEL), out_block),
                   pl.BlockSpec((depth, LANES, D_MODEL), const),
                   pl.BlockSpec((depth, LANES, D_MODEL), const)],
        out_shape=[jax.ShapeDtypeStruct((depth, D_PROJ, D_MODEL), BF16),
                   jax.ShapeDtypeStruct((depth, LANES, D_MODEL), BF16),
                   jax.ShapeDtypeStruct((depth, LANES, D_MODEL), BF16)],
        scratch_shapes=[pltpu.VMEM((2, depth, PREP_N, D_MODEL), F32),
                        pltpu.VMEM((depth, PREP_N, D_MODEL), BF16),
                        pltpu.SemaphoreType.DMA((2,))],
        compiler_params=_cparams(("arbitrary",)),
        name="weight_prep_in",
    )(w_t)


def _prep_out_body(w_ref, o_ref):
    o_ref[...] = w_ref[...].astype(BF16)


def _prep_out_call(w_out):
    depth = w_out.shape[0]
    spec = pl.BlockSpec((None, PREP_N, D_MODEL), lambda l, i: (l, i, 0))
    return pl.pallas_call(
        _prep_out_body,
        grid=(depth, D_MODEL // PREP_N),
        in_specs=[spec],
        out_specs=spec,
        out_shape=jax.ShapeDtypeStruct((depth, D_MODEL, D_MODEL), BF16),
        compiler_params=_cparams(("parallel", "parallel")),
        name="weight_prep_out",
    )(w_out)


def _dot_nt(a, b_t):
    return lax.dot_general(a, b_t, (((1,), (1,)), ((), ())), preferred_element_type=F32)


def _matmul_body(h_ref, w_ref, o_ref):
    o_ref[...] = _dot_nt(h_ref[...], w_ref[...]).astype(o_ref.dtype)


def _inproj_call(h, w_all, layer):
    T = h.shape[0]
    tm = min(MM_TM, T)
    return pl.pallas_call(
        _matmul_body,
        grid=(T // tm, D_PROJ // MM_TN),
        in_specs=[pl.BlockSpec((tm, D_MODEL), lambda i, j: (i, 0)),
                  pl.BlockSpec((None, MM_TN, D_MODEL), lambda i, j: (layer, j, 0))],
        out_specs=pl.BlockSpec((tm, MM_TN), lambda i, j: (i, j)),
        out_shape=jax.ShapeDtypeStruct((T, D_PROJ), BF16),
        compiler_params=_cparams(("parallel", "parallel")),
        name="inproj",
    )(h, w_all)


CONV_PAD = 32


def _conv_body(a_ref, g_ref, w_ref, b_ref, lng_ref, lnb_ref, o_ref, u_ref, sh_ref):
    S = a_ref.shape[0]
    u_ref[0:CONV_PAD, :] = jnp.zeros((CONV_PAD, D_CONV), F32)
    u_ref[CONV_PAD:CONV_PAD + S, :] = a_ref[...].astype(F32) * _sigmoid(g_ref[...].astype(F32))
    bias = b_ref[...]
    lng = lng_ref[...]
    lnb = lnb_ref[...]
    first = CONV_PAD - (CONV_WIDTH - 1)

    def chunk(c, carry):
        r0 = pl.multiple_of(c * CONV_RC, CONV_RC)
        acc = jnp.broadcast_to(bias, (CONV_RC, D_CONV))
        win = u_ref[pl.ds(r0, CONV_RC + CONV_PAD), :]
        for s in range(1, SUBLANES):
            sh_ref[s - 1] = win[s:s + CONV_RC + CONV_PAD - SUBLANES, :]
        for j in range(CONV_WIDTH):
            s = (first + j) % SUBLANES
            a = first + j - s
            tap = win[a:a + CONV_RC, :] if s == 0 else sh_ref[s - 1, a:a + CONV_RC, :]
            acc = acc + w_ref[j:j + 1, :] * tap
        mu = jnp.mean(acc, axis=-1, keepdims=True)
        d = acc - mu
        var = jnp.mean(d * d, axis=-1, keepdims=True)
        yn = d * lax.rsqrt(var + EPS) * lng + lnb
        o_ref[pl.ds(r0, CONV_RC), :] = (yn * _sigmoid(yn)).astype(o_ref.dtype)
        return carry

    lax.fori_loop(0, S // CONV_RC, chunk, 0)


def _conv_call(proj, w_pad, b_row, lng_row, lnb_row, B, S):
    T = B * S
    const = lambda shape: pl.BlockSpec(shape, lambda b: (0, 0))
    return pl.pallas_call(
        _conv_body,
        grid=(B,),
        in_specs=[pl.BlockSpec((S, D_CONV), lambda b: (b, 0)),
                  pl.BlockSpec((S, D_CONV), lambda b: (b, 1)),
                  const((CONV_PAD, D_CONV)), const((1, D_CONV)), const((1, D_CONV)), const((1, D_CONV))],
        out_specs=pl.BlockSpec((S, D_CONV), lambda b: (b, 0)),
        out_shape=jax.ShapeDtypeStruct((T, D_CONV), BF16),
        scratch_shapes=[pltpu.VMEM((CONV_PAD + S, D_CONV), F32),
                        pltpu.VMEM((SUBLANES - 1, CONV_RC + CONV_PAD - SUBLANES, D_CONV), F32)],
        compiler_params=_cparams(("parallel",)),
        name="conv_mixer",
    )(proj, proj, w_pad, b_row, lng_row, lnb_row)


def _gla_body(q_ref, k_ref, v_ref, g_ref, low_ref, w2hi_ref, w2lo_ref, b2_ref, ng_ref, o_ref, st_ref):
    ts = q_ref.shape[0]
    nchunk = ts // GLA_CHUNK

    @pl.when(pl.program_id(1) == 0)
    def _():
        st_ref[...] = jnp.zeros(st_ref.shape, F32)

    la = _log_sigmoid(_dot3(low_ref[...], w2hi_ref[...], w2lo_ref[...]) + b2_ref[...]) * (1.0 / GLA_GATE_NORMALIZER)
    r = lax.broadcasted_iota(I32, (2 * ts, ts), 0)
    c = lax.broadcasted_iota(I32, (2 * ts, ts), 1)
    rr = jnp.where(r >= ts, r - ts, r)
    same_chunk = (rr // GLA_CHUNK) == (c // GLA_CHUNK)
    sel = jnp.where(same_chunk & ((r >= ts) | (c <= rr)), 1.0, 0.0).astype(BF16)
    la_hi, la_lo = _split_bf16(la)
    sums = _dot(sel, la_hi) + _dot(sel, la_lo)
    cum = sums[0:ts, :]
    last = sums[ts:2 * ts, :]
    e_q = jnp.exp(cum)
    e_inv = jnp.exp(-cum)
    e_end = jnp.exp(last - cum)
    e_last = jnp.exp(last)

    qr = lax.broadcasted_iota(I32, (ts, ts), 0)
    qc = lax.broadcasted_iota(I32, (ts, ts), 1)
    att_mask = ((qr // GLA_CHUNK) == (qc // GLA_CHUNK)) & (qc <= qr)
    ng = ng_ref[...]

    for h in range(GLA_HEADS):
        ks = slice(h * GLA_DK, (h + 1) * GLA_DK)
        vs = slice(h * GLA_DV, (h + 1) * GLA_DV)
        qh = q_ref[:, ks].astype(F32) * (GLA_DK ** -0.5)
        kh = k_ref[:, ks].astype(F32)
        vh = v_ref[:, vs]
        q_dec = (qh * e_q[:, ks]).astype(BF16)
        k_inv = (kh * e_inv[:, ks]).astype(BF16)
        k_end = (kh * e_end[:, ks]).astype(BF16)
        att = lax.dot_general(q_dec, k_inv, (((1,), (1,)), ((), ())), preferred_element_type=F32)
        att = jnp.where(att_mask, att, 0.0).astype(BF16)
        o_intra = _dot(att, vh)
        state = st_ref[h]
        outs = []
        for n in range(nchunk):
            rs = slice(n * GLA_CHUNK, (n + 1) * GLA_CHUNK)
            inter = lax.dot_general(q_dec[rs], state.astype(BF16), (((1,), (1,)), ((), ())),
                                    preferred_element_type=F32)
            outs.append(o_intra[rs] + inter)
            kv_t = lax.dot_general(vh[rs], k_end[rs], (((0,), (0,)), ((), ())), preferred_element_type=F32)
            state = state * e_last[n * GLA_CHUNK:n * GLA_CHUNK + 1, ks] + kv_t
        st_ref[h] = state
        o = jnp.concatenate(outs, axis=0)
        o = o * lax.rsqrt(jnp.mean(o * o, axis=-1, keepdims=True) + EPS) * ng
        gate = g_ref[:, vs].astype(F32)
        o_ref[:, vs] = (o * (gate * _sigmoid(gate))).astype(o_ref.dtype)


def _gla_call(proj, small, w2_hi, w2_lo, b2_row, ng_row, B, S):
    T = B * S
    ts = GLA_TS
    nst = S // ts
    row = lambda b, s: b * nst + s
    const = lambda shape: pl.BlockSpec(shape, lambda b, s: (0, 0))
    return pl.pallas_call(
        _gla_body,
        grid=(B, nst),
        in_specs=[pl.BlockSpec((ts, GLA_KEY), lambda b, s: (row(b, s), 2)),
                  pl.BlockSpec((ts, GLA_KEY), lambda b, s: (row(b, s), 3)),
                  pl.BlockSpec((ts, D_GLA), lambda b, s: (row(b, s), 2)),
                  pl.BlockSpec((ts, D_GLA), lambda b, s: (row(b, s), 3)),
                  pl.BlockSpec((ts, LANES), lambda b, s: (row(b, s), 0)),
                  const((LANES, GLA_KEY)), const((LANES, GLA_KEY)), const((1, GLA_KEY)), const((1, GLA_DV))],
        out_specs=pl.BlockSpec((ts, D_GLA), lambda b, s: (row(b, s), 0)),
        out_shape=jax.ShapeDtypeStruct((T, D_GLA), BF16),
        scratch_shapes=[pltpu.VMEM((GLA_HEADS, GLA_DV, GLA_DK), F32)],
        compiler_params=_cparams(("parallel", "arbitrary")),
        name="gla_mixer",
    )(proj, proj, proj, proj, small, w2_hi, w2_lo, b2_row, ng_row)


FGATE_BLK = 256


def _fgate_body(small_ref, fb_ref, fcol_ref, frow_ref):
    S = small_ref.shape[0]
    r = lax.broadcasted_iota(I32, (FGATE_BLK, FGATE_BLK), 0)
    c = lax.broadcasted_iota(I32, (FGATE_BLK, FGATE_BLK), 1)
    tri = jnp.where(c <= r, 1.0, 0.0).astype(BF16)
    carry = jnp.zeros((1, LANES), F32)
    for n in range(S // FGATE_BLK):
        rs = slice(n * FGATE_BLK, (n + 1) * FGATE_BLK)
        lf = _log_sigmoid(small_ref[rs, :] + fb_ref[...])
        p0 = lf.astype(BF16)
        r1 = lf - p0.astype(F32)
        p1 = r1.astype(BF16)
        p2 = (r1 - p1.astype(F32)).astype(BF16)
        blk = _dot(tri, p0) + _dot(tri, p1) + _dot(tri, p2) + carry
        fcol_ref[rs, :] = blk
        carry = blk[FGATE_BLK - 1:FGATE_BLK, :]
    ft = fcol_ref[...].T
    for h in range(FOX_HEADS):
        frow_ref[0, h] = ft[SMALL_FOX_LANE + h:SMALL_FOX_LANE + h + 1, :]


def _fgate_call(small, fb_row, B, S):
    T = B * S
    return pl.pallas_call(
        _fgate_body,
        grid=(B,),
        in_specs=[pl.BlockSpec((S, LANES), lambda b: (b, 0)),
                  pl.BlockSpec((1, LANES), lambda b: (0, 0))],
        out_specs=[pl.BlockSpec((S, LANES), lambda b: (b, 0)),
                   pl.BlockSpec((1, FOX_HEADS, 1, S), lambda b: (b, 0, 0, 0))],
        out_shape=[jax.ShapeDtypeStruct((T, LANES), F32),
                   jax.ShapeDtypeStruct((B, FOX_HEADS, 1, S), F32)],
        compiler_params=_cparams(("parallel",)),
        name="fox_gate",
    )(small, fb_row)


def _fox_body(q_ref, k_ref, v_ref, fcol_ref, frow_ref, o_ref, vt_ref, fb_ref, acc_ref):
    tq = q_ref.shape[0]
    tk = FOX_TK
    S = k_ref.shape[0]
    i = pl.program_id(1)

    @pl.when(i == 0)
    def _():
        for c in range(S // FOX_VT_BLK):
            cs = slice(c * FOX_VT_BLK, (c + 1) * FOX_VT_BLK)
            vt_ref[:, cs] = v_ref[cs, :].astype(F32).T.astype(BF16)
        for h in range(FOX_HEADS):
            fb_ref[h] = jnp.broadcast_to(fcol_ref[:, SMALL_FOX_LANE + h:SMALL_FOX_LANE + h + 1], (S, LANES))

    q0 = pl.multiple_of(i * tq, tq)
    key = lax.broadcasted_iota(I32, (tk, tq), 0)
    qry = lax.broadcasted_iota(I32, (tk, tq), 1)
    n_diag = tq // tk
    n_full = i * n_diag

    heads = []
    for h in range(FOX_HEADS):
        hs = slice(h * FOX_DH, (h + 1) * FOX_DH)
        qh = (q_ref[:, hs].astype(F32) * (FOX_DH ** -0.5)).astype(BF16)
        f_t = frow_ref[0, h, :, pl.ds(q0, tq)]
        heads.append((hs, qh, f_t))

    def update(j, states, diag=None):
        k0 = pl.multiple_of(j * tk, tk)
        zs = []
        for hs, qh, f_t in heads:
            kt = k_ref[pl.ds(k0, tk), hs]
            zs.append(lax.dot_general(kt, qh, (((1,), (1,)), ((), ())), preferred_element_type=F32))
        ps, alphas, new_states = [], [], []
        for h, (hs, qh, f_t) in enumerate(heads):
            m, l = states[h]
            f_s = fb_ref[h, pl.ds(k0, tk), :]
            z = zs[h] - jnp.concatenate([f_s] * (tq // LANES), axis=1)
            if diag is not None:
                z = jnp.where(key + diag * tk <= qry, z, -jnp.inf)
            m_new = jnp.maximum(m, jnp.max(z, axis=0, keepdims=True) + f_t)
            p = jnp.exp(z + (f_t - m_new))
            alpha = jnp.exp(m - m_new)
            new_states.append((m_new, alpha * l + jnp.sum(p, axis=0, keepdims=True)))
            ps.append(p.astype(BF16))
            alphas.append(alpha)
        for h, (hs, qh, f_t) in enumerate(heads):
            pv = _dot(vt_ref[hs, pl.ds(k0, tk)], ps[h])
            acc_ref[h] = alphas[h] * acc_ref[h] + pv
        return tuple(new_states)

    acc_ref[...] = jnp.zeros(acc_ref.shape, F32)
    states = ((jnp.full((1, tq), -jnp.inf, F32), jnp.zeros((1, tq), F32)),) * FOX_HEADS
    for d in range(n_diag):
        states = update(n_full + d, states, diag=d)

    states = lax.fori_loop(0, n_full, update, states)
    for h in range(FOX_HEADS):
        m, l = states[h]
        o_ref[:, heads[h][0]] = (acc_ref[h] / l).T.astype(o_ref.dtype)


def _fox_call(proj, fcol, frow, B, S):
    T = B * S
    tq = FOX_TQ
    nq = S // tq
    col0 = D_MAIN // D_FOX
    return pl.pallas_call(
        _fox_body,
        grid=(B, nq),
        in_specs=[pl.BlockSpec((tq, D_FOX), lambda b, i: (b * nq + i, col0)),
                  pl.BlockSpec((S, D_FOX), lambda b, i: (b, col0 + 1)),
                  pl.BlockSpec((S, D_FOX), lambda b, i: (b, col0 + 2)),
                  pl.BlockSpec((S, LANES), lambda b, i: (b, 0)),
                  pl.BlockSpec((1, FOX_HEADS, 1, S), lambda b, i: (b, 0, 0, 0))],
        out_specs=pl.BlockSpec((tq, D_FOX), lambda b, i: (b * nq + i, 0)),
        out_shape=jax.ShapeDtypeStruct((T, D_FOX), BF16),
        scratch_shapes=[pltpu.VMEM((D_FOX, S), BF16),
                        pltpu.VMEM((FOX_HEADS, S, LANES), F32),
                        pltpu.VMEM((FOX_HEADS, FOX_DH, tq), F32)],
        compiler_params=_cparams(("parallel", "arbitrary")),
        name="fox_mixer",
    )(proj, proj, proj, fcol, frow)


def _outproj_body(yc_ref, yg_ref, yf_ref, x_ref, w_ref, g_ref, wr_hi_ref, wr_lo_ref, br_ref,
                  xo_ref, h_ref, lg_ref):
    acc = x_ref[...]
    acc = acc + _dot(yc_ref[...], w_ref[0:D_CONV, :])
    acc = acc + _dot(yg_ref[...], w_ref[D_CONV:D_CONV + D_GLA, :])
    acc = acc + _dot(yf_ref[...], w_ref[D_CONV + D_GLA:D_MODEL, :])
    xo_ref[...] = acc
    hn = acc * lax.rsqrt(jnp.mean(acc * acc, axis=-1, keepdims=True) + EPS) * g_ref[...]
    h_ref[...] = hn
    lg_ref[...] = _dot3(hn, wr_hi_ref[...], wr_lo_ref[...]) + br_ref[...]


def _outproj_call(yc, yg, yf, x, w_all, layer, g_row, wr_hi, wr_lo, br_row):
    T = x.shape[0]
    tm = ROW_TILE
    const = lambda shape: pl.BlockSpec(shape, lambda i: (0, 0))
    rows = lambda width: pl.BlockSpec((tm, width), lambda i: (i, 0))
    return pl.pallas_call(
        _outproj_body,
        grid=(T // tm,),
        in_specs=[rows(D_CONV), rows(D_GLA), rows(D_FOX), rows(D_MODEL),
                  pl.BlockSpec((None, D_MODEL, D_MODEL), lambda i: (layer, 0, 0)), const((1, D_MODEL)),
                  const((D_MODEL, LANES)), const((D_MODEL, LANES)), const((1, LANES))],
        out_specs=[rows(D_MODEL), rows(D_MODEL), rows(LANES)],
        out_shape=[jax.ShapeDtypeStruct((T, D_MODEL), F32),
                   jax.ShapeDtypeStruct((T, D_MODEL), F32),
                   jax.ShapeDtypeStruct((T, LANES), F32)],
        compiler_params=_cparams(("parallel",)),
        name="outproj",
    )(yc, yg, yf, x, w_all, g_row, wr_hi, wr_lo, br_row)


def _router_body(lg_ref, ri_ref, rf_ref, cnt_ref, carry_ref):
    tr = lg_ref.shape[0]

    @pl.when(pl.program_id(0) == 0)
    def _():
        carry_ref[...] = jnp.zeros(carry_ref.shape, F32)

    lg = lg_ref[...]
    lane = lax.broadcasted_iota(I32, (tr, LANES), 1).astype(F32)
    big = float(LANES)
    neg = -jnp.inf

    is_g = lane < N_GROUPS
    gl = jnp.where(is_g, lg, neg)
    gmax = jnp.max(gl, axis=-1, keepdims=True)
    gexp = jnp.where(is_g, jnp.exp(lg - gmax), 0.0)
    gprob = gexp / jnp.sum(gexp, axis=-1, keepdims=True)
    gtop = jnp.max(gprob, axis=-1, keepdims=True)
    grp = jnp.min(jnp.where(is_g & (gprob == gtop), lane, big), axis=-1, keepdims=True)

    lo = ROUTER_EXPERT_LANE + grp * EXPERTS_PER_GROUP
    in_grp = (lane >= lo) & (lane < lo + EXPERTS_PER_GROUP)
    el = jnp.where(in_grp, lg, neg)
    v1 = jnp.max(el, axis=-1, keepdims=True)
    i1 = jnp.min(jnp.where(in_grp & (el == v1), lane, big), axis=-1, keepdims=True)
    rest = in_grp & (lane != i1)
    el2 = jnp.where(rest, lg, neg)
    v2 = jnp.max(el2, axis=-1, keepdims=True)
    i2 = jnp.min(jnp.where(rest & (el2 == v2), lane, big), axis=-1, keepdims=True)
    ex = jnp.exp(v2 - v1)
    p1 = 1.0 / (1.0 + ex)
    p2 = ex / (1.0 + ex)

    hit1 = lane == i1
    hit2 = lane == i2
    onehot = jnp.where(hit1 | hit2, 1.0, 0.0)
    r = lax.broadcasted_iota(I32, (tr, tr), 0)
    c = lax.broadcasted_iota(I32, (tr, tr), 1)
    strict = jnp.where(c < r, 1.0, 0.0).astype(BF16)
    before = _dot(strict, onehot.astype(BF16)) + carry_ref[...]
    rank1 = jnp.sum(jnp.where(hit1, before, 0.0), axis=-1, keepdims=True)
    rank2 = jnp.sum(jnp.where(hit2, before, 0.0), axis=-1, keepdims=True)
    carry_ref[...] = carry_ref[...] + jnp.sum(onehot, axis=0, keepdims=True)
    cnt_ref[...] = carry_ref[...]

    e1 = i1 - ROUTER_EXPERT_LANE
    e2 = i2 - ROUTER_EXPERT_LANE
    ri = jnp.where(lane == 0, e1, jnp.where(lane == 1, e2, jnp.where(lane == 2, rank1, jnp.where(lane == 3, rank2, 0.0))))
    ri_ref[...] = ri.astype(I32)
    rf_ref[...] = jnp.where(lane == 0, gtop * p1, jnp.where(lane == 1, gtop * p2, 0.0))


def _router_call(logits):
    T = logits.shape[0]
    tr = ROUTE_TR
    rows = pl.BlockSpec((tr, LANES), lambda i: (i, 0))
    return pl.pallas_call(
        _router_body,
        grid=(T // tr,),
        in_specs=[rows],
        out_specs=[rows, rows, pl.BlockSpec((1, LANES), lambda i: (0, 0))],
        out_shape=[jax.ShapeDtypeStruct((T, LANES), I32),
                   jax.ShapeDtypeStruct((T, LANES), F32),
                   jax.ShapeDtypeStruct((1, LANES), F32)],
        scratch_shapes=[pltpu.VMEM((1, LANES), F32)],
        compiler_params=_cparams(("arbitrary",)),
        name="router",
    )(logits)


def _moe_body(n_act_ref, tile_e_ref, tile_n_ref, first_ref, wslot_ref, next_e_ref, src_ref, dst_ref,
              h_hbm, wg_hbm, wu_hbm, wd_hbm, y_hbm,
              xbuf, ybuf, xs, wg_f, wu_f, wd_f, wg_b, wu_b, wd_b, gsem, ssem, wsem, *, expert0):
    k = pl.program_id(0)
    nv = tile_n_ref[k]
    n_act = n_act_ref[0]

    def slot_of(tile):
        return lax.rem(tile + MOE_SLOTS, MOE_SLOTS)

    def rows_moved(tile):
        nv_t = tile_n_ref[jnp.maximum(tile, 0)]
        nv_t = jnp.where(tile < 0, MOE_TM, nv_t)
        return pl.multiple_of(((nv_t + SUBLANES - 1) // SUBLANES) * SUBLANES, SUBLANES)

    def weight_copies(e, ws):
        return (pltpu.make_async_copy(wg_hbm.at[expert0 + e], wg_f.at[ws], wsem.at[ws]),
                pltpu.make_async_copy(wu_hbm.at[expert0 + e], wu_f.at[ws], wsem.at[ws]),
                pltpu.make_async_copy(wd_hbm.at[expert0 + e], wd_f.at[ws], wsem.at[ws]))

    def gather_row(tile, r, buf):
        return pltpu.make_async_copy(h_hbm.at[pl.ds(src_ref[tile * MOE_TM + r], 1)],
                                     xbuf.at[buf, pl.ds(r, 1)], gsem.at[buf])

    def scatter_row(tile, r, buf):
        return pltpu.make_async_copy(ybuf.at[buf, pl.ds(r, 1)],
                                     y_hbm.at[pl.ds(dst_ref[(tile + 1) * MOE_TM + r], 1)], ssem.at[buf])

    def wait_gather(tile, buf):
        n = rows_moved(tile)

        @pl.when(n > 0)
        def _():
            pltpu.make_async_copy(h_hbm.at[pl.ds(0, n)], xbuf.at[buf, pl.ds(0, n)], gsem.at[buf]).wait()

    def wait_scatter(tile, buf):
        n = rows_moved(tile)

        @pl.when(n > 0)
        def _():
            pltpu.make_async_copy(ybuf.at[buf, pl.ds(0, n)], y_hbm.at[pl.ds(0, n)], ssem.at[buf]).wait()

    def issue_loop(make_copy, tile, buf):
        def issue(r, carry):
            make_copy(tile, r, buf).start()
            return carry

        lax.fori_loop(0, rows_moved(tile), issue, 0)

    @pl.when(k == 0)
    def _():
        xbuf[...] = jnp.zeros(xbuf.shape, F32)
        ybuf[MOE_SLOTS - 1] = jnp.zeros((MOE_TM, D_MODEL), F32)
        for c in weight_copies(tile_e_ref[0], 0):
            c.start()
        issue_loop(gather_row, 0, 0)
        issue_loop(gather_row, 1, 1)

    @pl.when(nv > 0)
    def _():
        @pl.when(first_ref[k] == 1)
        def _():
            ws = wslot_ref[k]
            for c in weight_copies(tile_e_ref[k], ws):
                c.wait()
            nxt = next_e_ref[k]

            @pl.when(nxt >= 0)
            def _():
                for c in weight_copies(nxt, 1 - ws):
                    c.start(priority=1)

            wg_b[...] = wg_f[ws].astype(BF16)
            wu_b[...] = wu_f[ws].astype(BF16)
            wd_b[...] = wd_f[ws].astype(BF16)

        def tile_step(cur):
            prv = (cur + MOE_SLOTS - 1) % MOE_SLOTS
            wait_gather(k, cur)
            xs[...] = xbuf[cur].astype(BF16)
            n_in = rows_moved(k + 2)
            n_out = rows_moved(k - 1)
            for g in range(0, MOE_TM, SUBLANES):
                @pl.when(g < n_in)
                def _(g=g):
                    for r in range(g, g + SUBLANES):
                        gather_row(k + 2, r, prv).start(priority=0)
            for g in range(0, MOE_TM, SUBLANES):
                @pl.when(g < n_out)
                def _(g=g):
                    for r in range(g, g + SUBLANES):
                        scatter_row(k - 1, r, prv).start(priority=1)
            xb = xs[...]
            gate = _dot(xb, wg_b[...])
            up = _dot(xb, wu_b[...])
            mid = (gate * _sigmoid(gate) * up).astype(BF16)
            ybuf[cur] = _dot(mid, wd_b[...])

        for s in range(MOE_SLOTS):
            pl.when(slot_of(k) == s)(functools.partial(tile_step, s))

        @pl.when(k > 0)
        def _():
            wait_scatter(k - 2, slot_of(k - 2))

    @pl.when(k == n_act)
    def _():
        issue_loop(scatter_row, k - 1, slot_of(k - 1))
        wait_scatter(k - 2, slot_of(k - 2))

    @pl.when(k == n_act + 1)
    def _():
        wait_scatter(k - 2, slot_of(k - 2))


def _moe_call(tables, h, wg, wu, wd, layer, n_tiles):
    T = h.shape[0]
    any_spec = pl.BlockSpec(memory_space=pl.ANY)
    grid_spec = pltpu.PrefetchScalarGridSpec(
        num_scalar_prefetch=len(tables),
        grid=(n_tiles + MOE_DRAIN_STEPS,),
        in_specs=[any_spec, any_spec, any_spec, any_spec],
        out_specs=any_spec,
        scratch_shapes=[pltpu.VMEM((MOE_SLOTS, MOE_TM, D_MODEL), F32),
                        pltpu.VMEM((MOE_SLOTS, MOE_TM, D_MODEL), F32),
                        pltpu.VMEM((MOE_TM, D_MODEL), BF16),
                        pltpu.VMEM((2, D_MODEL, D_EXPERT), F32),
                        pltpu.VMEM((2, D_MODEL, D_EXPERT), F32),
                        pltpu.VMEM((2, D_EXPERT, D_MODEL), F32),
                        pltpu.VMEM((D_MODEL, D_EXPERT), BF16),
                        pltpu.VMEM((D_MODEL, D_EXPERT), BF16),
                        pltpu.VMEM((D_EXPERT, D_MODEL), BF16),
                        pltpu.SemaphoreType.DMA((MOE_SLOTS,)),
                        pltpu.SemaphoreType.DMA((MOE_SLOTS,)),
                        pltpu.SemaphoreType.DMA((2,))],
    )
    return pl.pallas_call(
        functools.partial(_moe_body, expert0=layer * N_EXPERTS),
        grid_spec=grid_spec,
        out_shape=jax.ShapeDtypeStruct((2 * T + MOE_TM, D_MODEL), F32),
        compiler_params=_cparams(("arbitrary",)),
        name="moe_experts",
    )(*tables, h, wg, wu, wd)


def _tables_body(e1_ref, e2_ref, r1_ref, r2_ref, cnt_ref,
                 n_act_ref, tile_e_ref, tile_n_ref, first_ref, wslot_ref, next_e_ref, src_ref, dst_ref,
                 row0_ref, after_ref):
    T = e1_ref.shape[0]
    n_steps = tile_e_ref.shape[0]
    n_rows = src_ref.shape[0]

    nxt = jnp.int32(-1)
    for e in reversed(range(N_EXPERTS)):
        after_ref[e] = nxt
        nxt = jnp.where(cnt_ref[e] > 0, jnp.int32(e), nxt)

    def padding(t, carry):
        base = t * MOE_TM
        for r in range(MOE_TM):
            src_ref[base + r] = 0
            dst_ref[base + r] = 2 * T + r
        return carry

    lax.fori_loop(0, n_rows // MOE_TM, padding, 0)

    k = jnp.int32(0)
    order = jnp.int32(0)
    for e in range(N_EXPERTS):
        n = cnt_ref[e]
        nt = (n + (MOE_TM - 1)) // MOE_TM
        row0_ref[e] = k * MOE_TM

        def tile(i, carry, e=e, n=n, k=k, order=order):
            tile_e_ref[k + i] = e
            tile_n_ref[k + i] = jnp.minimum(n - i * MOE_TM, MOE_TM)
            first_ref[k + i] = (i == 0).astype(I32)
            wslot_ref[k + i] = order & 1
            next_e_ref[k + i] = after_ref[e]
            return carry

        lax.fori_loop(0, nt, tile, 0)
        k = k + nt
        order = order + (nt > 0).astype(I32)
    n_act_ref[0] = k

    def idle(i, carry):
        tile_e_ref[i] = 0
        tile_n_ref[i] = 0
        first_ref[i] = 0
        wslot_ref[i] = 0
        next_e_ref[i] = -1
        return carry

    lax.fori_loop(k, n_steps, idle, 0)

    def assign(t, carry):
        p1 = row0_ref[e1_ref[t]] + r1_ref[t]
        p2 = row0_ref[e2_ref[t]] + r2_ref[t]
        src_ref[p1] = t
        src_ref[p2] = t
        dst_ref[p1 + MOE_TM] = t
        dst_ref[p2 + MOE_TM] = T + t
        return carry

    lax.fori_loop(0, T, assign, 0, unroll=8)


def _route_tables(route_i, counts, T, n_tiles):
    n_steps = n_tiles + MOE_DRAIN_STEPS
    n_rows = n_steps * MOE_TM
    cnt = counts[0, ROUTER_EXPERT_LANE:ROUTER_EXPERT_LANE + N_EXPERTS].astype(I32)
    smem = pl.BlockSpec(memory_space=pltpu.SMEM)
    vec = lambda n: jax.ShapeDtypeStruct((n,), I32)
    return pl.pallas_call(
        _tables_body,
        in_specs=[smem] * 5,
        out_specs=[smem] * 8,
        out_shape=[vec(1)] + [vec(n_steps)] * 5 + [vec(n_rows)] * 2,
        scratch_shapes=[pltpu.SMEM((N_EXPERTS,), I32), pltpu.SMEM((N_EXPERTS,), I32)],
        name="route_tables",
    )(route_i[:, 0], route_i[:, 1], route_i[:, 2], route_i[:, 3], cnt)


def _pad_lanes(w, offset=0):
    return jnp.pad(w, ((0, 0), (offset, LANES - offset - w.shape[1])))


def kernel(x, norm1_g, w_in, conv_w, conv_b, conv_ln_g, conv_ln_b, gla_w2, gla_b2, gla_norm_g, fox_f_b, w_out, norm2_g, router_group_w, router_group_b, router_expert_w, router_expert_b, ffn_w_gate, ffn_w_up, ffn_w_down, final_norm_g):
    B, S, D = x.shape
    T = B * S
    depth = w_in.shape[0]
    n_tiles = (2 * T) // MOE_TM + N_EXPERTS

    wg_all = ffn_w_gate.reshape(depth * N_EXPERTS, D_MODEL, D_EXPERT)
    wu_all = ffn_w_up.reshape(depth * N_EXPERTS, D_MODEL, D_EXPERT)
    wd_all = ffn_w_down.reshape(depth * N_EXPERTS, D_EXPERT, D_MODEL)
    w_proj, ws_hi, ws_lo = _prep_in_call(w_in)
    w_o = _prep_out_call(w_out)

    xt = x.reshape(T, D)
    h, small = _norm_call(xt, norm1_g[0][None, :], small_w=(ws_hi, ws_lo, 0))
    for l in range(depth):
        proj = _inproj_call(h, w_proj, l)

        y_conv = _conv_call(proj, jnp.pad(conv_w[l], ((0, CONV_PAD - CONV_WIDTH), (0, 0))), conv_b[l][None, :],
                            conv_ln_g[l][None, :], conv_ln_b[l][None, :], B, S)
        w2_hi, w2_lo = _split_bf16(jnp.pad(gla_w2[l], ((0, LANES - GLA_RANK), (0, 0))))
        y_gla = _gla_call(proj, small, w2_hi, w2_lo, gla_b2[l][None, :], gla_norm_g[l][None, :], B, S)
        fcol, frow = _fgate_call(small, _pad_lanes(fox_f_b[l][None, :], SMALL_FOX_LANE), B, S)
        y_fox = _fox_call(proj, fcol, frow, B, S)

        w_route = jnp.concatenate([router_group_w[l],
                                   router_expert_w[l].transpose(1, 0, 2).reshape(D_MODEL, N_EXPERTS)], axis=1)
        wr_hi, wr_lo = _split_bf16(_pad_lanes(w_route))
        b_route = _pad_lanes(jnp.concatenate([router_group_b[l], router_expert_b[l].reshape(-1)])[None, :])
        xt, h2, logits = _outproj_call(y_conv, y_gla, y_fox, xt, w_o, l, norm2_g[l][None, :],
                                       wr_hi, wr_lo, b_route)

        route_i, gates, counts = _router_call(logits)
        tables = _route_tables(route_i, counts, T, n_tiles)
        y2 = _moe_call(tables, h2, wg_all, wu_all, wd_all, l, n_tiles)

        if l + 1 < depth:
            xt, h, small = _norm_call(xt, norm1_g[l + 1][None, :], moe=(y2, gates), small_w=(ws_hi, ws_lo, l + 1))
        else:
            (out,) = _norm_call(xt, final_norm_g[None, :], moe=(y2, gates), out_dtype=F32)
    return out.reshape(B, S, D)
```

```python
import functools

import jax
import jax.numpy as jnp
from jax import lax
from jax.experimental import pallas as pl
from jax.experimental.pallas import tpu as pltpu

F32 = jnp.float32
BF16 = jnp.bfloat16
I32 = jnp.int32

D_MODEL = 2048
EPS = 1e-6
D_CONV = 512
CONV_WIDTH = 31
D_GLA = 1024
GLA_HEADS = 4
GLA_DK = 128
GLA_DV = 256
GLA_KEY = GLA_HEADS * GLA_DK
GLA_RANK = 16
GLA_GATE_NORMALIZER = 16.0
GLA_CHUNK = 64
D_FOX = 512
FOX_HEADS = 4
FOX_DH = 128
N_GROUPS = 4
EXPERTS_PER_GROUP = 8
N_EXPERTS = N_GROUPS * EXPERTS_PER_GROUP
D_EXPERT = 512

LANES = 128
SUBLANES = 8
D_MAIN = 2 * D_CONV + 2 * GLA_KEY + 2 * D_GLA
D_PROJ = D_MAIN + 3 * D_FOX
D_IN = D_MAIN + GLA_RANK + 3 * D_FOX + FOX_HEADS
PREP_N = 512
PREP_LAST = 24
SMALL_FOX_LANE = GLA_RANK
ROUTER_EXPERT_LANE = N_GROUPS

VMEM_LIMIT = 56 * 1024 * 1024

ROW_TILE = 256
MM_TM = 1024
MM_TN = 1408
GLA_TS = 256
FOX_TQ = 256
FOX_TK = 256
FOX_VT_BLK = 512
CONV_RC = 64
ROUTE_TR = 512
MOE_TM = 256
MOE_SLOTS = 3
MOE_DRAIN_STEPS = 2


def _cparams(sem):
    return pltpu.CompilerParams(dimension_semantics=sem, vmem_limit_bytes=VMEM_LIMIT)


def _split_bf16(x):
    hi = x.astype(BF16)
    lo = (x - hi.astype(F32)).astype(BF16)
    return hi, lo


def _dot(a, b):
    return jnp.dot(a, b, preferred_element_type=F32)


def _dot3(a, b_hi, b_lo):
    a_hi, a_lo = _split_bf16(a)
    return _dot(a_hi, b_hi) + _dot(a_lo, b_hi) + _dot(a_hi, b_lo)


def _sigmoid(x):
    return 1.0 / (1.0 + jnp.exp(-x))


def _log_sigmoid(x):
    return jnp.minimum(x, 0.0) - jnp.log(1.0 + jnp.exp(-jnp.abs(x)))


def _norm_body(*refs, combine, project):
    it = iter(refs)
    x_ref = next(it)
    if combine:
        ya_ref, yb_ref, gates_ref = next(it), next(it), next(it)
    g_ref = next(it)
    if project:
        ws_hi_ref, ws_lo_ref = next(it), next(it)
    if combine and project:
        xo_ref = next(it)
    h_ref = next(it)
    if project:
        small_ref = next(it)

    x = x_ref[...]
    if combine:
        gates = gates_ref[...]
        x = x + gates[:, 0:1] * ya_ref[...] + gates[:, 1:2] * yb_ref[...]
        if project:
            xo_ref[...] = x
    y = x * lax.rsqrt(jnp.mean(x * x, axis=-1, keepdims=True) + EPS) * g_ref[...]
    h_ref[...] = y.astype(h_ref.dtype)
    if project:
        y_hi, y_lo = _split_bf16(y)
        ws_hi = ws_hi_ref[...]
        small_ref[...] = _dot_nt(y_hi, ws_hi) + _dot_nt(y_lo, ws_hi) + _dot_nt(y_hi, ws_lo_ref[...])


def _norm_call(x, g_row, *, moe=None, small_w=None, out_dtype=BF16):
    T = x.shape[0]
    tm = ROW_TILE
    combine = moe is not None
    project = small_w is not None
    row_spec = pl.BlockSpec((tm, D_MODEL), lambda i: (i, 0))
    lane_spec = pl.BlockSpec((tm, LANES), lambda i: (i, 0))
    const = lambda shape: pl.BlockSpec(shape, lambda i: (0, 0))
    nblk = T // tm
    ins, in_specs = [x], [row_spec]
    if combine:
        y2, gates = moe
        ins += [y2, y2, gates]
        in_specs += [row_spec, pl.BlockSpec((tm, D_MODEL), lambda i: (i + nblk, 0)), lane_spec]
    ins.append(g_row)
    in_specs.append(const((1, D_MODEL)))
    if project:
        ws_hi, ws_lo, layer = small_w
        ins += [ws_hi, ws_lo]
        in_specs += [pl.BlockSpec((None, LANES, D_MODEL), lambda i: (layer, 0, 0))] * 2
    out_shape, out_specs = [], []
    if combine and project:
        out_shape.append(jax.ShapeDtypeStruct((T, D_MODEL), F32))
        out_specs.append(row_spec)
    out_shape.append(jax.ShapeDtypeStruct((T, D_MODEL), out_dtype))
    out_specs.append(row_spec)
    if project:
        out_shape.append(jax.ShapeDtypeStruct((T, LANES), F32))
        out_specs.append(lane_spec)
    return pl.pallas_call(
        functools.partial(_norm_body, combine=combine, project=project),
        grid=(nblk,),
        in_specs=in_specs,
        out_specs=out_specs,
        out_shape=out_shape,
        compiler_params=_cparams(("parallel",)),
        name="norm",
    )(*ins)


def _prep_in_body(w_hbm, wp_ref, ws_hi_ref, ws_lo_ref, buf, tail_ref, sem):
    j = pl.program_id(0)
    depth = buf.shape[1]
    n_main = D_MAIN // PREP_N
    n_last = D_IN // PREP_N
    keep = PREP_N - GLA_RANK
    slot = j % 2

    def full_block(jj, s, l):
        return pltpu.make_async_copy(w_hbm.at[pl.ds(jj * PREP_N, PREP_N), l, :], buf.at[s, l], sem.at[s])

    def last_block(s, l):
        return pltpu.make_async_copy(w_hbm.at[pl.ds(D_IN - PREP_LAST, PREP_LAST), l, :],
                                     buf.at[s, l, pl.ds(0, PREP_LAST)], sem.at[s])

    def for_block(jj, s, action):
        @pl.when(jj < n_last)
        def _():
            for l in range(depth):
                action(full_block(jj, s, l))

        @pl.when(jj == n_last)
        def _():
            for l in range(depth):
                action(last_block(s, l))

    @pl.when(j == 0)
    def _():
        for_block(j, slot, lambda c: c.start())
        ws_hi_ref[...] = jnp.zeros(ws_hi_ref.shape, BF16)
        ws_lo_ref[...] = jnp.zeros(ws_lo_ref.shape, BF16)

    @pl.when(j < n_last)
    def _():
        for_block(j + 1, 1 - slot, lambda c: c.start())

    for_block(j, slot, lambda c: c.wait())

    head0 = PREP_N * n_last - (D_IN - PREP_LAST)
    for l in range(depth):
        @pl.when(j < n_main)
        def _(l=l):
            wp_ref[l] = buf[slot, l].astype(BF16)

        @pl.when(j == n_main)
        def _(l=l):
            hi, lo = _split_bf16(buf[slot, l, 0:GLA_RANK, :])
            ws_hi_ref[l, 0:GLA_RANK, :] = hi
            ws_lo_ref[l, 0:GLA_RANK, :] = lo

        @pl.when(j > n_main)
        def _(l=l):
            wp_ref[l, 0:keep, :] = tail_ref[l, 0:keep, :]

        @pl.when((j > n_main) & (j < n_last))
        def _(l=l):
            wp_ref[l, keep:PREP_N, :] = buf[slot, l, 0:GLA_RANK, :].astype(BF16)

        @pl.when((j >= n_main) & (j < n_last))
        def _(l=l):
            tail_ref[l, 0:keep, :] = buf[slot, l, GLA_RANK:PREP_N, :].astype(BF16)

        @pl.when(j == n_last)
        def _(l=l):
            w = buf[slot, l, 0:2 * PREP_LAST, :]
            wp_ref[l, keep:PREP_N, :] = w[head0:head0 + GLA_RANK].astype(BF16)
            row = lax.broadcasted_iota(I32, (GLA_RANK, D_MODEL), 0)
            logits = jnp.where(row < FOX_HEADS, w[head0 + GLA_RANK:head0 + 2 * GLA_RANK], 0.0)
            hi, lo = _split_bf16(logits)
            ws_hi_ref[l, GLA_RANK:2 * GLA_RANK, :] = hi
            ws_lo_ref[l, GLA_RANK:2 * GLA_RANK, :] = lo


def _prep_in_call(w_in):
    depth = w_in.shape[0]
    n_main = D_MAIN // PREP_N
    w_t = jnp.transpose(w_in, (2, 0, 1))
    out_block = lambda j: (0, jnp.where(j <= n_main, jnp.minimum(j, n_main - 1), j - 1), 0)
    const = lambda j: (0, 0, 0)
    return pl.pallas_call(
        _prep_in_body,
        grid=(D_IN // PREP_N + 1,),
        in_specs=[pl.BlockSpec(memory_space=pl.ANY)],
        out_specs=[pl.BlockSpec((depth, PREP_N, D_MODEL), out_block),
                   pl.BlockSpec((depth, LANES, D_MODEL), const),
                   pl.BlockSpec((depth, LANES, D_MODEL), const)],
        out_shape=[jax.ShapeDtypeStruct((depth, D_PROJ, D_MODEL), BF16),
                   jax.ShapeDtypeStruct((depth, LANES, D_MODEL), BF16),
                   jax.ShapeDtypeStruct((depth, LANES, D_MODEL), BF16)],
        scratch_shapes=[pltpu.VMEM((2, depth, PREP_N, D_MODEL), F32),
                        pltpu.VMEM((depth, PREP_N, D_MODEL), BF16),
                        pltpu.SemaphoreType.DMA((2,))],
        compiler_params=_cparams(("arbitrary",)),
        name="weight_prep_in",
    )(w_t)


def _prep_out_body(w_ref, o_ref):
    o_ref[...] = w_ref[...].astype(BF16)


def _prep_out_call(w_out):
    depth = w_out.shape[0]
    spec = pl.BlockSpec((None, PREP_N, D_MODEL), lambda l, i: (l, i, 0))
    return pl.pallas_call(
        _prep_out_body,
        grid=(depth, D_MODEL // PREP_N),
        in_specs=[spec],
        out_specs=spec,
        out_shape=jax.ShapeDtypeStruct((depth, D_MODEL, D_MODEL), BF16),
        compiler_params=_cparams(("parallel", "parallel")),
        name="weight_prep_out",
    )(w_out)


def _dot_nt(a, b_t):
    return lax.dot_general(a, b_t, (((1,), (1,)), ((), ())), preferred_element_type=F32)


def _matmul_body(h_ref, w_ref, o_ref):
    o_ref[...] = _dot_nt(h_ref[...], w_ref[...]).astype(o_ref.dtype)


def _inproj_call(h, w_all, layer):
    T = h.shape[0]
    tm = min(MM_TM, T)
    return pl.pallas_call(
        _matmul_body,
        grid=(T // tm, D_PROJ // MM_TN),
        in_specs=[pl.BlockSpec((tm, D_MODEL), lambda i, j: (i, 0)),
                  pl.BlockSpec((None, MM_TN, D_MODEL), lambda i, j: (layer, j, 0))],
        out_specs=pl.BlockSpec((tm, MM_TN), lambda i, j: (i, j)),
        out_shape=jax.ShapeDtypeStruct((T, D_PROJ), BF16),
        compiler_params=_cparams(("parallel", "parallel")),
        name="inproj",
    )(h, w_all)


CONV_PAD = 32


def _conv_body(a_ref, g_ref, w_ref, b_ref, lng_ref, lnb_ref, o_ref, u_ref, sh_ref):
    S = a_ref.shape[0]
    u_ref[0:CONV_PAD, :] = jnp.zeros((CONV_PAD, D_CONV), F32)
    u_ref[CONV_PAD:CONV_PAD + S, :] = a_ref[...].astype(F32) * _sigmoid(g_ref[...].astype(F32))
    bias = b_ref[...]
    lng = lng_ref[...]
    lnb = lnb_ref[...]
    first = CONV_PAD - (CONV_WIDTH - 1)

    def chunk(c, carry):
        r0 = pl.multiple_of(c * CONV_RC, CONV_RC)
        acc = jnp.broadcast_to(bias, (CONV_RC, D_CONV))
        win = u_ref[pl.ds(r0, CONV_RC + CONV_PAD), :]
        for s in range(1, SUBLANES):
            sh_ref[s - 1] = win[s:s + CONV_RC + CONV_PAD - SUBLANES, :]
        for j in range(CONV_WIDTH):
            s = (first + j) % SUBLANES
            a = first + j - s
            tap = win[a:a + CONV_RC, :] if s == 0 else sh_ref[s - 1, a:a + CONV_RC, :]
            acc = acc + w_ref[j:j + 1, :] * tap
        mu = jnp.mean(acc, axis=-1, keepdims=True)
        d = acc - mu
        var = jnp.mean(d * d, axis=-1, keepdims=True)
        yn = d * lax.rsqrt(var + EPS) * lng + lnb
        o_ref[pl.ds(r0, CONV_RC), :] = (yn * _sigmoid(yn)).astype(o_ref.dtype)
        return carry

    lax.fori_loop(0, S // CONV_RC, chunk, 0)


def _conv_call(proj, w_pad, b_row, lng_row, lnb_row, B, S):
    T = B * S
    const = lambda shape: pl.BlockSpec(shape, lambda b: (0, 0))
    return pl.pallas_call(
        _conv_body,
        grid=(B,),
        in_specs=[pl.BlockSpec((S, D_CONV), lambda b: (b, 0)),
                  pl.BlockSpec((S, D_CONV), lambda b: (b, 1)),
                  const((CONV_PAD, D_CONV)), const((1, D_CONV)), const((1, D_CONV)), const((1, D_CONV))],
        out_specs=pl.BlockSpec((S, D_CONV), lambda b: (b, 0)),
        out_shape=jax.ShapeDtypeStruct((T, D_CONV), BF16),
        scratch_shapes=[pltpu.VMEM((CONV_PAD + S, D_CONV), F32),
                        pltpu.VMEM((SUBLANES - 1, CONV_RC + CONV_PAD - SUBLANES, D_CONV), F32)],
        compiler_params=_cparams(("parallel",)),
        name="conv_mixer",
    )(proj, proj, w_pad, b_row, lng_row, lnb_row)


def _gla_body(q_ref, k_ref, v_ref, g_ref, low_ref, w2hi_ref, w2lo_ref, b2_ref, ng_ref, o_ref, st_ref):
    ts = q_ref.shape[0]
    nchunk = ts // GLA_CHUNK

    @pl.when(pl.program_id(1) == 0)
    def _():
        st_ref[...] = jnp.zeros(st_ref.shape, F32)

    la = _log_sigmoid(_dot3(low_ref[...], w2hi_ref[...], w2lo_ref[...]) + b2_ref[...]) * (1.0 / GLA_GATE_NORMALIZER)
    r = lax.broadcasted_iota(I32, (2 * ts, ts), 0)
    c = lax.broadcasted_iota(I32, (2 * ts, ts), 1)
    rr = jnp.where(r >= ts, r - ts, r)
    same_chunk = (rr // GLA_CHUNK) == (c // GLA_CHUNK)
    sel = jnp.where(same_chunk & ((r >= ts) | (c <= rr)), 1.0, 0.0).astype(BF16)
    la_hi, la_lo = _split_bf16(la)
    sums = _dot(sel, la_hi) + _dot(sel, la_lo)
    cum = sums[0:ts, :]
    last = sums[ts:2 * ts, :]
    e_q = jnp.exp(cum)
    e_inv = jnp.exp(-cum)
    e_end = jnp.exp(last - cum)
    e_last = jnp.exp(last)

    qr = lax.broadcasted_iota(I32, (ts, ts), 0)
    qc = lax.broadcasted_iota(I32, (ts, ts), 1)
    att_mask = ((qr // GLA_CHUNK) == (qc // GLA_CHUNK)) & (qc <= qr)
    ng = ng_ref[...]

    for h in range(GLA_HEADS):
        ks = slice(h * GLA_DK, (h + 1) * GLA_DK)
        vs = slice(h * GLA_DV, (h + 1) * GLA_DV)
        qh = q_ref[:, ks].astype(F32) * (GLA_DK ** -0.5)
        kh = k_ref[:, ks].astype(F32)
        vh = v_ref[:, vs]
        q_dec = (qh * e_q[:, ks]).astype(BF16)
        k_inv = (kh * e_inv[:, ks]).astype(BF16)
        k_end = (kh * e_end[:, ks]).astype(BF16)
        att = lax.dot_general(q_dec, k_inv, (((1,), (1,)), ((), ())), preferred_element_type=F32)
        att = jnp.where(att_mask, att, 0.0).astype(BF16)
        o_intra = _dot(att, vh)
        state = st_ref[h]
        outs = []
        for n in range(nchunk):
            rs = slice(n * GLA_CHUNK, (n + 1) * GLA_CHUNK)
            inter = lax.dot_general(q_dec[rs], state.astype(BF16), (((1,), (1,)), ((), ())),
                                    preferred_element_type=F32)
            outs.append(o_intra[rs] + inter)
            kv_t = lax.dot_general(vh[rs], k_end[rs], (((0,), (0,)), ((), ())), preferred_element_type=F32)
            state = state * e_last[n * GLA_CHUNK:n * GLA_CHUNK + 1, ks] + kv_t
        st_ref[h] = state
        o = jnp.concatenate(outs, axis=0)
        o = o * lax.rsqrt(jnp.mean(o * o, axis=-1, keepdims=True) + EPS) * ng
        gate = g_ref[:, vs].astype(F32)
        o_ref[:, vs] = (o * (gate * _sigmoid(gate))).astype(o_ref.dtype)


def _gla_call(proj, small, w2_hi, w2_lo, b2_row, ng_row, B, S):
    T = B * S
    ts = GLA_TS
    nst = S // ts
    row = lambda b, s: b * nst + s
    const = lambda shape: pl.BlockSpec(shape, lambda b, s: (0, 0))
    return pl.pallas_call(
        _gla_body,
        grid=(B, nst),
        in_specs=[pl.BlockSpec((ts, GLA_KEY), lambda b, s: (row(b, s), 2)),
                  pl.BlockSpec((ts, GLA_KEY), lambda b, s: (row(b, s), 3)),
                  pl.BlockSpec((ts, D_GLA), lambda b, s: (row(b, s), 2)),
                  pl.BlockSpec((ts, D_GLA), lambda b, s: (row(b, s), 3)),
                  pl.BlockSpec((ts, LANES), lambda b, s: (row(b, s), 0)),
                  const((LANES, GLA_KEY)), const((LANES, GLA_KEY)), const((1, GLA_KEY)), const((1, GLA_DV))],
        out_specs=pl.BlockSpec((ts, D_GLA), lambda b, s: (row(b, s), 0)),
        out_shape=jax.ShapeDtypeStruct((T, D_GLA), BF16),
        scratch_shapes=[pltpu.VMEM((GLA_HEADS, GLA_DV, GLA_DK), F32)],
        compiler_params=_cparams(("parallel", "arbitrary")),
        name="gla_mixer",
    )(proj, proj, proj, proj, small, w2_hi, w2_lo, b2_row, ng_row)


FGATE_BLK = 256


def _fgate_body(small_ref, fb_ref, fcol_ref, frow_ref):
    S = small_ref.shape[0]
    r = lax.broadcasted_iota(I32, (FGATE_BLK, FGATE_BLK), 0)
    c = lax.broadcasted_iota(I32, (FGATE_BLK, FGATE_BLK), 1)
    tri = jnp.where(c <= r, 1.0, 0.0).astype(BF16)
    carry = jnp.zeros((1, LANES), F32)
    for n in range(S // FGATE_BLK):
        rs = slice(n * FGATE_BLK, (n + 1) * FGATE_BLK)
        lf = _log_sigmoid(small_ref[rs, :] + fb_ref[...])
        p0 = lf.astype(BF16)
        r1 = lf - p0.astype(F32)
        p1 = r1.astype(BF16)
        p2 = (r1 - p1.astype(F32)).astype(BF16)
        blk = _dot(tri, p0) + _dot(tri, p1) + _dot(tri, p2) + carry
        fcol_ref[rs, :] = blk
        carry = blk[FGATE_BLK - 1:FGATE_BLK, :]
    ft = fcol_ref[...].T
    for h in range(FOX_HEADS):
        frow_ref[0, h] = ft[SMALL_FOX_LANE + h:SMALL_FOX_LANE + h + 1, :]


def _fgate_call(small, fb_row, B, S):
    T = B * S
    return pl.pallas_call(
        _fgate_body,
        grid=(B,),
        in_specs=[pl.BlockSpec((S, LANES), lambda b: (b, 0)),
                  pl.BlockSpec((1, LANES), lambda b: (0, 0))],
        out_specs=[pl.BlockSpec((S, LANES), lambda b: (b, 0)),
                   pl.BlockSpec((1, FOX_HEADS, 1, S), lambda b: (b, 0, 0, 0))],
        out_shape=[jax.ShapeDtypeStruct((T, LANES), F32),
                   jax.ShapeDtypeStruct((B, FOX_HEADS, 1, S), F32)],
        compiler_params=_cparams(("parallel",)),
        name="fox_gate",
    )(small, fb_row)


def _fox_body(q_ref, k_ref, v_ref, fcol_ref, frow_ref, o_ref, vt_ref, fb_ref, acc_ref):
    tq = q_ref.shape[0]
    tk = FOX_TK
    S = k_ref.shape[0]
    i = pl.program_id(1)

    @pl.when(i == 0)
    def _():
        for c in range(S // FOX_VT_BLK):
            cs = slice(c * FOX_VT_BLK, (c + 1) * FOX_VT_BLK)
            vt_ref[:, cs] = v_ref[cs, :].astype(F32).T.astype(BF16)
        for h in range(FOX_HEADS):
            fb_ref[h] = jnp.broadcast_to(fcol_ref[:, SMALL_FOX_LANE + h:SMALL_FOX_LANE + h + 1], (S, LANES))

    q0 = pl.multiple_of(i * tq, tq)
    key = lax.broadcasted_iota(I32, (tk, tq), 0)
    qry = lax.broadcasted_iota(I32, (tk, tq), 1)
    n_diag = tq // tk
    n_full = i * n_diag

    heads = []
    for h in range(FOX_HEADS):
        hs = slice(h * FOX_DH, (h + 1) * FOX_DH)
        qh = (q_ref[:, hs].astype(F32) * (FOX_DH ** -0.5)).astype(BF16)
        f_t = frow_ref[0, h, :, pl.ds(q0, tq)]
        heads.append((hs, qh, f_t))

    def update(j, states, diag=None):
        k0 = pl.multiple_of(j * tk, tk)
        zs = []
        for hs, qh, f_t in heads:
            kt = k_ref[pl.ds(k0, tk), hs]
            zs.append(lax.dot_general(kt, qh, (((1,), (1,)), ((), ())), preferred_element_type=F32))
        ps, alphas, new_states = [], [], []
        for h, (hs, qh, f_t) in enumerate(heads):
            m, l = states[h]
            f_s = fb_ref[h, pl.ds(k0, tk), :]
            z = zs[h] - jnp.concatenate([f_s] * (tq // LANES), axis=1)
            if diag is not None:
                z = jnp.where(key + diag * tk <= qry, z, -jnp.inf)
            m_new = jnp.maximum(m, jnp.max(z, axis=0, keepdims=True) + f_t)
            p = jnp.exp(z + (f_t - m_new))
            alpha = jnp.exp(m - m_new)
            new_states.append((m_new, alpha * l + jnp.sum(p, axis=0, keepdims=True)))
            ps.append(p.astype(BF16))
            alphas.append(alpha)
        for h, (hs, qh, f_t) in enumerate(heads):
            pv = _dot(vt_ref[hs, pl.ds(k0, tk)], ps[h])
            acc_ref[h] = alphas[h] * acc_ref[h] + pv
        return tuple(new_states)

    acc_ref[...] = jnp.zeros(acc_ref.shape, F32)
    states = ((jnp.full((1, tq), -jnp.inf, F32), jnp.zeros((1, tq), F32)),) * FOX_HEADS
    for d in range(n_diag):
        states = update(n_full + d, states, diag=d)

    states = lax.fori_loop(0, n_full, update, states)
    for h in range(FOX_HEADS):
        m, l = states[h]
        o_ref[:, heads[h][0]] = (acc_ref[h] / l).T.astype(o_ref.dtype)


def _fox_call(proj, fcol, frow, B, S):
    T = B * S
    tq = FOX_TQ
    nq = S // tq
    col0 = D_MAIN // D_FOX
    return pl.pallas_call(
        _fox_body,
        grid=(B, nq),
        in_specs=[pl.BlockSpec((tq, D_FOX), lambda b, i: (b * nq + i, col0)),
                  pl.BlockSpec((S, D_FOX), lambda b, i: (b, col0 + 1)),
                  pl.BlockSpec((S, D_FOX), lambda b, i: (b, col0 + 2)),
                  pl.BlockSpec((S, LANES), lambda b, i: (b, 0)),
                  pl.BlockSpec((1, FOX_HEADS, 1, S), lambda b, i: (b, 0, 0, 0))],
        out_specs=pl.BlockSpec((tq, D_FOX), lambda b, i: (b * nq + i, 0)),
        out_shape=jax.ShapeDtypeStruct((T, D_FOX), BF16),
        scratch_shapes=[pltpu.VMEM((D_FOX, S), BF16),
                        pltpu.VMEM((FOX_HEADS, S, LANES), F32),
                        pltpu.VMEM((FOX_HEADS, FOX_DH, tq), F32)],
        compiler_params=_cparams(("parallel", "arbitrary")),
        name="fox_mixer",
    )(proj, proj, proj, fcol, frow)


def _outproj_body(yc_ref, yg_ref, yf_ref, x_ref, w_ref, g_ref, wr_ref, br_ref,
                  xo_ref, h_ref, lg_ref):
    acc = x_ref[...]
    acc = acc + _dot(yc_ref[...], w_ref[0:D_CONV, :])
    acc = acc + _dot(yg_ref[...], w_ref[D_CONV:D_CONV + D_GLA, :])
    acc = acc + _dot(yf_ref[...], w_ref[D_CONV + D_GLA:D_MODEL, :])
    xo_ref[...] = acc
    hn = acc * lax.rsqrt(jnp.mean(acc * acc, axis=-1, keepdims=True) + EPS) * g_ref[...]
    h_ref[...] = hn
    hn_hi, hn_lo = _split_bf16(hn)
    both = _dot(hn_hi, wr_ref[...])
    lg_ref[...] = both[:, 0:LANES] + both[:, LANES:2 * LANES] + _dot(hn_lo, wr_ref[:, 0:LANES]) + br_ref[...]


def _outproj_call(yc, yg, yf, x, w_all, layer, g_row, wr_hi, wr_lo, br_row):
    T = x.shape[0]
    tm = ROW_TILE
    const = lambda shape: pl.BlockSpec(shape, lambda i: (0, 0))
    rows = lambda width: pl.BlockSpec((tm, width), lambda i: (i, 0))
    return pl.pallas_call(
        _outproj_body,
        grid=(T // tm,),
        in_specs=[rows(D_CONV), rows(D_GLA), rows(D_FOX), rows(D_MODEL),
                  pl.BlockSpec((None, D_MODEL, D_MODEL), lambda i: (layer, 0, 0)), const((1, D_MODEL)),
                  const((D_MODEL, 2 * LANES)), const((1, LANES))],
        out_specs=[rows(D_MODEL), rows(D_MODEL), rows(LANES)],
        out_shape=[jax.ShapeDtypeStruct((T, D_MODEL), F32),
                   jax.ShapeDtypeStruct((T, D_MODEL), F32),
                   jax.ShapeDtypeStruct((T, LANES), F32)],
        compiler_params=_cparams(("parallel",)),
        name="outproj",
    )(yc, yg, yf, x, w_all, g_row, jnp.concatenate([wr_hi, wr_lo], axis=1), br_row)


def _router_body(lg_ref, ri_ref, rf_ref, cnt_ref, carry_ref):
    tr = lg_ref.shape[0]

    @pl.when(pl.program_id(0) == 0)
    def _():
        carry_ref[...] = jnp.zeros(carry_ref.shape, F32)

    lg = lg_ref[...]
    lane = lax.broadcasted_iota(I32, (tr, LANES), 1).astype(F32)
    big = float(LANES)
    neg = -jnp.inf

    is_g = lane < N_GROUPS
    gl = jnp.where(is_g, lg, neg)
    gmax = jnp.max(gl, axis=-1, keepdims=True)
    gexp = jnp.where(is_g, jnp.exp(lg - gmax), 0.0)
    gprob = gexp / jnp.sum(gexp, axis=-1, keepdims=True)
    gtop = jnp.max(gprob, axis=-1, keepdims=True)
    grp = jnp.min(jnp.where(is_g & (gprob == gtop), lane, big), axis=-1, keepdims=True)

    lo = ROUTER_EXPERT_LANE + grp * EXPERTS_PER_GROUP
    in_grp = (lane >= lo) & (lane < lo + EXPERTS_PER_GROUP)
    el = jnp.where(in_grp, lg, neg)
    v1 = jnp.max(el, axis=-1, keepdims=True)
    i1 = jnp.min(jnp.where(in_grp & (el == v1), lane, big), axis=-1, keepdims=True)
    rest = in_grp & (lane != i1)
    el2 = jnp.where(rest, lg, neg)
    v2 = jnp.max(el2, axis=-1, keepdims=True)
    i2 = jnp.min(jnp.where(rest & (el2 == v2), lane, big), axis=-1, keepdims=True)
    ex = jnp.exp(v2 - v1)
    p1 = 1.0 / (1.0 + ex)
    p2 = ex / (1.0 + ex)

    hit1 = lane == i1
    hit2 = lane == i2
    onehot = jnp.where(hit1 | hit2, 1.0, 0.0)
    r = lax.broadcasted_iota(I32, (tr, tr), 0)
    c = lax.broadcasted_iota(I32, (tr, tr), 1)
    strict = jnp.where(c < r, 1.0, 0.0).astype(BF16)
    before = _dot(strict, onehot.astype(BF16)) + carry_ref[...]
    rank1 = jnp.sum(jnp.where(hit1, before, 0.0), axis=-1, keepdims=True)
    rank2 = jnp.sum(jnp.where(hit2, before, 0.0), axis=-1, keepdims=True)
    carry_ref[...] = carry_ref[...] + jnp.sum(onehot, axis=0, keepdims=True)
    cnt_ref[...] = carry_ref[...]

    e1 = i1 - ROUTER_EXPERT_LANE
    e2 = i2 - ROUTER_EXPERT_LANE
    ri = jnp.where(lane == 0, e1, jnp.where(lane == 1, e2, jnp.where(lane == 2, rank1, jnp.where(lane == 3, rank2, 0.0))))
    ri_ref[...] = ri.astype(I32)
    rf_ref[...] = jnp.where(lane == 0, gtop * p1, jnp.where(lane == 1, gtop * p2, 0.0))


def _router_call(logits):
    T = logits.shape[0]
    tr = ROUTE_TR
    rows = pl.BlockSpec((tr, LANES), lambda i: (i, 0))
    return pl.pallas_call(
        _router_body,
        grid=(T // tr,),
        in_specs=[rows],
        out_specs=[rows, rows, pl.BlockSpec((1, LANES), lambda i: (0, 0))],
        out_shape=[jax.ShapeDtypeStruct((T, LANES), I32),
                   jax.ShapeDtypeStruct((T, LANES), F32),
                   jax.ShapeDtypeStruct((1, LANES), F32)],
        scratch_shapes=[pltpu.VMEM((1, LANES), F32)],
        compiler_params=_cparams(("arbitrary",)),
        name="router",
    )(logits)


def _moe_body(n_act_ref, tile_e_ref, tile_n_ref, first_ref, wslot_ref, next_e_ref, src_ref, dst_ref,
              h_hbm, wg_hbm, wu_hbm, wd_hbm, y_hbm,
              xbuf, ybuf, xs, wg_f, wu_f, wd_f, wg_b, wu_b, wd_b, gsem, ssem, wsem, *, expert0):
    k = pl.program_id(0)
    nv = tile_n_ref[k]
    n_act = n_act_ref[0]

    def slot_of(tile):
        return lax.rem(tile + MOE_SLOTS, MOE_SLOTS)

    def rows_moved(tile):
        nv_t = tile_n_ref[jnp.maximum(tile, 0)]
        nv_t = jnp.where(tile < 0, MOE_TM, nv_t)
        return pl.multiple_of(((nv_t + SUBLANES - 1) // SUBLANES) * SUBLANES, SUBLANES)

    def weight_copies(e, ws):
        return (pltpu.make_async_copy(wg_hbm.at[expert0 + e], wg_f.at[ws], wsem.at[ws]),
                pltpu.make_async_copy(wu_hbm.at[expert0 + e], wu_f.at[ws], wsem.at[ws]),
                pltpu.make_async_copy(wd_hbm.at[expert0 + e], wd_f.at[ws], wsem.at[ws]))

    def gather_row(tile, r, buf):
        return pltpu.make_async_copy(h_hbm.at[pl.ds(src_ref[tile * MOE_TM + r], 1)],
                                     xbuf.at[buf, pl.ds(r, 1)], gsem.at[buf])

    def scatter_row(tile, r, buf):
        return pltpu.make_async_copy(ybuf.at[buf, pl.ds(r, 1)],
                                     y_hbm.at[pl.ds(dst_ref[(tile + 1) * MOE_TM + r], 1)], ssem.at[buf])

    def wait_gather(tile, buf):
        n = rows_moved(tile)

        @pl.when(n > 0)
        def _():
            pltpu.make_async_copy(h_hbm.at[pl.ds(0, n)], xbuf.at[buf, pl.ds(0, n)], gsem.at[buf]).wait()

    def wait_scatter(tile, buf):
        n = rows_moved(tile)

        @pl.when(n > 0)
        def _():
            pltpu.make_async_copy(ybuf.at[buf, pl.ds(0, n)], y_hbm.at[pl.ds(0, n)], ssem.at[buf]).wait()

    def issue_loop(make_copy, tile, buf):
        def issue(r, carry):
            make_copy(tile, r, buf).start()
            return carry

        lax.fori_loop(0, rows_moved(tile), issue, 0)

    @pl.when(k == 0)
    def _():
        xbuf[...] = jnp.zeros(xbuf.shape, F32)
        ybuf[MOE_SLOTS - 1] = jnp.zeros((MOE_TM, D_MODEL), F32)
        for c in weight_copies(tile_e_ref[0], 0):
            c.start()
        issue_loop(gather_row, 0, 0)
        issue_loop(gather_row, 1, 1)

    @pl.when(nv > 0)
    def _():
        @pl.when(first_ref[k] == 1)
        def _():
            ws = wslot_ref[k]
            for c in weight_copies(tile_e_ref[k], ws):
                c.wait()
            nxt = next_e_ref[k]

            @pl.when(nxt >= 0)
            def _():
                for c in weight_copies(nxt, 1 - ws):
                    c.start(priority=1)

            wg_b[...] = wg_f[ws].astype(BF16)
            wu_b[...] = wu_f[ws].astype(BF16)
            wd_b[...] = wd_f[ws].astype(BF16)

        def tile_step(cur):
            prv = (cur + MOE_SLOTS - 1) % MOE_SLOTS
            wait_gather(k, cur)
            xs[...] = xbuf[cur].astype(BF16)
            n_in = rows_moved(k + 2)
            n_out = rows_moved(k - 1)
            for g in range(0, MOE_TM, SUBLANES):
                @pl.when(g < n_in)
                def _(g=g):
                    for r in range(g, g + SUBLANES):
                        gather_row(k + 2, r, prv).start(priority=0)
            for g in range(0, MOE_TM, SUBLANES):
                @pl.when(g < n_out)
                def _(g=g):
                    for r in range(g, g + SUBLANES):
                        scatter_row(k - 1, r, prv).start(priority=1)
            xb = xs[...]
            gate = _dot(xb, wg_b[...])
            up = _dot(xb, wu_b[...])
            mid = (gate * _sigmoid(gate) * up).astype(BF16)
            ybuf[cur] = _dot(mid, wd_b[...])

        for s in range(MOE_SLOTS):
            pl.when(slot_of(k) == s)(functools.partial(tile_step, s))

        @pl.when(k > 0)
        def _():
            wait_scatter(k - 2, slot_of(k - 2))

    @pl.when(k == n_act)
    def _():
        issue_loop(scatter_row, k - 1, slot_of(k - 1))
        wait_scatter(k - 2, slot_of(k - 2))

    @pl.when(k == n_act + 1)
    def _():
        wait_scatter(k - 2, slot_of(k - 2))


def _moe_call(tables, h, wg, wu, wd, layer, n_tiles):
    T = h.shape[0]
    any_spec = pl.BlockSpec(memory_space=pl.ANY)
    grid_spec = pltpu.PrefetchScalarGridSpec(
        num_scalar_prefetch=len(tables),
        grid=(n_tiles + MOE_DRAIN_STEPS,),
        in_specs=[any_spec, any_spec, any_spec, any_spec],
        out_specs=any_spec,
        scratch_shapes=[pltpu.VMEM((MOE_SLOTS, MOE_TM, D_MODEL), F32),
                        pltpu.VMEM((MOE_SLOTS, MOE_TM, D_MODEL), F32),
                        pltpu.VMEM((MOE_TM, D_MODEL), BF16),
                        pltpu.VMEM((2, D_MODEL, D_EXPERT), F32),
                        pltpu.VMEM((2, D_MODEL, D_EXPERT), F32),
                        pltpu.VMEM((2, D_EXPERT, D_MODEL), F32),
                        pltpu.VMEM((D_MODEL, D_EXPERT), BF16),
                        pltpu.VMEM((D_MODEL, D_EXPERT), BF16),
                        pltpu.VMEM((D_EXPERT, D_MODEL), BF16),
                        pltpu.SemaphoreType.DMA((MOE_SLOTS,)),
                        pltpu.SemaphoreType.DMA((MOE_SLOTS,)),
                        pltpu.SemaphoreType.DMA((2,))],
    )
    return pl.pallas_call(
        functools.partial(_moe_body, expert0=layer * N_EXPERTS),
        grid_spec=grid_spec,
        out_shape=jax.ShapeDtypeStruct((2 * T + MOE_TM, D_MODEL), F32),
        compiler_params=_cparams(("arbitrary",)),
        name="moe_experts",
    )(*tables, h, wg, wu, wd)


def _tables_body(e1_ref, e2_ref, r1_ref, r2_ref, cnt_ref, src0_hbm, dst0_hbm,
                 n_act_ref, tile_e_ref, tile_n_ref, first_ref, wslot_ref, next_e_ref, src_hbm, dst_hbm,
                 row0_ref, after_ref, src_ref, dst_ref, sem):
    T = e1_ref.shape[0]
    n_steps = tile_e_ref.shape[0]

    defaults = (pltpu.make_async_copy(src0_hbm, src_ref, sem.at[0]),
                pltpu.make_async_copy(dst0_hbm, dst_ref, sem.at[1]))
    for c in defaults:
        c.start()

    nxt = jnp.int32(-1)
    for e in reversed(range(N_EXPERTS)):
        after_ref[e] = nxt
        nxt = jnp.where(cnt_ref[e] > 0, jnp.int32(e), nxt)

    k = jnp.int32(0)
    order = jnp.int32(0)
    for e in range(N_EXPERTS):
        n = cnt_ref[e]
        nt = (n + (MOE_TM - 1)) // MOE_TM
        row0_ref[e] = k * MOE_TM

        def tile(i, carry, e=e, n=n, k=k, order=order):
            tile_e_ref[k + i] = e
            tile_n_ref[k + i] = jnp.minimum(n - i * MOE_TM, MOE_TM)
            first_ref[k + i] = (i == 0).astype(I32)
            wslot_ref[k + i] = order & 1
            next_e_ref[k + i] = after_ref[e]
            return carry

        lax.fori_loop(0, nt, tile, 0)
        k = k + nt
        order = order + (nt > 0).astype(I32)
    n_act_ref[0] = k

    def idle(i, carry):
        tile_e_ref[i] = 0
        tile_n_ref[i] = 0
        first_ref[i] = 0
        wslot_ref[i] = 0
        next_e_ref[i] = -1
        return carry

    lax.fori_loop(k, n_steps, idle, 0)

    for c in defaults:
        c.wait()

    def assign(t, carry):
        p1 = row0_ref[e1_ref[t]] + r1_ref[t]
        p2 = row0_ref[e2_ref[t]] + r2_ref[t]
        src_ref[p1] = t
        src_ref[p2] = t
        dst_ref[p1 + MOE_TM] = t
        dst_ref[p2 + MOE_TM] = T + t
        return carry

    lax.fori_loop(0, T, assign, 0, unroll=8)

    results = (pltpu.make_async_copy(src_ref, src_hbm, sem.at[0]),
               pltpu.make_async_copy(dst_ref, dst_hbm, sem.at[1]))
    for c in results:
        c.start()
    for c in results:
        c.wait()


def _route_tables(route_i, counts, T, n_tiles):
    n_steps = n_tiles + MOE_DRAIN_STEPS
    n_rows = n_steps * MOE_TM
    cnt = counts[0, ROUTER_EXPERT_LANE:ROUTER_EXPERT_LANE + N_EXPERTS].astype(I32)
    src0 = jnp.zeros((n_rows,), I32)
    dst0 = 2 * T + jnp.arange(n_rows, dtype=I32) % MOE_TM
    smem = pl.BlockSpec(memory_space=pltpu.SMEM)
    hbm = pl.BlockSpec(memory_space=pl.ANY)
    vec = lambda n: jax.ShapeDtypeStruct((n,), I32)
    return pl.pallas_call(
        _tables_body,
        in_specs=[smem] * 5 + [hbm] * 2,
        out_specs=[smem] * 6 + [hbm] * 2,
        out_shape=[vec(1)] + [vec(n_steps)] * 5 + [vec(n_rows)] * 2,
        scratch_shapes=[pltpu.SMEM((N_EXPERTS,), I32), pltpu.SMEM((N_EXPERTS,), I32),
                        pltpu.SMEM((n_rows,), I32), pltpu.SMEM((n_rows,), I32),
                        pltpu.SemaphoreType.DMA((2,))],
        name="route_tables",
    )(route_i[:, 0], route_i[:, 1], route_i[:, 2], route_i[:, 3], cnt, src0, dst0)


def _pad_lanes(w, offset=0):
    return jnp.pad(w, ((0, 0), (offset, LANES - offset - w.shape[1])))


def kernel(x, norm1_g, w_in, conv_w, conv_b, conv_ln_g, conv_ln_b, gla_w2, gla_b2, gla_norm_g, fox_f_b, w_out, norm2_g, router_group_w, router_group_b, router_expert_w, router_expert_b, ffn_w_gate, ffn_w_up, ffn_w_down, final_norm_g):
    B, S, D = x.shape
    T = B * S
    depth = w_in.shape[0]
    n_tiles = (2 * T) // MOE_TM + N_EXPERTS

    wg_all = ffn_w_gate.reshape(depth * N_EXPERTS, D_MODEL, D_EXPERT)
    wu_all = ffn_w_up.reshape(depth * N_EXPERTS, D_MODEL, D_EXPERT)
    wd_all = ffn_w_down.reshape(depth * N_EXPERTS, D_EXPERT, D_MODEL)
    w_proj, ws_hi, ws_lo = _prep_in_call(w_in)
    w_o = _prep_out_call(w_out)

    xt = x.reshape(T, D)
    h, small = _norm_call(xt, norm1_g[0][None, :], small_w=(ws_hi, ws_lo, 0))
    for l in range(depth):
        proj = _inproj_call(h, w_proj, l)

        y_conv = _conv_call(proj, jnp.pad(conv_w[l], ((0, CONV_PAD - CONV_WIDTH), (0, 0))), conv_b[l][None, :],
                            conv_ln_g[l][None, :], conv_ln_b[l][None, :], B, S)
        w2_hi, w2_lo = _split_bf16(jnp.pad(gla_w2[l], ((0, LANES - GLA_RANK), (0, 0))))
        y_gla = _gla_call(proj, small, w2_hi, w2_lo, gla_b2[l][None, :], gla_norm_g[l][None, :], B, S)
        fcol, frow = _fgate_call(small, _pad_lanes(fox_f_b[l][None, :], SMALL_FOX_LANE), B, S)
        y_fox = _fox_call(proj, fcol, frow, B, S)

        w_route = jnp.concatenate([router_group_w[l],
                                   router_expert_w[l].transpose(1, 0, 2).reshape(D_MODEL, N_EXPERTS)], axis=1)
        wr_hi, wr_lo = _split_bf16(_pad_lanes(w_route))
        b_route = _pad_lanes(jnp.concatenate([router_group_b[l], router_expert_b[l].reshape(-1)])[None, :])
        xt, h2, logits = _outproj_call(y_conv, y_gla, y_fox, xt, w_o, l, norm2_g[l][None, :],
                                       wr_hi, wr_lo, b_route)

        route_i, gates, counts = _router_call(logits)
        tables = _route_tables(route_i, counts, T, n_tiles)
        y2 = _moe_call(tables, h2, wg_all, wu_all, wd_all, l, n_tiles)

        if l + 1 < depth:
            xt, h, small = _norm_call(xt, norm1_g[l + 1][None, :], moe=(y2, gates), small_w=(ws_hi, ws_lo, l + 1))
        else:
            (out,) = _norm_call(xt, final_norm_g[None, :], moe=(y2, gates), out_dtype=F32)
    return out.reshape(B, S, D)
```

```python
import functools

import jax
import jax.numpy as jnp
from jax import lax
from jax.experimental import pallas as pl
from jax.experimental.pallas import tpu as pltpu

F32 = jnp.float32
BF16 = jnp.bfloat16
I32 = jnp.int32

D_MODEL = 2048
EPS = 1e-6
D_CONV = 512
CONV_WIDTH = 31
D_GLA = 1024
GLA_HEADS = 4
GLA_DK = 128
GLA_DV = 256
GLA_KEY = GLA_HEADS * GLA_DK
GLA_RANK = 16
GLA_GATE_NORMALIZER = 16.0
GLA_CHUNK = 64
D_FOX = 512
FOX_HEADS = 4
FOX_DH = 128
N_GROUPS = 4
EXPERTS_PER_GROUP = 8
N_EXPERTS = N_GROUPS * EXPERTS_PER_GROUP
D_EXPERT = 512

LANES = 128
SUBLANES = 8
D_MAIN = 2 * D_CONV + 2 * GLA_KEY + 2 * D_GLA
D_PROJ = D_MAIN + 3 * D_FOX
D_IN = D_MAIN + GLA_RANK + 3 * D_FOX + FOX_HEADS
PREP_N = 512
PREP_LAST = 24
SMALL_FOX_LANE = GLA_RANK
ROUTER_EXPERT_LANE = N_GROUPS

VMEM_LIMIT = 56 * 1024 * 1024

ROW_TILE = 512
MM_TM = 1024
MM_TN = 1408
GLA_TS = 256
FOX_TQ = 512
FOX_TK = 256
FOX_VT_BLK = 512
CONV_RC = 256
ROUTE_TR = 512
MOE_TM = 256
MOE_SLOTS = 3
MOE_DRAIN_STEPS = 2


def _cparams(sem):
    return pltpu.CompilerParams(dimension_semantics=sem, vmem_limit_bytes=VMEM_LIMIT)


def _split_bf16(x):
    hi = x.astype(BF16)
    lo = (x - hi.astype(F32)).astype(BF16)
    return hi, lo


def _dot(a, b):
    return jnp.dot(a, b, preferred_element_type=F32)


def _dot3(a, b_hi, b_lo):
    a_hi, a_lo = _split_bf16(a)
    return _dot(a_hi, b_hi) + _dot(a_lo, b_hi) + _dot(a_hi, b_lo)


def _sigmoid(x):
    return 1.0 / (1.0 + jnp.exp(-x))


def _log_sigmoid(x):
    return jnp.minimum(x, 0.0) - jnp.log(1.0 + jnp.exp(-jnp.abs(x)))


def _norm_body(*refs, combine, project):
    it = iter(refs)
    x_ref = next(it)
    if combine:
        ya_ref, yb_ref, gates_ref = next(it), next(it), next(it)
    g_ref = next(it)
    if project:
        ws_hi_ref, ws_lo_ref = next(it), next(it)
    if combine and project:
        xo_ref = next(it)
    h_ref = next(it)
    if project:
        small_ref = next(it)

    x = x_ref[...]
    if combine:
        gates = gates_ref[...]
        x = x + gates[:, 0:1] * ya_ref[...] + gates[:, 1:2] * yb_ref[...]
        if project:
            xo_ref[...] = x
    y = x * lax.rsqrt(jnp.mean(x * x, axis=-1, keepdims=True) + EPS) * g_ref[...]
    h_ref[...] = y.astype(h_ref.dtype)
    if project:
        y_hi, y_lo = _split_bf16(y)
        ws_hi = ws_hi_ref[...]
        small_ref[...] = _dot_nt(y_hi, ws_hi) + _dot_nt(y_lo, ws_hi) + _dot_nt(y_hi, ws_lo_ref[...])


def _norm_call(x, g_row, *, moe=None, small_w=None, out_dtype=BF16):
    T = x.shape[0]
    tm = ROW_TILE
    combine = moe is not None
    project = small_w is not None
    row_spec = pl.BlockSpec((tm, D_MODEL), lambda i: (i, 0))
    lane_spec = pl.BlockSpec((tm, LANES), lambda i: (i, 0))
    const = lambda shape: pl.BlockSpec(shape, lambda i: (0, 0))
    nblk = T // tm
    ins, in_specs = [x], [row_spec]
    if combine:
        y2, gates = moe
        ins += [y2, y2, gates]
        in_specs += [row_spec, pl.BlockSpec((tm, D_MODEL), lambda i: (i + nblk, 0)), lane_spec]
    ins.append(g_row)
    in_specs.append(const((1, D_MODEL)))
    if project:
        ws_hi, ws_lo, layer = small_w
        ins += [ws_hi, ws_lo]
        in_specs += [pl.BlockSpec((None, LANES, D_MODEL), lambda i: (layer, 0, 0))] * 2
    out_shape, out_specs = [], []
    if combine and project:
        out_shape.append(jax.ShapeDtypeStruct((T, D_MODEL), F32))
        out_specs.append(row_spec)
    out_shape.append(jax.ShapeDtypeStruct((T, D_MODEL), out_dtype))
    out_specs.append(row_spec)
    if project:
        out_shape.append(jax.ShapeDtypeStruct((T, LANES), F32))
        out_specs.append(lane_spec)
    return pl.pallas_call(
        functools.partial(_norm_body, combine=combine, project=project),
        grid=(nblk,),
        in_specs=in_specs,
        out_specs=out_specs,
        out_shape=out_shape,
        compiler_params=_cparams(("parallel",)),
        name="norm",
    )(*ins)


def _prep_in_body(w_hbm, wp_ref, ws_hi_ref, ws_lo_ref, buf, tail_ref, sem):
    j = pl.program_id(0)
    depth = buf.shape[1]
    n_main = D_MAIN // PREP_N
    n_last = D_IN // PREP_N
    keep = PREP_N - GLA_RANK
    slot = j % 2

    def full_block(jj, s, l):
        return pltpu.make_async_copy(w_hbm.at[pl.ds(jj * PREP_N, PREP_N), l, :], buf.at[s, l], sem.at[s])

    def last_block(s, l):
        return pltpu.make_async_copy(w_hbm.at[pl.ds(D_IN - PREP_LAST, PREP_LAST), l, :],
                                     buf.at[s, l, pl.ds(0, PREP_LAST)], sem.at[s])

    def for_block(jj, s, action):
        @pl.when(jj < n_last)
        def _():
            for l in range(depth):
                action(full_block(jj, s, l))

        @pl.when(jj == n_last)
        def _():
            for l in range(depth):
                action(last_block(s, l))

    @pl.when(j == 0)
    def _():
        for_block(j, slot, lambda c: c.start())
        ws_hi_ref[...] = jnp.zeros(ws_hi_ref.shape, BF16)
        ws_lo_ref[...] = jnp.zeros(ws_lo_ref.shape, BF16)

    @pl.when(j < n_last)
    def _():
        for_block(j + 1, 1 - slot, lambda c: c.start())

    for_block(j, slot, lambda c: c.wait())

    head0 = PREP_N * n_last - (D_IN - PREP_LAST)
    for l in range(depth):
        @pl.when(j < n_main)
        def _(l=l):
            wp_ref[l] = buf[slot, l].astype(BF16)

        @pl.when(j == n_main)
        def _(l=l):
            hi, lo = _split_bf16(buf[slot, l, 0:GLA_RANK, :])
            ws_hi_ref[l, 0:GLA_RANK, :] = hi
            ws_lo_ref[l, 0:GLA_RANK, :] = lo

        @pl.when(j > n_main)
        def _(l=l):
            wp_ref[l, 0:keep, :] = tail_ref[l, 0:keep, :]

        @pl.when((j > n_main) & (j < n_last))
        def _(l=l):
            wp_ref[l, keep:PREP_N, :] = buf[slot, l, 0:GLA_RANK, :].astype(BF16)

        @pl.when((j >= n_main) & (j < n_last))
        def _(l=l):
            tail_ref[l, 0:keep, :] = buf[slot, l, GLA_RANK:PREP_N, :].astype(BF16)

        @pl.when(j == n_last)
        def _(l=l):
            w = buf[slot, l, 0:2 * PREP_LAST, :]
            wp_ref[l, keep:PREP_N, :] = w[head0:head0 + GLA_RANK].astype(BF16)
            row = lax.broadcasted_iota(I32, (GLA_RANK, D_MODEL), 0)
            logits = jnp.where(row < FOX_HEADS, w[head0 + GLA_RANK:head0 + 2 * GLA_RANK], 0.0)
            hi, lo = _split_bf16(logits)
            ws_hi_ref[l, GLA_RANK:2 * GLA_RANK, :] = hi
            ws_lo_ref[l, GLA_RANK:2 * GLA_RANK, :] = lo


def _prep_in_call(w_in):
    depth = w_in.shape[0]
    n_main = D_MAIN // PREP_N
    w_t = jnp.transpose(w_in, (2, 0, 1))
    out_block = lambda j: (0, jnp.where(j <= n_main, jnp.minimum(j, n_main - 1), j - 1), 0)
    const = lambda j: (0, 0, 0)
    return pl.pallas_call(
        _prep_in_body,
        grid=(D_IN // PREP_N + 1,),
        in_specs=[pl.BlockSpec(memory_space=pl.ANY)],
        out_specs=[pl.BlockSpec((depth, PREP_N, D_MODEL), out_block),
                   pl.BlockSpec((depth, LANES, D_MODEL), const),
                   pl.BlockSpec((depth, LANES, D_MODEL), const)],
        out_shape=[jax.ShapeDtypeStruct((depth, D_PROJ, D_MODEL), BF16),
                   jax.ShapeDtypeStruct((depth, LANES, D_MODEL), BF16),
                   jax.ShapeDtypeStruct((depth, LANES, D_MODEL), BF16)],
        scratch_shapes=[pltpu.VMEM((2, depth, PREP_N, D_MODEL), F32),
                        pltpu.VMEM((depth, PREP_N, D_MODEL), BF16),
                        pltpu.SemaphoreType.DMA((2,))],
        compiler_params=_cparams(("arbitrary",)),
        name="weight_prep_in",
    )(w_t)


def _prep_out_body(w_ref, o_ref):
    o_ref[...] = w_ref[...].astype(BF16)


def _prep_out_call(w_out):
    depth = w_out.shape[0]
    spec = pl.BlockSpec((None, PREP_N, D_MODEL), lambda l, i: (l, i, 0))
    return pl.pallas_call(
        _prep_out_body,
        grid=(depth, D_MODEL // PREP_N),
        in_specs=[spec],
        out_specs=spec,
        out_shape=jax.ShapeDtypeStruct((depth, D_MODEL, D_MODEL), BF16),
        compiler_params=_cparams(("parallel", "parallel")),
        name="weight_prep_out",
    )(w_out)


def _dot_nt(a, b_t):
    return lax.dot_general(a, b_t, (((1,), (1,)), ((), ())), preferred_element_type=F32)


def _matmul_body(h_ref, w_ref, o_ref):
    o_ref[...] = _dot_nt(h_ref[...], w_ref[...]).astype(o_ref.dtype)


def _inproj_call(h, w_all, layer):
    T = h.shape[0]
    tm = min(MM_TM, T)
    return pl.pallas_call(
        _matmul_body,
        grid=(T // tm, D_PROJ // MM_TN),
        in_specs=[pl.BlockSpec((tm, D_MODEL), lambda i, j: (i, 0)),
                  pl.BlockSpec((None, MM_TN, D_MODEL), lambda i, j: (layer, j, 0))],
        out_specs=pl.BlockSpec((tm, MM_TN), lambda i, j: (i, j)),
        out_shape=jax.ShapeDtypeStruct((T, D_PROJ), BF16),
        compiler_params=_cparams(("parallel", "parallel")),
        name="inproj",
    )(h, w_all)


CONV_PAD = 32


def _conv_body(a_ref, g_ref, w_ref, b_ref, lng_ref, lnb_ref, o_ref, u_ref, sh_ref):
    S = a_ref.shape[0]
    u_ref[0:CONV_PAD, :] = jnp.zeros((CONV_PAD, D_CONV), F32)
    u_ref[CONV_PAD:CONV_PAD + S, :] = a_ref[...].astype(F32) * _sigmoid(g_ref[...].astype(F32))
    bias = b_ref[...]
    lng = lng_ref[...]
    lnb = lnb_ref[...]
    first = CONV_PAD - (CONV_WIDTH - 1)

    def chunk(c, carry):
        r0 = pl.multiple_of(c * CONV_RC, CONV_RC)
        acc = jnp.broadcast_to(bias, (CONV_RC, D_CONV))
        win = u_ref[pl.ds(r0, CONV_RC + CONV_PAD), :]
        for s in range(1, SUBLANES):
            sh_ref[s - 1] = win[s:s + CONV_RC + CONV_PAD - SUBLANES, :]
        for j in range(CONV_WIDTH):
            s = (first + j) % SUBLANES
            a = first + j - s
            tap = win[a:a + CONV_RC, :] if s == 0 else sh_ref[s - 1, a:a + CONV_RC, :]
            acc = acc + w_ref[j:j + 1, :] * tap
        mu = jnp.mean(acc, axis=-1, keepdims=True)
        d = acc - mu
        var = jnp.mean(d * d, axis=-1, keepdims=True)
        yn = d * lax.rsqrt(var + EPS) * lng + lnb
        o_ref[pl.ds(r0, CONV_RC), :] = (yn * _sigmoid(yn)).astype(o_ref.dtype)
        return carry

    lax.fori_loop(0, S // CONV_RC, chunk, 0)


def _conv_call(proj, w_pad, b_row, lng_row, lnb_row, B, S):
    T = B * S
    const = lambda shape: pl.BlockSpec(shape, lambda b: (0, 0))
    return pl.pallas_call(
        _conv_body,
        grid=(B,),
        in_specs=[pl.BlockSpec((S, D_CONV), lambda b: (b, 0)),
                  pl.BlockSpec((S, D_CONV), lambda b: (b, 1)),
                  const((CONV_PAD, D_CONV)), const((1, D_CONV)), const((1, D_CONV)), const((1, D_CONV))],
        out_specs=pl.BlockSpec((S, D_CONV), lambda b: (b, 0)),
        out_shape=jax.ShapeDtypeStruct((T, D_CONV), BF16),
        scratch_shapes=[pltpu.VMEM((CONV_PAD + S, D_CONV), F32),
                        pltpu.VMEM((SUBLANES - 1, CONV_RC + CONV_PAD - SUBLANES, D_CONV), F32)],
        compiler_params=_cparams(("parallel",)),
        name="conv_mixer",
    )(proj, proj, w_pad, b_row, lng_row, lnb_row)


def _gla_body(q_ref, k_ref, v_ref, g_ref, low_ref, w2hi_ref, w2lo_ref, b2_ref, ng_ref, o_ref, st_ref):
    ts = q_ref.shape[0]
    nchunk = ts // GLA_CHUNK

    @pl.when(pl.program_id(1) == 0)
    def _():
        st_ref[...] = jnp.zeros(st_ref.shape, F32)

    la = _log_sigmoid(_dot3(low_ref[...], w2hi_ref[...], w2lo_ref[...]) + b2_ref[...]) * (1.0 / GLA_GATE_NORMALIZER)
    r = lax.broadcasted_iota(I32, (2 * ts, ts), 0)
    c = lax.broadcasted_iota(I32, (2 * ts, ts), 1)
    rr = jnp.where(r >= ts, r - ts, r)
    same_chunk = (rr // GLA_CHUNK) == (c // GLA_CHUNK)
    sel = jnp.where(same_chunk & ((r >= ts) | (c <= rr)), 1.0, 0.0).astype(BF16)
    la_hi, la_lo = _split_bf16(la)
    sums = _dot(sel, la_hi) + _dot(sel, la_lo)
    cum = sums[0:ts, :]
    last = sums[ts:2 * ts, :]
    e_q = jnp.exp(cum)
    e_inv = jnp.exp(-cum)
    e_end = jnp.exp(last - cum)
    e_last = jnp.exp(last)

    qr = lax.broadcasted_iota(I32, (ts, ts), 0)
    qc = lax.broadcasted_iota(I32, (ts, ts), 1)
    att_mask = ((qr // GLA_CHUNK) == (qc // GLA_CHUNK)) & (qc <= qr)
    ng = ng_ref[...]

    for h in range(GLA_HEADS):
        ks = slice(h * GLA_DK, (h + 1) * GLA_DK)
        vs = slice(h * GLA_DV, (h + 1) * GLA_DV)
        qh = q_ref[:, ks].astype(F32) * (GLA_DK ** -0.5)
        kh = k_ref[:, ks].astype(F32)
        vh = v_ref[:, vs]
        q_dec = (qh * e_q[:, ks]).astype(BF16)
        k_inv = (kh * e_inv[:, ks]).astype(BF16)
        k_end = (kh * e_end[:, ks]).astype(BF16)
        att = lax.dot_general(q_dec, k_inv, (((1,), (1,)), ((), ())), preferred_element_type=F32)
        att = jnp.where(att_mask, att, 0.0).astype(BF16)
        o_intra = _dot(att, vh)
        state = st_ref[h]
        outs = []
        for n in range(nchunk):
            rs = slice(n * GLA_CHUNK, (n + 1) * GLA_CHUNK)
            inter = lax.dot_general(q_dec[rs], state.astype(BF16), (((1,), (1,)), ((), ())),
                                    preferred_element_type=F32)
            outs.append(o_intra[rs] + inter)
            kv_t = lax.dot_general(vh[rs], k_end[rs], (((0,), (0,)), ((), ())), preferred_element_type=F32)
            state = state * e_last[n * GLA_CHUNK:n * GLA_CHUNK + 1, ks] + kv_t
        st_ref[h] = state
        o = jnp.concatenate(outs, axis=0)
        o = o * lax.rsqrt(jnp.mean(o * o, axis=-1, keepdims=True) + EPS) * ng
        gate = g_ref[:, vs].astype(F32)
        o_ref[:, vs] = (o * (gate * _sigmoid(gate))).astype(o_ref.dtype)


def _gla_call(proj, small, w2_hi, w2_lo, b2_row, ng_row, B, S):
    T = B * S
    ts = GLA_TS
    nst = S // ts
    row = lambda b, s: b * nst + s
    const = lambda shape: pl.BlockSpec(shape, lambda b, s: (0, 0))
    return pl.pallas_call(
        _gla_body,
        grid=(B, nst),
        in_specs=[pl.BlockSpec((ts, GLA_KEY), lambda b, s: (row(b, s), 2)),
                  pl.BlockSpec((ts, GLA_KEY), lambda b, s: (row(b, s), 3)),
                  pl.BlockSpec((ts, D_GLA), lambda b, s: (row(b, s), 2)),
                  pl.BlockSpec((ts, D_GLA), lambda b, s: (row(b, s), 3)),
                  pl.BlockSpec((ts, LANES), lambda b, s: (row(b, s), 0)),
                  const((LANES, GLA_KEY)), const((LANES, GLA_KEY)), const((1, GLA_KEY)), const((1, GLA_DV))],
        out_specs=pl.BlockSpec((ts, D_GLA), lambda b, s: (row(b, s), 0)),
        out_shape=jax.ShapeDtypeStruct((T, D_GLA), BF16),
        scratch_shapes=[pltpu.VMEM((GLA_HEADS, GLA_DV, GLA_DK), F32)],
        compiler_params=_cparams(("parallel", "arbitrary")),
        name="gla_mixer",
    )(proj, proj, proj, proj, small, w2_hi, w2_lo, b2_row, ng_row)


FGATE_BLK = 256


def _fgate_body(small_ref, fb_ref, fcol_ref, frow_ref):
    S = small_ref.shape[0]
    r = lax.broadcasted_iota(I32, (FGATE_BLK, FGATE_BLK), 0)
    c = lax.broadcasted_iota(I32, (FGATE_BLK, FGATE_BLK), 1)
    tri = jnp.where(c <= r, 1.0, 0.0).astype(BF16)
    carry = jnp.zeros((1, LANES), F32)
    for n in range(S // FGATE_BLK):
        rs = slice(n * FGATE_BLK, (n + 1) * FGATE_BLK)
        lf = _log_sigmoid(small_ref[rs, :] + fb_ref[...])
        p0 = lf.astype(BF16)
        r1 = lf - p0.astype(F32)
        p1 = r1.astype(BF16)
        p2 = (r1 - p1.astype(F32)).astype(BF16)
        blk = _dot(tri, p0) + _dot(tri, p1) + _dot(tri, p2) + carry
        fcol_ref[rs, :] = blk
        carry = blk[FGATE_BLK - 1:FGATE_BLK, :]
    ft = fcol_ref[...].T
    for h in range(FOX_HEADS):
        frow_ref[0, h] = ft[SMALL_FOX_LANE + h:SMALL_FOX_LANE + h + 1, :]


def _fgate_call(small, fb_row, B, S):
    T = B * S
    return pl.pallas_call(
        _fgate_body,
        grid=(B,),
        in_specs=[pl.BlockSpec((S, LANES), lambda b: (b, 0)),
                  pl.BlockSpec((1, LANES), lambda b: (0, 0))],
        out_specs=[pl.BlockSpec((S, LANES), lambda b: (b, 0)),
                   pl.BlockSpec((1, FOX_HEADS, 1, S), lambda b: (b, 0, 0, 0))],
        out_shape=[jax.ShapeDtypeStruct((T, LANES), F32),
                   jax.ShapeDtypeStruct((B, FOX_HEADS, 1, S), F32)],
        compiler_params=_cparams(("parallel",)),
        name="fox_gate",
    )(small, fb_row)


def _fox_body(q_ref, k_ref, v_ref, fcol_ref, frow_ref, o_ref, vt_ref, fb_ref, acc_ref):
    tq = q_ref.shape[0]
    tk = FOX_TK
    S = k_ref.shape[0]
    i = pl.program_id(1)

    @pl.when(i == 0)
    def _():
        for c in range(S // FOX_VT_BLK):
            cs = slice(c * FOX_VT_BLK, (c + 1) * FOX_VT_BLK)
            vt_ref[:, cs] = v_ref[cs, :].astype(F32).T.astype(BF16)
        for h in range(FOX_HEADS):
            fb_ref[h] = jnp.broadcast_to(fcol_ref[:, SMALL_FOX_LANE + h:SMALL_FOX_LANE + h + 1], (S, LANES))

    q0 = pl.multiple_of(i * tq, tq)
    key = lax.broadcasted_iota(I32, (tk, tq), 0)
    qry = lax.broadcasted_iota(I32, (tk, tq), 1)
    n_diag = tq // tk
    n_full = i * n_diag

    heads = []
    for h in range(FOX_HEADS):
        hs = slice(h * FOX_DH, (h + 1) * FOX_DH)
        qh = (q_ref[:, hs].astype(F32) * (FOX_DH ** -0.5)).astype(BF16)
        f_t = frow_ref[0, h, :, pl.ds(q0, tq)]
        heads.append((hs, qh, f_t))

    def update(j, states, diag=None):
        k0 = pl.multiple_of(j * tk, tk)
        zs = []
        for hs, qh, f_t in heads:
            kt = k_ref[pl.ds(k0, tk), hs]
            zs.append(lax.dot_general(kt, qh, (((1,), (1,)), ((), ())), preferred_element_type=F32))
        ps, alphas, new_states = [], [], []
        for h, (hs, qh, f_t) in enumerate(heads):
            m, l = states[h]
            f_s = fb_ref[h, pl.ds(k0, tk), :]
            z = zs[h] - jnp.concatenate([f_s] * (tq // LANES), axis=1)
            if diag is not None:
                z = jnp.where(key + diag * tk <= qry, z, -jnp.inf)
            m_new = jnp.maximum(m, jnp.max(z, axis=0, keepdims=True) + f_t)
            p = jnp.exp(z + (f_t - m_new))
            alpha = jnp.exp(m - m_new)
            new_states.append((m_new, alpha * l + jnp.sum(p, axis=0, keepdims=True)))
            ps.append(p.astype(BF16))
            alphas.append(alpha)
        for h, (hs, qh, f_t) in enumerate(heads):
            pv = _dot(vt_ref[hs, pl.ds(k0, tk)], ps[h])
            acc_ref[h] = alphas[h] * acc_ref[h] + pv
        return tuple(new_states)

    acc_ref[...] = jnp.zeros(acc_ref.shape, F32)
    states = ((jnp.full((1, tq), -jnp.inf, F32), jnp.zeros((1, tq), F32)),) * FOX_HEADS
    for d in range(n_diag):
        states = update(n_full + d, states, diag=d)

    states = lax.fori_loop(0, n_full, update, states)
    for h in range(FOX_HEADS):
        m, l = states[h]
        o_ref[:, heads[h][0]] = (acc_ref[h] / l).T.astype(o_ref.dtype)


def _fox_call(proj, fcol, frow, B, S):
    T = B * S
    tq = FOX_TQ
    nq = S // tq
    col0 = D_MAIN // D_FOX
    return pl.pallas_call(
        _fox_body,
        grid=(B, nq),
        in_specs=[pl.BlockSpec((tq, D_FOX), lambda b, i: (b * nq + i, col0)),
                  pl.BlockSpec((S, D_FOX), lambda b, i: (b, col0 + 1)),
                  pl.BlockSpec((S, D_FOX), lambda b, i: (b, col0 + 2)),
                  pl.BlockSpec((S, LANES), lambda b, i: (b, 0)),
                  pl.BlockSpec((1, FOX_HEADS, 1, S), lambda b, i: (b, 0, 0, 0))],
        out_specs=pl.BlockSpec((tq, D_FOX), lambda b, i: (b * nq + i, 0)),
        out_shape=jax.ShapeDtypeStruct((T, D_FOX), BF16),
        scratch_shapes=[pltpu.VMEM((D_FOX, S), BF16),
                        pltpu.VMEM((FOX_HEADS, S, LANES), F32),
                        pltpu.VMEM((FOX_HEADS, FOX_DH, tq), F32)],
        compiler_params=_cparams(("parallel", "arbitrary")),
        name="fox_mixer",
    )(proj, proj, proj, fcol, frow)


def _outproj_body(yc_ref, yg_ref, yf_ref, x_ref, w_ref, g_ref, wr_ref, br_ref,
                  xo_ref, h_ref, lg_ref):
    acc = x_ref[...]
    acc = acc + _dot(yc_ref[...], w_ref[0:D_CONV, :])
    acc = acc + _dot(yg_ref[...], w_ref[D_CONV:D_CONV + D_GLA, :])
    acc = acc + _dot(yf_ref[...], w_ref[D_CONV + D_GLA:D_MODEL, :])
    xo_ref[...] = acc
    hn = acc * lax.rsqrt(jnp.mean(acc * acc, axis=-1, keepdims=True) + EPS) * g_ref[...]
    h_ref[...] = hn
    hn_hi, hn_lo = _split_bf16(hn)
    both = _dot(hn_hi, wr_ref[...])
    lg_ref[...] = both[:, 0:LANES] + both[:, LANES:2 * LANES] + _dot(hn_lo, wr_ref[:, 0:LANES]) + br_ref[...]


def _outproj_call(yc, yg, yf, x, w_all, layer, g_row, wr_hi, wr_lo, br_row):
    T = x.shape[0]
    tm = ROW_TILE
    const = lambda shape: pl.BlockSpec(shape, lambda i: (0, 0))
    rows = lambda width: pl.BlockSpec((tm, width), lambda i: (i, 0))
    return pl.pallas_call(
        _outproj_body,
        grid=(T // tm,),
        in_specs=[rows(D_CONV), rows(D_GLA), rows(D_FOX), rows(D_MODEL),
                  pl.BlockSpec((None, D_MODEL, D_MODEL), lambda i: (layer, 0, 0)), const((1, D_MODEL)),
                  const((D_MODEL, 2 * LANES)), const((1, LANES))],
        out_specs=[rows(D_MODEL), rows(D_MODEL), rows(LANES)],
        out_shape=[jax.ShapeDtypeStruct((T, D_MODEL), F32),
                   jax.ShapeDtypeStruct((T, D_MODEL), F32),
                   jax.ShapeDtypeStruct((T, LANES), F32)],
        compiler_params=_cparams(("parallel",)),
        name="outproj",
    )(yc, yg, yf, x, w_all, g_row, jnp.concatenate([wr_hi, wr_lo], axis=1), br_row)


def _router_body(lg_ref, ri_ref, rf_ref, cnt_ref, carry_ref):
    tr = lg_ref.shape[0]

    @pl.when(pl.program_id(0) == 0)
    def _():
        carry_ref[...] = jnp.zeros(carry_ref.shape, F32)

    lg = lg_ref[...]
    lane = lax.broadcasted_iota(I32, (tr, LANES), 1).astype(F32)
    big = float(LANES)
    neg = -jnp.inf

    is_g = lane < N_GROUPS
    gl = jnp.where(is_g, lg, neg)
    gmax = jnp.max(gl, axis=-1, keepdims=True)
    gexp = jnp.where(is_g, jnp.exp(lg - gmax), 0.0)
    gprob = gexp / jnp.sum(gexp, axis=-1, keepdims=True)
    gtop = jnp.max(gprob, axis=-1, keepdims=True)
    grp = jnp.min(jnp.where(is_g & (gprob == gtop), lane, big), axis=-1, keepdims=True)

    lo = ROUTER_EXPERT_LANE + grp * EXPERTS_PER_GROUP
    in_grp = (lane >= lo) & (lane < lo + EXPERTS_PER_GROUP)
    el = jnp.where(in_grp, lg, neg)
    v1 = jnp.max(el, axis=-1, keepdims=True)
    i1 = jnp.min(jnp.where(in_grp & (el == v1), lane, big), axis=-1, keepdims=True)
    rest = in_grp & (lane != i1)
    el2 = jnp.where(rest, lg, neg)
    v2 = jnp.max(el2, axis=-1, keepdims=True)
    i2 = jnp.min(jnp.where(rest & (el2 == v2), lane, big), axis=-1, keepdims=True)
    ex = jnp.exp(v2 - v1)
    p1 = 1.0 / (1.0 + ex)
    p2 = ex / (1.0 + ex)

    hit1 = lane == i1
    hit2 = lane == i2
    onehot = jnp.where(hit1 | hit2, 1.0, 0.0)
    r = lax.broadcasted_iota(I32, (tr, tr), 0)
    c = lax.broadcasted_iota(I32, (tr, tr), 1)
    strict = jnp.where(c < r, 1.0, 0.0).astype(BF16)
    before = _dot(strict, onehot.astype(BF16)) + carry_ref[...]
    rank1 = jnp.sum(jnp.where(hit1, before, 0.0), axis=-1, keepdims=True)
    rank2 = jnp.sum(jnp.where(hit2, before, 0.0), axis=-1, keepdims=True)
    carry_ref[...] = carry_ref[...] + jnp.sum(onehot, axis=0, keepdims=True)
    cnt_ref[...] = carry_ref[...]

    e1 = i1 - ROUTER_EXPERT_LANE
    e2 = i2 - ROUTER_EXPERT_LANE
    ri = jnp.where(lane == 0, e1, jnp.where(lane == 1, e2, jnp.where(lane == 2, rank1, jnp.where(lane == 3, rank2, 0.0))))
    ri_ref[...] = ri.astype(I32)
    rf_ref[...] = jnp.where(lane == 0, gtop * p1, jnp.where(lane == 1, gtop * p2, 0.0))


def _router_call(logits):
    T = logits.shape[0]
    tr = ROUTE_TR
    rows = pl.BlockSpec((tr, LANES), lambda i: (i, 0))
    return pl.pallas_call(
        _router_body,
        grid=(T // tr,),
        in_specs=[rows],
        out_specs=[rows, rows, pl.BlockSpec((1, LANES), lambda i: (0, 0))],
        out_shape=[jax.ShapeDtypeStruct((T, LANES), I32),
                   jax.ShapeDtypeStruct((T, LANES), F32),
                   jax.ShapeDtypeStruct((1, LANES), F32)],
        scratch_shapes=[pltpu.VMEM((1, LANES), F32)],
        compiler_params=_cparams(("arbitrary",)),
        name="router",
    )(logits)


def _moe_body(n_act_ref, tile_e_ref, tile_n_ref, first_ref, wslot_ref, next_e_ref, src_ref, dst_ref,
              h_hbm, wg_hbm, wu_hbm, wd_hbm, y_hbm,
              xbuf, ybuf, xs, wg_f, wu_f, wd_f, wg_b, wu_b, wd_b, gsem, ssem, wsem, *, expert0):
    k = pl.program_id(0)
    nv = tile_n_ref[k]
    n_act = n_act_ref[0]

    def slot_of(tile):
        return lax.rem(tile + MOE_SLOTS, MOE_SLOTS)

    def rows_moved(tile):
        nv_t = tile_n_ref[jnp.maximum(tile, 0)]
        nv_t = jnp.where(tile < 0, MOE_TM, nv_t)
        return pl.multiple_of(((nv_t + SUBLANES - 1) // SUBLANES) * SUBLANES, SUBLANES)

    def weight_copies(e, ws):
        return (pltpu.make_async_copy(wg_hbm.at[expert0 + e], wg_f.at[ws], wsem.at[ws]),
                pltpu.make_async_copy(wu_hbm.at[expert0 + e], wu_f.at[ws], wsem.at[ws]),
                pltpu.make_async_copy(wd_hbm.at[expert0 + e], wd_f.at[ws], wsem.at[ws]))

    def gather_row(tile, r, buf):
        return pltpu.make_async_copy(h_hbm.at[pl.ds(src_ref[tile * MOE_TM + r], 1)],
                                     xbuf.at[buf, pl.ds(r, 1)], gsem.at[buf])

    def scatter_row(tile, r, buf):
        return pltpu.make_async_copy(ybuf.at[buf, pl.ds(r, 1)],
                                     y_hbm.at[pl.ds(dst_ref[(tile + 1) * MOE_TM + r], 1)], ssem.at[buf])

    def wait_gather(tile, buf):
        n = rows_moved(tile)

        @pl.when(n > 0)
        def _():
            pltpu.make_async_copy(h_hbm.at[pl.ds(0, n)], xbuf.at[buf, pl.ds(0, n)], gsem.at[buf]).wait()

    def wait_scatter(tile, buf):
        n = rows_moved(tile)

        @pl.when(n > 0)
        def _():
            pltpu.make_async_copy(ybuf.at[buf, pl.ds(0, n)], y_hbm.at[pl.ds(0, n)], ssem.at[buf]).wait()

    def issue_loop(make_copy, tile, buf):
        def issue(r, carry):
            make_copy(tile, r, buf).start()
            return carry

        lax.fori_loop(0, rows_moved(tile), issue, 0)

    @pl.when(k == 0)
    def _():
        xbuf[...] = jnp.zeros(xbuf.shape, F32)
        ybuf[MOE_SLOTS - 1] = jnp.zeros((MOE_TM, D_MODEL), F32)
        for c in weight_copies(tile_e_ref[0], 0):
            c.start()
        issue_loop(gather_row, 0, 0)
        issue_loop(gather_row, 1, 1)

    @pl.when(nv > 0)
    def _():
        @pl.when(first_ref[k] == 1)
        def _():
            ws = wslot_ref[k]
            for c in weight_copies(tile_e_ref[k], ws):
                c.wait()
            nxt = next_e_ref[k]

            @pl.when(nxt >= 0)
            def _():
                for c in weight_copies(nxt, 1 - ws):
                    c.start(priority=1)

            wg_b[...] = wg_f[ws].astype(BF16)
            wu_b[...] = wu_f[ws].astype(BF16)
            wd_b[...] = wd_f[ws].astype(BF16)

        def tile_step(cur):
            prv = (cur + MOE_SLOTS - 1) % MOE_SLOTS
            wait_gather(k, cur)
            xs[...] = xbuf[cur].astype(BF16)
            n_in = rows_moved(k + 2)
            n_out = rows_moved(k - 1)
            for g in range(0, MOE_TM, SUBLANES):
                @pl.when(g < n_in)
                def _(g=g):
                    for r in range(g, g + SUBLANES):
                        gather_row(k + 2, r, prv).start(priority=0)
            for g in range(0, MOE_TM, SUBLANES):
                @pl.when(g < n_out)
                def _(g=g):
                    for r in range(g, g + SUBLANES):
                        scatter_row(k - 1, r, prv).start(priority=1)
            xb = xs[...]
            gate = _dot(xb, wg_b[...])
            up = _dot(xb, wu_b[...])
            mid = (gate * _sigmoid(gate) * up).astype(BF16)
            ybuf[cur] = _dot(mid, wd_b[...])

        for s in range(MOE_SLOTS):
            pl.when(slot_of(k) == s)(functools.partial(tile_step, s))

        @pl.when(k > 0)
        def _():
            wait_scatter(k - 2, slot_of(k - 2))

    @pl.when(k == n_act)
    def _():
        issue_loop(scatter_row, k - 1, slot_of(k - 1))
        wait_scatter(k - 2, slot_of(k - 2))

    @pl.when(k == n_act + 1)
    def _():
        wait_scatter(k - 2, slot_of(k - 2))


def _moe_call(tables, h, wg, wu, wd, layer, n_tiles):
    T = h.shape[0]
    any_spec = pl.BlockSpec(memory_space=pl.ANY)
    grid_spec = pltpu.PrefetchScalarGridSpec(
        num_scalar_prefetch=len(tables),
        grid=(n_tiles + MOE_DRAIN_STEPS,),
        in_specs=[any_spec, any_spec, any_spec, any_spec],
        out_specs=any_spec,
        scratch_shapes=[pltpu.VMEM((MOE_SLOTS, MOE_TM, D_MODEL), F32),
                        pltpu.VMEM((MOE_SLOTS, MOE_TM, D_MODEL), F32),
                        pltpu.VMEM((MOE_TM, D_MODEL), BF16),
                        pltpu.VMEM((2, D_MODEL, D_EXPERT), F32),
                        pltpu.VMEM((2, D_MODEL, D_EXPERT), F32),
                        pltpu.VMEM((2, D_EXPERT, D_MODEL), F32),
                        pltpu.VMEM((D_MODEL, D_EXPERT), BF16),
                        pltpu.VMEM((D_MODEL, D_EXPERT), BF16),
                        pltpu.VMEM((D_EXPERT, D_MODEL), BF16),
                        pltpu.SemaphoreType.DMA((MOE_SLOTS,)),
                        pltpu.SemaphoreType.DMA((MOE_SLOTS,)),
                        pltpu.SemaphoreType.DMA((2,))],
    )
    return pl.pallas_call(
        functools.partial(_moe_body, expert0=layer * N_EXPERTS),
        grid_spec=grid_spec,
        out_shape=jax.ShapeDtypeStruct((2 * T + MOE_TM, D_MODEL), F32),
        compiler_params=_cparams(("arbitrary",)),
        name="moe_experts",
    )(*tables, h, wg, wu, wd)


def _tables_body(e1_ref, e2_ref, r1_ref, r2_ref, cnt_ref, src0_hbm, dst0_hbm,
                 n_act_ref, tile_e_ref, tile_n_ref, first_ref, wslot_ref, next_e_ref, src_hbm, dst_hbm,
                 row0_ref, after_ref, src_ref, dst_ref, sem):
    T = e1_ref.shape[0]
    n_steps = tile_e_ref.shape[0]

    defaults = (pltpu.make_async_copy(src0_hbm, src_ref, sem.at[0]),
                pltpu.make_async_copy(dst0_hbm, dst_ref, sem.at[1]))
    for c in defaults:
        c.start()

    nxt = jnp.int32(-1)
    for e in reversed(range(N_EXPERTS)):
        after_ref[e] = nxt
        nxt = jnp.where(cnt_ref[e] > 0, jnp.int32(e), nxt)

    k = jnp.int32(0)
    order = jnp.int32(0)
    for e in range(N_EXPERTS):
        n = cnt_ref[e]
        nt = (n + (MOE_TM - 1)) // MOE_TM
        row0_ref[e] = k * MOE_TM

        def tile(i, carry, e=e, n=n, k=k, order=order):
            tile_e_ref[k + i] = e
            tile_n_ref[k + i] = jnp.minimum(n - i * MOE_TM, MOE_TM)
            first_ref[k + i] = (i == 0).astype(I32)
            wslot_ref[k + i] = order & 1
            next_e_ref[k + i] = after_ref[e]
            return carry

        lax.fori_loop(0, nt, tile, 0)
        k = k + nt
        order = order + (nt > 0).astype(I32)
    n_act_ref[0] = k

    def idle(i, carry):
        tile_e_ref[i] = 0
        tile_n_ref[i] = 0
        first_ref[i] = 0
        wslot_ref[i] = 0
        next_e_ref[i] = -1
        return carry

    lax.fori_loop(k, n_steps, idle, 0)

    for c in defaults:
        c.wait()

    def assign(t, carry):
        p1 = row0_ref[e1_ref[t]] + r1_ref[t]
        p2 = row0_ref[e2_ref[t]] + r2_ref[t]
        src_ref[p1] = t
        src_ref[p2] = t
        dst_ref[p1 + MOE_TM] = t
        dst_ref[p2 + MOE_TM] = T + t
        return carry

    lax.fori_loop(0, T, assign, 0, unroll=8)

    results = (pltpu.make_async_copy(src_ref, src_hbm, sem.at[0]),
               pltpu.make_async_copy(dst_ref, dst_hbm, sem.at[1]))
    for c in results:
        c.start()
    for c in results:
        c.wait()


def _route_tables(route_i, counts, T, n_tiles):
    n_steps = n_tiles + MOE_DRAIN_STEPS
    n_rows = n_steps * MOE_TM
    cnt = counts[0, ROUTER_EXPERT_LANE:ROUTER_EXPERT_LANE + N_EXPERTS].astype(I32)
    src0 = jnp.zeros((n_rows,), I32)
    dst0 = 2 * T + jnp.arange(n_rows, dtype=I32) % MOE_TM
    smem = pl.BlockSpec(memory_space=pltpu.SMEM)
    hbm = pl.BlockSpec(memory_space=pl.ANY)
    vec = lambda n: jax.ShapeDtypeStruct((n,), I32)
    return pl.pallas_call(
        _tables_body,
        in_specs=[smem] * 5 + [hbm] * 2,
        out_specs=[smem] * 6 + [hbm] * 2,
        out_shape=[vec(1)] + [vec(n_steps)] * 5 + [vec(n_rows)] * 2,
        scratch_shapes=[pltpu.SMEM((N_EXPERTS,), I32), pltpu.SMEM((N_EXPERTS,), I32),
                        pltpu.SMEM((n_rows,), I32), pltpu.SMEM((n_rows,), I32),
                        pltpu.SemaphoreType.DMA((2,))],
        name="route_tables",
    )(route_i[:, 0], route_i[:, 1], route_i[:, 2], route_i[:, 3], cnt, src0, dst0)


def _pad_lanes(w, offset=0):
    return jnp.pad(w, ((0, 0), (offset, LANES - offset - w.shape[1])))


def kernel(x, norm1_g, w_in, conv_w, conv_b, conv_ln_g, conv_ln_b, gla_w2, gla_b2, gla_norm_g, fox_f_b, w_out, norm2_g, router_group_w, router_group_b, router_expert_w, router_expert_b, ffn_w_gate, ffn_w_up, ffn_w_down, final_norm_g):
    B, S, D = x.shape
    T = B * S
    depth = w_in.shape[0]
    n_tiles = (2 * T) // MOE_TM + N_EXPERTS

    wg_all = ffn_w_gate.reshape(depth * N_EXPERTS, D_MODEL, D_EXPERT)
    wu_all = ffn_w_up.reshape(depth * N_EXPERTS, D_MODEL, D_EXPERT)
    wd_all = ffn_w_down.reshape(depth * N_EXPERTS, D_EXPERT, D_MODEL)
    w_proj, ws_hi, ws_lo = _prep_in_call(w_in)
    w_o = _prep_out_call(w_out)

    xt = x.reshape(T, D)
    h, small = _norm_call(xt, norm1_g[0][None, :], small_w=(ws_hi, ws_lo, 0))
    for l in range(depth):
        proj = _inproj_call(h, w_proj, l)

        y_conv = _conv_call(proj, jnp.pad(conv_w[l], ((0, CONV_PAD - CONV_WIDTH), (0, 0))), conv_b[l][None, :],
                            conv_ln_g[l][None, :], conv_ln_b[l][None, :], B, S)
        w2_hi, w2_lo = _split_bf16(jnp.pad(gla_w2[l], ((0, LANES - GLA_RANK), (0, 0))))
        y_gla = _gla_call(proj, small, w2_hi, w2_lo, gla_b2[l][None, :], gla_norm_g[l][None, :], B, S)
        fcol, frow = _fgate_call(small, _pad_lanes(fox_f_b[l][None, :], SMALL_FOX_LANE), B, S)
        y_fox = _fox_call(proj, fcol, frow, B, S)

        w_route = jnp.concatenate([router_group_w[l],
                                   router_expert_w[l].transpose(1, 0, 2).reshape(D_MODEL, N_EXPERTS)], axis=1)
        wr_hi, wr_lo = _split_bf16(_pad_lanes(w_route))
        b_route = _pad_lanes(jnp.concatenate([router_group_b[l], router_expert_b[l].reshape(-1)])[None, :])
        xt, h2, logits = _outproj_call(y_conv, y_gla, y_fox, xt, w_o, l, norm2_g[l][None, :],
                                       wr_hi, wr_lo, b_route)

        route_i, gates, counts = _router_call(logits)
        tables = _route_tables(route_i, counts, T, n_tiles)
        y2 = _moe_call(tables, h2, wg_all, wu_all, wd_all, l, n_tiles)

        if l + 1 < depth:
            xt, h, small = _norm_call(xt, norm1_g[l + 1][None, :], moe=(y2, gates), small_w=(ws_hi, ws_lo, l + 1))
        else:
            (out,) = _norm_call(xt, final_norm_g[None, :], moe=(y2, gates), out_dtype=F32)
    return out.reshape(B, S, D)
```

```python
import functools

import jax
import jax.numpy as jnp
from jax import lax
from jax.experimental import pallas as pl
from jax.experimental.pallas import tpu as pltpu

F32 = jnp.float32
BF16 = jnp.bfloat16
I32 = jnp.int32

D_MODEL = 2048
EPS = 1e-6
D_CONV = 512
CONV_WIDTH = 31
D_GLA = 1024
GLA_HEADS = 4
GLA_DK = 128
GLA_DV = 256
GLA_KEY = GLA_HEADS * GLA_DK
GLA_RANK = 16
GLA_GATE_NORMALIZER = 16.0
GLA_CHUNK = 64
D_FOX = 512
FOX_HEADS = 4
FOX_DH = 128
N_GROUPS = 4
EXPERTS_PER_GROUP = 8
N_EXPERTS = N_GROUPS * EXPERTS_PER_GROUP
D_EXPERT = 512

LANES = 128
SUBLANES = 8
D_MAIN = 2 * D_CONV + 2 * GLA_KEY + 2 * D_GLA
D_PROJ = D_MAIN + 3 * D_FOX
D_IN = D_MAIN + GLA_RANK + 3 * D_FOX + FOX_HEADS
PREP_N = 512
PREP_LAST = 24
SMALL_FOX_LANE = GLA_RANK
ROUTER_EXPERT_LANE = N_GROUPS

VMEM_LIMIT = 56 * 1024 * 1024

ROW_TILE = 512
MM_TM = 1024
MM_TN = 1408
GLA_TS = 256
FOX_TQ = 512
FOX_TK = 256
FOX_VT_BLK = 512
CONV_RC = 256
ROUTE_TR = 512
MOE_TM = 256
MOE_GROUP = SUBLANES
MOE_SLOTS = 3
MOE_DRAIN_STEPS = 2


def _cparams(sem):
    return pltpu.CompilerParams(dimension_semantics=sem, vmem_limit_bytes=VMEM_LIMIT)


def _split_bf16(x):
    hi = x.astype(BF16)
    lo = (x - hi.astype(F32)).astype(BF16)
    return hi, lo


def _dot(a, b):
    return jnp.dot(a, b, preferred_element_type=F32)


def _dot3(a, b_hi, b_lo):
    a_hi, a_lo = _split_bf16(a)
    return _dot(a_hi, b_hi) + _dot(a_lo, b_hi) + _dot(a_hi, b_lo)


def _sigmoid(x):
    return 1.0 / (1.0 + jnp.exp(-x))


def _log_sigmoid(x):
    return jnp.minimum(x, 0.0) - jnp.log(1.0 + jnp.exp(-jnp.abs(x)))


def _norm_body(*refs, combine, project):
    it = iter(refs)
    x_ref = next(it)
    if combine:
        ya_ref, yb_ref, gates_ref = next(it), next(it), next(it)
    g_ref = next(it)
    if project:
        ws_hi_ref, ws_lo_ref = next(it), next(it)
    if combine and project:
        xo_ref = next(it)
    h_ref = next(it)
    if project:
        small_ref = next(it)

    x = x_ref[...]
    if combine:
        gates = gates_ref[...]
        x = x + gates[:, 0:1] * ya_ref[...] + gates[:, 1:2] * yb_ref[...]
        if project:
            xo_ref[...] = x
    y = x * lax.rsqrt(jnp.mean(x * x, axis=-1, keepdims=True) + EPS) * g_ref[...]
    h_ref[...] = y.astype(h_ref.dtype)
    if project:
        y_hi, y_lo = _split_bf16(y)
        ws_hi = ws_hi_ref[...]
        small_ref[...] = _dot_nt(y_hi, ws_hi) + _dot_nt(y_lo, ws_hi) + _dot_nt(y_hi, ws_lo_ref[...])


def _norm_call(x, g_row, *, moe=None, small_w=None, out_dtype=BF16):
    T = x.shape[0]
    tm = ROW_TILE
    combine = moe is not None
    project = small_w is not None
    row_spec = pl.BlockSpec((tm, D_MODEL), lambda i: (i, 0))
    lane_spec = pl.BlockSpec((tm, LANES), lambda i: (i, 0))
    const = lambda shape: pl.BlockSpec(shape, lambda i: (0, 0))
    nblk = T // tm
    ins, in_specs = [x], [row_spec]
    if combine:
        y2, gates = moe
        ins += [y2, y2, gates]
        in_specs += [row_spec, pl.BlockSpec((tm, D_MODEL), lambda i: (i + nblk, 0)), lane_spec]
    ins.append(g_row)
    in_specs.append(const((1, D_MODEL)))
    if project:
        ws_hi, ws_lo, layer = small_w
        ins += [ws_hi, ws_lo]
        in_specs += [pl.BlockSpec((None, LANES, D_MODEL), lambda i: (layer, 0, 0))] * 2
    out_shape, out_specs = [], []
    if combine and project:
        out_shape.append(jax.ShapeDtypeStruct((T, D_MODEL), F32))
        out_specs.append(row_spec)
    out_shape.append(jax.ShapeDtypeStruct((T, D_MODEL), out_dtype))
    out_specs.append(row_spec)
    if project:
        out_shape.append(jax.ShapeDtypeStruct((T, LANES), F32))
        out_specs.append(lane_spec)
    return pl.pallas_call(
        functools.partial(_norm_body, combine=combine, project=project),
        grid=(nblk,),
        in_specs=in_specs,
        out_specs=out_specs,
        out_shape=out_shape,
        compiler_params=_cparams(("parallel",)),
        name="norm",
    )(*ins)


def _prep_in_body(w_hbm, wp_ref, ws_hi_ref, ws_lo_ref, buf, tail_ref, sem):
    j = pl.program_id(0)
    depth = buf.shape[1]
    n_main = D_MAIN // PREP_N
    n_last = D_IN // PREP_N
    keep = PREP_N - GLA_RANK
    slot = j % 2

    def full_block(jj, s, l):
        return pltpu.make_async_copy(w_hbm.at[pl.ds(jj * PREP_N, PREP_N), l, :], buf.at[s, l], sem.at[s])

    def last_block(s, l):
        return pltpu.make_async_copy(w_hbm.at[pl.ds(D_IN - PREP_LAST, PREP_LAST), l, :],
                                     buf.at[s, l, pl.ds(0, PREP_LAST)], sem.at[s])

    def for_block(jj, s, action):
        @pl.when(jj < n_last)
        def _():
            for l in range(depth):
                action(full_block(jj, s, l))

        @pl.when(jj == n_last)
        def _():
            for l in range(depth):
                action(last_block(s, l))

    @pl.when(j == 0)
    def _():
        for_block(j, slot, lambda c: c.start())
        ws_hi_ref[...] = jnp.zeros(ws_hi_ref.shape, BF16)
        ws_lo_ref[...] = jnp.zeros(ws_lo_ref.shape, BF16)

    @pl.when(j < n_last)
    def _():
        for_block(j + 1, 1 - slot, lambda c: c.start())

    for_block(j, slot, lambda c: c.wait())

    head0 = PREP_N * n_last - (D_IN - PREP_LAST)
    for l in range(depth):
        @pl.when(j < n_main)
        def _(l=l):
            wp_ref[l] = buf[slot, l].astype(BF16)

        @pl.when(j == n_main)
        def _(l=l):
            hi, lo = _split_bf16(buf[slot, l, 0:GLA_RANK, :])
            ws_hi_ref[l, 0:GLA_RANK, :] = hi
            ws_lo_ref[l, 0:GLA_RANK, :] = lo

        @pl.when(j > n_main)
        def _(l=l):
            wp_ref[l, 0:keep, :] = tail_ref[l, 0:keep, :]

        @pl.when((j > n_main) & (j < n_last))
        def _(l=l):
            wp_ref[l, keep:PREP_N, :] = buf[slot, l, 0:GLA_RANK, :].astype(BF16)

        @pl.when((j >= n_main) & (j < n_last))
        def _(l=l):
            tail_ref[l, 0:keep, :] = buf[slot, l, GLA_RANK:PREP_N, :].astype(BF16)

        @pl.when(j == n_last)
        def _(l=l):
            w = buf[slot, l, 0:2 * PREP_LAST, :]
            wp_ref[l, keep:PREP_N, :] = w[head0:head0 + GLA_RANK].astype(BF16)
            row = lax.broadcasted_iota(I32, (GLA_RANK, D_MODEL), 0)
            logits = jnp.where(row < FOX_HEADS, w[head0 + GLA_RANK:head0 + 2 * GLA_RANK], 0.0)
            hi, lo = _split_bf16(logits)
            ws_hi_ref[l, GLA_RANK:2 * GLA_RANK, :] = hi
            ws_lo_ref[l, GLA_RANK:2 * GLA_RANK, :] = lo


def _prep_in_call(w_in):
    depth = w_in.shape[0]
    n_main = D_MAIN // PREP_N
    w_t = jnp.transpose(w_in, (2, 0, 1))
    out_block = lambda j: (0, jnp.where(j <= n_main, jnp.minimum(j, n_main - 1), j - 1), 0)
    const = lambda j: (0, 0, 0)
    return pl.pallas_call(
        _prep_in_body,
        grid=(D_IN // PREP_N + 1,),
        in_specs=[pl.BlockSpec(memory_space=pl.ANY)],
        out_specs=[pl.BlockSpec((depth, PREP_N, D_MODEL), out_block),
                   pl.BlockSpec((depth, LANES, D_MODEL), const),
                   pl.BlockSpec((depth, LANES, D_MODEL), const)],
        out_shape=[jax.ShapeDtypeStruct((depth, D_PROJ, D_MODEL), BF16),
                   jax.ShapeDtypeStruct((depth, LANES, D_MODEL), BF16),
                   jax.ShapeDtypeStruct((depth, LANES, D_MODEL), BF16)],
        scratch_shapes=[pltpu.VMEM((2, depth, PREP_N, D_MODEL), F32),
                        pltpu.VMEM((depth, PREP_N, D_MODEL), BF16),
                        pltpu.SemaphoreType.DMA((2,))],
        compiler_params=_cparams(("arbitrary",)),
        name="weight_prep_in",
    )(w_t)


def _prep_out_body(w_ref, o_ref):
    o_ref[...] = w_ref[...].astype(BF16)


def _prep_out_call(w_out):
    depth = w_out.shape[0]
    spec = pl.BlockSpec((None, PREP_N, D_MODEL), lambda l, i: (l, i, 0))
    return pl.pallas_call(
        _prep_out_body,
        grid=(depth, D_MODEL // PREP_N),
        in_specs=[spec],
        out_specs=spec,
        out_shape=jax.ShapeDtypeStruct((depth, D_MODEL, D_MODEL), BF16),
        compiler_params=_cparams(("parallel", "parallel")),
        name="weight_prep_out",
    )(w_out)


def _dot_nt(a, b_t):
    return lax.dot_general(a, b_t, (((1,), (1,)), ((), ())), preferred_element_type=F32)


def _matmul_body(h_ref, w_ref, o_ref):
    o_ref[...] = _dot_nt(h_ref[...], w_ref[...]).astype(o_ref.dtype)


def _inproj_call(h, w_all, layer):
    T = h.shape[0]
    tm = min(MM_TM, T)
    return pl.pallas_call(
        _matmul_body,
        grid=(T // tm, D_PROJ // MM_TN),
        in_specs=[pl.BlockSpec((tm, D_MODEL), lambda i, j: (i, 0)),
                  pl.BlockSpec((None, MM_TN, D_MODEL), lambda i, j: (layer, j, 0))],
        out_specs=pl.BlockSpec((tm, MM_TN), lambda i, j: (i, j)),
        out_shape=jax.ShapeDtypeStruct((T, D_PROJ), BF16),
        compiler_params=_cparams(("parallel", "parallel")),
        name="inproj",
    )(h, w_all)


CONV_PAD = 32


def _conv_body(a_ref, g_ref, w_ref, b_ref, lng_ref, lnb_ref, o_ref, u_ref, sh_ref):
    S = a_ref.shape[0]
    u_ref[0:CONV_PAD, :] = jnp.zeros((CONV_PAD, D_CONV), F32)
    u_ref[CONV_PAD:CONV_PAD + S, :] = a_ref[...].astype(F32) * _sigmoid(g_ref[...].astype(F32))
    bias = b_ref[...]
    lng = lng_ref[...]
    lnb = lnb_ref[...]
    first = CONV_PAD - (CONV_WIDTH - 1)

    def chunk(c, carry):
        r0 = pl.multiple_of(c * CONV_RC, CONV_RC)
        acc = jnp.broadcast_to(bias, (CONV_RC, D_CONV))
        win = u_ref[pl.ds(r0, CONV_RC + CONV_PAD), :]
        for s in range(1, SUBLANES):
            sh_ref[s - 1] = win[s:s + CONV_RC + CONV_PAD - SUBLANES, :]
        for j in range(CONV_WIDTH):
            s = (first + j) % SUBLANES
            a = first + j - s
            tap = win[a:a + CONV_RC, :] if s == 0 else sh_ref[s - 1, a:a + CONV_RC, :]
            acc = acc + w_ref[j:j + 1, :] * tap
        mu = jnp.mean(acc, axis=-1, keepdims=True)
        d = acc - mu
        var = jnp.mean(d * d, axis=-1, keepdims=True)
        yn = d * lax.rsqrt(var + EPS) * lng + lnb
        o_ref[pl.ds(r0, CONV_RC), :] = (yn * _sigmoid(yn)).astype(o_ref.dtype)
        return carry

    lax.fori_loop(0, S // CONV_RC, chunk, 0)


def _conv_call(proj, w_pad, b_row, lng_row, lnb_row, B, S):
    T = B * S
    const = lambda shape: pl.BlockSpec(shape, lambda b: (0, 0))
    return pl.pallas_call(
        _conv_body,
        grid=(B,),
        in_specs=[pl.BlockSpec((S, D_CONV), lambda b: (b, 0)),
                  pl.BlockSpec((S, D_CONV), lambda b: (b, 1)),
                  const((CONV_PAD, D_CONV)), const((1, D_CONV)), const((1, D_CONV)), const((1, D_CONV))],
        out_specs=pl.BlockSpec((S, D_CONV), lambda b: (b, 0)),
        out_shape=jax.ShapeDtypeStruct((T, D_CONV), BF16),
        scratch_shapes=[pltpu.VMEM((CONV_PAD + S, D_CONV), F32),
                        pltpu.VMEM((SUBLANES - 1, CONV_RC + CONV_PAD - SUBLANES, D_CONV), F32)],
        compiler_params=_cparams(("parallel",)),
        name="conv_mixer",
    )(proj, proj, w_pad, b_row, lng_row, lnb_row)


def _gla_body(q_ref, k_ref, v_ref, g_ref, low_ref, w2hi_ref, w2lo_ref, b2_ref, ng_ref, o_ref, st_ref):
    ts = q_ref.shape[0]
    nchunk = ts // GLA_CHUNK

    @pl.when(pl.program_id(1) == 0)
    def _():
        st_ref[...] = jnp.zeros(st_ref.shape, F32)

    la = _log_sigmoid(_dot3(low_ref[...], w2hi_ref[...], w2lo_ref[...]) + b2_ref[...]) * (1.0 / GLA_GATE_NORMALIZER)
    r = lax.broadcasted_iota(I32, (2 * ts, ts), 0)
    c = lax.broadcasted_iota(I32, (2 * ts, ts), 1)
    rr = jnp.where(r >= ts, r - ts, r)
    same_chunk = (rr // GLA_CHUNK) == (c // GLA_CHUNK)
    sel = jnp.where(same_chunk & ((r >= ts) | (c <= rr)), 1.0, 0.0).astype(BF16)
    la_hi, la_lo = _split_bf16(la)
    sums = _dot(sel, la_hi) + _dot(sel, la_lo)
    cum = sums[0:ts, :]
    last = sums[ts:2 * ts, :]
    e_q = jnp.exp(cum)
    e_inv = jnp.exp(-cum)
    e_end = jnp.exp(last - cum)
    e_last = jnp.exp(last)

    qr = lax.broadcasted_iota(I32, (ts, ts), 0)
    qc = lax.broadcasted_iota(I32, (ts, ts), 1)
    att_mask = ((qr // GLA_CHUNK) == (qc // GLA_CHUNK)) & (qc <= qr)
    ng = ng_ref[...]

    for h in range(GLA_HEADS):
        ks = slice(h * GLA_DK, (h + 1) * GLA_DK)
        vs = slice(h * GLA_DV, (h + 1) * GLA_DV)
        qh = q_ref[:, ks].astype(F32) * (GLA_DK ** -0.5)
        kh = k_ref[:, ks].astype(F32)
        vh = v_ref[:, vs]
        q_dec = (qh * e_q[:, ks]).astype(BF16)
        k_inv = (kh * e_inv[:, ks]).astype(BF16)
        k_end = (kh * e_end[:, ks]).astype(BF16)
        att = lax.dot_general(q_dec, k_inv, (((1,), (1,)), ((), ())), preferred_element_type=F32)
        att = jnp.where(att_mask, att, 0.0).astype(BF16)
        o_intra = _dot(att, vh)
        state = st_ref[h]
        outs = []
        for n in range(nchunk):
            rs = slice(n * GLA_CHUNK, (n + 1) * GLA_CHUNK)
            inter = lax.dot_general(q_dec[rs], state.astype(BF16), (((1,), (1,)), ((), ())),
                                    preferred_element_type=F32)
            outs.append(o_intra[rs] + inter)
            kv_t = lax.dot_general(vh[rs], k_end[rs], (((0,), (0,)), ((), ())), preferred_element_type=F32)
            state = state * e_last[n * GLA_CHUNK:n * GLA_CHUNK + 1, ks] + kv_t
        st_ref[h] = state
        o = jnp.concatenate(outs, axis=0)
        o = o * lax.rsqrt(jnp.mean(o * o, axis=-1, keepdims=True) + EPS) * ng
        gate = g_ref[:, vs].astype(F32)
        o_ref[:, vs] = (o * (gate * _sigmoid(gate))).astype(o_ref.dtype)


def _gla_call(proj, small, w2_hi, w2_lo, b2_row, ng_row, B, S):
    T = B * S
    ts = GLA_TS
    nst = S // ts
    row = lambda b, s: b * nst + s
    const = lambda shape: pl.BlockSpec(shape, lambda b, s: (0, 0))
    return pl.pallas_call(
        _gla_body,
        grid=(B, nst),
        in_specs=[pl.BlockSpec((ts, GLA_KEY), lambda b, s: (row(b, s), 2)),
                  pl.BlockSpec((ts, GLA_KEY), lambda b, s: (row(b, s), 3)),
                  pl.BlockSpec((ts, D_GLA), lambda b, s: (row(b, s), 2)),
                  pl.BlockSpec((ts, D_GLA), lambda b, s: (row(b, s), 3)),
                  pl.BlockSpec((ts, LANES), lambda b, s: (row(b, s), 0)),
                  const((LANES, GLA_KEY)), const((LANES, GLA_KEY)), const((1, GLA_KEY)), const((1, GLA_DV))],
        out_specs=pl.BlockSpec((ts, D_GLA), lambda b, s: (row(b, s), 0)),
        out_shape=jax.ShapeDtypeStruct((T, D_GLA), BF16),
        scratch_shapes=[pltpu.VMEM((GLA_HEADS, GLA_DV, GLA_DK), F32)],
        compiler_params=_cparams(("parallel", "arbitrary")),
        name="gla_mixer",
    )(proj, proj, proj, proj, small, w2_hi, w2_lo, b2_row, ng_row)


FGATE_BLK = 256


def _fgate_body(small_ref, fb_ref, fcol_ref, frow_ref):
    S = small_ref.shape[0]
    r = lax.broadcasted_iota(I32, (FGATE_BLK, FGATE_BLK), 0)
    c = lax.broadcasted_iota(I32, (FGATE_BLK, FGATE_BLK), 1)
    tri = jnp.where(c <= r, 1.0, 0.0).astype(BF16)
    carry = jnp.zeros((1, LANES), F32)
    for n in range(S // FGATE_BLK):
        rs = slice(n * FGATE_BLK, (n + 1) * FGATE_BLK)
        lf = _log_sigmoid(small_ref[rs, :] + fb_ref[...])
        p0 = lf.astype(BF16)
        r1 = lf - p0.astype(F32)
        p1 = r1.astype(BF16)
        p2 = (r1 - p1.astype(F32)).astype(BF16)
        blk = _dot(tri, p0) + _dot(tri, p1) + _dot(tri, p2) + carry
        fcol_ref[rs, :] = blk
        carry = blk[FGATE_BLK - 1:FGATE_BLK, :]
    ft = fcol_ref[...].T
    for h in range(FOX_HEADS):
        frow_ref[0, h] = ft[SMALL_FOX_LANE + h:SMALL_FOX_LANE + h + 1, :]


def _fgate_call(small, fb_row, B, S):
    T = B * S
    return pl.pallas_call(
        _fgate_body,
        grid=(B,),
        in_specs=[pl.BlockSpec((S, LANES), lambda b: (b, 0)),
                  pl.BlockSpec((1, LANES), lambda b: (0, 0))],
        out_specs=[pl.BlockSpec((S, LANES), lambda b: (b, 0)),
                   pl.BlockSpec((1, FOX_HEADS, 1, S), lambda b: (b, 0, 0, 0))],
        out_shape=[jax.ShapeDtypeStruct((T, LANES), F32),
                   jax.ShapeDtypeStruct((B, FOX_HEADS, 1, S), F32)],
        compiler_params=_cparams(("parallel",)),
        name="fox_gate",
    )(small, fb_row)


def _fox_body(q_ref, k_ref, v_ref, fcol_ref, frow_ref, o_ref, vt_ref, fb_ref, acc_ref):
    tq = q_ref.shape[0]
    tk = FOX_TK
    S = k_ref.shape[0]
    i = pl.program_id(1)

    @pl.when(i == 0)
    def _():
        for c in range(S // FOX_VT_BLK):
            cs = slice(c * FOX_VT_BLK, (c + 1) * FOX_VT_BLK)
            vt_ref[:, cs] = v_ref[cs, :].astype(F32).T.astype(BF16)
        for h in range(FOX_HEADS):
            fb_ref[h] = jnp.broadcast_to(fcol_ref[:, SMALL_FOX_LANE + h:SMALL_FOX_LANE + h + 1], (S, LANES))

    q0 = pl.multiple_of(i * tq, tq)
    key = lax.broadcasted_iota(I32, (tk, tq), 0)
    qry = lax.broadcasted_iota(I32, (tk, tq), 1)
    n_diag = tq // tk
    n_full = i * n_diag

    heads = []
    for h in range(FOX_HEADS):
        hs = slice(h * FOX_DH, (h + 1) * FOX_DH)
        qh = (q_ref[:, hs].astype(F32) * (FOX_DH ** -0.5)).astype(BF16)
        f_t = frow_ref[0, h, :, pl.ds(q0, tq)]
        heads.append((hs, qh, f_t))

    def update(j, states, diag=None):
        k0 = pl.multiple_of(j * tk, tk)
        zs = []
        for hs, qh, f_t in heads:
            kt = k_ref[pl.ds(k0, tk), hs]
            zs.append(lax.dot_general(kt, qh, (((1,), (1,)), ((), ())), preferred_element_type=F32))
        ps, alphas, new_states = [], [], []
        for h, (hs, qh, f_t) in enumerate(heads):
            m, l = states[h]
            f_s = fb_ref[h, pl.ds(k0, tk), :]
            z = zs[h] - jnp.concatenate([f_s] * (tq // LANES), axis=1)
            if diag is not None:
                z = jnp.where(key + diag * tk <= qry, z, -jnp.inf)
            m_new = jnp.maximum(m, jnp.max(z, axis=0, keepdims=True) + f_t)
            p = jnp.exp(z + (f_t - m_new))
            alpha = jnp.exp(m - m_new)
            new_states.append((m_new, alpha * l + jnp.sum(p, axis=0, keepdims=True)))
            ps.append(p.astype(BF16))
            alphas.append(alpha)
        for h, (hs, qh, f_t) in enumerate(heads):
            pv = _dot(vt_ref[hs, pl.ds(k0, tk)], ps[h])
            acc_ref[h] = alphas[h] * acc_ref[h] + pv
        return tuple(new_states)

    acc_ref[...] = jnp.zeros(acc_ref.shape, F32)
    states = ((jnp.full((1, tq), -jnp.inf, F32), jnp.zeros((1, tq), F32)),) * FOX_HEADS
    for d in range(n_diag):
        states = update(n_full + d, states, diag=d)

    states = lax.fori_loop(0, n_full, update, states)
    for h in range(FOX_HEADS):
        m, l = states[h]
        o_ref[:, heads[h][0]] = (acc_ref[h] / l).T.astype(o_ref.dtype)


def _fox_call(proj, fcol, frow, B, S):
    T = B * S
    tq = FOX_TQ
    nq = S // tq
    col0 = D_MAIN // D_FOX
    return pl.pallas_call(
        _fox_body,
        grid=(B, nq),
        in_specs=[pl.BlockSpec((tq, D_FOX), lambda b, i: (b * nq + i, col0)),
                  pl.BlockSpec((S, D_FOX), lambda b, i: (b, col0 + 1)),
                  pl.BlockSpec((S, D_FOX), lambda b, i: (b, col0 + 2)),
                  pl.BlockSpec((S, LANES), lambda b, i: (b, 0)),
                  pl.BlockSpec((1, FOX_HEADS, 1, S), lambda b, i: (b, 0, 0, 0))],
        out_specs=pl.BlockSpec((tq, D_FOX), lambda b, i: (b * nq + i, 0)),
        out_shape=jax.ShapeDtypeStruct((T, D_FOX), BF16),
        scratch_shapes=[pltpu.VMEM((D_FOX, S), BF16),
                        pltpu.VMEM((FOX_HEADS, S, LANES), F32),
                        pltpu.VMEM((FOX_HEADS, FOX_DH, tq), F32)],
        compiler_params=_cparams(("parallel", "arbitrary")),
        name="fox_mixer",
    )(proj, proj, proj, fcol, frow)


def _outproj_body(yc_ref, yg_ref, yf_ref, x_ref, w_ref, g_ref, wr_ref, br_ref,
                  xo_ref, h_ref, lg_ref):
    acc = x_ref[...]
    acc = acc + _dot(yc_ref[...], w_ref[0:D_CONV, :])
    acc = acc + _dot(yg_ref[...], w_ref[D_CONV:D_CONV + D_GLA, :])
    acc = acc + _dot(yf_ref[...], w_ref[D_CONV + D_GLA:D_MODEL, :])
    xo_ref[...] = acc
    hn = acc * lax.rsqrt(jnp.mean(acc * acc, axis=-1, keepdims=True) + EPS) * g_ref[...]
    h_ref[...] = hn
    hn_hi, hn_lo = _split_bf16(hn)
    both = _dot(hn_hi, wr_ref[...])
    lg_ref[...] = both[:, 0:LANES] + both[:, LANES:2 * LANES] + _dot(hn_lo, wr_ref[:, 0:LANES]) + br_ref[...]


def _outproj_call(yc, yg, yf, x, w_all, layer, g_row, wr_hi, wr_lo, br_row):
    T = x.shape[0]
    tm = ROW_TILE
    const = lambda shape: pl.BlockSpec(shape, lambda i: (0, 0))
    rows = lambda width: pl.BlockSpec((tm, width), lambda i: (i, 0))
    return pl.pallas_call(
        _outproj_body,
        grid=(T // tm,),
        in_specs=[rows(D_CONV), rows(D_GLA), rows(D_FOX), rows(D_MODEL),
                  pl.BlockSpec((None, D_MODEL, D_MODEL), lambda i: (layer, 0, 0)), const((1, D_MODEL)),
                  const((D_MODEL, 2 * LANES)), const((1, LANES))],
        out_specs=[rows(D_MODEL), rows(D_MODEL), rows(LANES)],
        out_shape=[jax.ShapeDtypeStruct((T, D_MODEL), F32),
                   jax.ShapeDtypeStruct((T, D_MODEL), F32),
                   jax.ShapeDtypeStruct((T, LANES), F32)],
        compiler_params=_cparams(("parallel",)),
        name="outproj",
    )(yc, yg, yf, x, w_all, g_row, jnp.concatenate([wr_hi, wr_lo], axis=1), br_row)


def _router_body(lg_ref, ri_ref, rf_ref, cnt_ref, carry_ref):
    tr = lg_ref.shape[0]

    @pl.when(pl.program_id(0) == 0)
    def _():
        carry_ref[...] = jnp.zeros(carry_ref.shape, F32)

    lg = lg_ref[...]
    lane = lax.broadcasted_iota(I32, (tr, LANES), 1).astype(F32)
    big = float(LANES)
    neg = -jnp.inf

    is_g = lane < N_GROUPS
    gl = jnp.where(is_g, lg, neg)
    gmax = jnp.max(gl, axis=-1, keepdims=True)
    gexp = jnp.where(is_g, jnp.exp(lg - gmax), 0.0)
    gprob = gexp / jnp.sum(gexp, axis=-1, keepdims=True)
    gtop = jnp.max(gprob, axis=-1, keepdims=True)
    grp = jnp.min(jnp.where(is_g & (gprob == gtop), lane, big), axis=-1, keepdims=True)

    lo = ROUTER_EXPERT_LANE + grp * EXPERTS_PER_GROUP
    in_grp = (lane >= lo) & (lane < lo + EXPERTS_PER_GROUP)
    el = jnp.where(in_grp, lg, neg)
    v1 = jnp.max(el, axis=-1, keepdims=True)
    i1 = jnp.min(jnp.where(in_grp & (el == v1), lane, big), axis=-1, keepdims=True)
    rest = in_grp & (lane != i1)
    el2 = jnp.where(rest, lg, neg)
    v2 = jnp.max(el2, axis=-1, keepdims=True)
    i2 = jnp.min(jnp.where(rest & (el2 == v2), lane, big), axis=-1, keepdims=True)
    ex = jnp.exp(v2 - v1)
    p1 = 1.0 / (1.0 + ex)
    p2 = ex / (1.0 + ex)

    hit1 = lane == i1
    hit2 = lane == i2
    onehot = jnp.where(hit1 | hit2, 1.0, 0.0)
    r = lax.broadcasted_iota(I32, (tr, tr), 0)
    c = lax.broadcasted_iota(I32, (tr, tr), 1)
    strict = jnp.where(c < r, 1.0, 0.0).astype(BF16)
    before = _dot(strict, onehot.astype(BF16)) + carry_ref[...]
    rank1 = jnp.sum(jnp.where(hit1, before, 0.0), axis=-1, keepdims=True)
    rank2 = jnp.sum(jnp.where(hit2, before, 0.0), axis=-1, keepdims=True)
    carry_ref[...] = carry_ref[...] + jnp.sum(onehot, axis=0, keepdims=True)
    cnt_ref[...] = carry_ref[...]

    e1 = i1 - ROUTER_EXPERT_LANE
    e2 = i2 - ROUTER_EXPERT_LANE
    ri = jnp.where(lane == 0, e1, jnp.where(lane == 1, e2, jnp.where(lane == 2, rank1, jnp.where(lane == 3, rank2, 0.0))))
    ri_ref[...] = ri.T[0:SUBLANES, :].astype(I32)
    rf_ref[...] = jnp.where(lane == 0, gtop * p1, jnp.where(lane == 1, gtop * p2, 0.0))


def _router_call(logits):
    T = logits.shape[0]
    tr = ROUTE_TR
    rows = pl.BlockSpec((tr, LANES), lambda i: (i, 0))
    return pl.pallas_call(
        _router_body,
        grid=(T // tr,),
        in_specs=[rows],
        out_specs=[pl.BlockSpec((SUBLANES, tr), lambda i: (0, i)), rows, pl.BlockSpec((1, LANES), lambda i: (0, 0))],
        out_shape=[jax.ShapeDtypeStruct((SUBLANES, T), I32),
                   jax.ShapeDtypeStruct((T, LANES), F32),
                   jax.ShapeDtypeStruct((1, LANES), F32)],
        scratch_shapes=[pltpu.VMEM((1, LANES), F32)],
        compiler_params=_cparams(("arbitrary",)),
        name="router",
    )(logits)


def _moe_body(n_act_ref, tile_e_ref, tile_n_ref, first_ref, wslot_ref, next_e_ref, src_ref, dst_ref,
              h_hbm, wg_hbm, wu_hbm, wd_hbm, y_hbm,
              xbuf, ybuf, xs, wg_f, wu_f, wd_f, wg_b, wu_b, wd_b, gsem, ssem, wsem, *, expert0):
    k = pl.program_id(0)
    nv = tile_n_ref[k]
    n_act = n_act_ref[0]

    def slot_of(tile):
        return lax.rem(tile + MOE_SLOTS, MOE_SLOTS)

    def rows_moved(tile):
        nv_t = tile_n_ref[jnp.maximum(tile, 0)]
        nv_t = jnp.where(tile < 0, MOE_TM, nv_t)
        return pl.multiple_of(((nv_t + MOE_GROUP - 1) // MOE_GROUP) * MOE_GROUP, MOE_GROUP)

    def weight_copies(e, ws):
        return (pltpu.make_async_copy(wg_hbm.at[expert0 + e], wg_f.at[ws], wsem.at[ws]),
                pltpu.make_async_copy(wu_hbm.at[expert0 + e], wu_f.at[ws], wsem.at[ws]),
                pltpu.make_async_copy(wd_hbm.at[expert0 + e], wd_f.at[ws], wsem.at[ws]))

    def gather_row(tile, r, buf):
        return pltpu.make_async_copy(h_hbm.at[pl.ds(src_ref[tile * MOE_TM + r], 1)],
                                     xbuf.at[buf, pl.ds(r, 1)], gsem.at[buf])

    def scatter_row(tile, r, buf):
        return pltpu.make_async_copy(ybuf.at[buf, pl.ds(r, 1)],
                                     y_hbm.at[pl.ds(dst_ref[(tile + 1) * MOE_TM + r], 1)], ssem.at[buf])

    def wait_gather(tile, buf):
        n = rows_moved(tile)

        @pl.when(n > 0)
        def _():
            pltpu.make_async_copy(h_hbm.at[pl.ds(0, n)], xbuf.at[buf, pl.ds(0, n)], gsem.at[buf]).wait()

    def wait_scatter(tile, buf):
        n = rows_moved(tile)

        @pl.when(n > 0)
        def _():
            pltpu.make_async_copy(ybuf.at[buf, pl.ds(0, n)], y_hbm.at[pl.ds(0, n)], ssem.at[buf]).wait()

    def issue_loop(make_copy, tile, buf):
        def issue(r, carry):
            make_copy(tile, r, buf).start()
            return carry

        lax.fori_loop(0, rows_moved(tile), issue, 0)

    @pl.when(k == 0)
    def _():
        xbuf[...] = jnp.zeros(xbuf.shape, F32)
        ybuf[MOE_SLOTS - 1] = jnp.zeros((MOE_TM, D_MODEL), F32)
        for c in weight_copies(tile_e_ref[0], 0):
            c.start()
        issue_loop(gather_row, 0, 0)
        issue_loop(gather_row, 1, 1)

    @pl.when(nv > 0)
    def _():
        @pl.when(first_ref[k] == 1)
        def _():
            ws = wslot_ref[k]
            for c in weight_copies(tile_e_ref[k], ws):
                c.wait()
            nxt = next_e_ref[k]

            @pl.when(nxt >= 0)
            def _():
                for c in weight_copies(nxt, 1 - ws):
                    c.start(priority=1)

            wg_b[...] = wg_f[ws].astype(BF16)
            wu_b[...] = wu_f[ws].astype(BF16)
            wd_b[...] = wd_f[ws].astype(BF16)

        def tile_step(cur):
            prv = (cur + MOE_SLOTS - 1) % MOE_SLOTS
            wait_gather(k, cur)
            xs[...] = xbuf[cur].astype(BF16)
            n_in = rows_moved(k + 2)
            n_out = rows_moved(k - 1)
            for g in range(0, MOE_TM, MOE_GROUP):
                @pl.when(g < n_in)
                def _(g=g):
                    for r in range(g, g + MOE_GROUP):
                        gather_row(k + 2, r, prv).start(priority=0)
            for g in range(0, MOE_TM, MOE_GROUP):
                @pl.when(g < n_out)
                def _(g=g):
                    for r in range(g, g + MOE_GROUP):
                        scatter_row(k - 1, r, prv).start(priority=1)
            xb = xs[...]
            gate = _dot(xb, wg_b[...])
            up = _dot(xb, wu_b[...])
            mid = (gate * _sigmoid(gate) * up).astype(BF16)
            ybuf[cur] = _dot(mid, wd_b[...])

        for s in range(MOE_SLOTS):
            pl.when(slot_of(k) == s)(functools.partial(tile_step, s))

        @pl.when(k > 0)
        def _():
            wait_scatter(k - 2, slot_of(k - 2))

    @pl.when(k == n_act)
    def _():
        issue_loop(scatter_row, k - 1, slot_of(k - 1))
        wait_scatter(k - 2, slot_of(k - 2))

    @pl.when(k == n_act + 1)
    def _():
        wait_scatter(k - 2, slot_of(k - 2))


def _moe_call(tables, h, wg, wu, wd, layer, n_tiles):
    T = h.shape[0]
    any_spec = pl.BlockSpec(memory_space=pl.ANY)
    grid_spec = pltpu.PrefetchScalarGridSpec(
        num_scalar_prefetch=len(tables),
        grid=(n_tiles + MOE_DRAIN_STEPS,),
        in_specs=[any_spec, any_spec, any_spec, any_spec],
        out_specs=any_spec,
        scratch_shapes=[pltpu.VMEM((MOE_SLOTS, MOE_TM, D_MODEL), F32),
                        pltpu.VMEM((MOE_SLOTS, MOE_TM, D_MODEL), F32),
                        pltpu.VMEM((MOE_TM, D_MODEL), BF16),
                        pltpu.VMEM((2, D_MODEL, D_EXPERT), F32),
                        pltpu.VMEM((2, D_MODEL, D_EXPERT), F32),
                        pltpu.VMEM((2, D_EXPERT, D_MODEL), F32),
                        pltpu.VMEM((D_MODEL, D_EXPERT), BF16),
                        pltpu.VMEM((D_MODEL, D_EXPERT), BF16),
                        pltpu.VMEM((D_EXPERT, D_MODEL), BF16),
                        pltpu.SemaphoreType.DMA((MOE_SLOTS,)),
                        pltpu.SemaphoreType.DMA((MOE_SLOTS,)),
                        pltpu.SemaphoreType.DMA((2,))],
    )
    return pl.pallas_call(
        functools.partial(_moe_body, expert0=layer * N_EXPERTS),
        grid_spec=grid_spec,
        out_shape=jax.ShapeDtypeStruct((2 * T + MOE_TM, D_MODEL), F32),
        compiler_params=_cparams(("arbitrary",)),
        name="moe_experts",
    )(*tables, h, wg, wu, wd)


def _tables_body(e1_ref, e2_ref, r1_ref, r2_ref, cnt_ref, src0_hbm, dst0_hbm,
                 n_act_ref, tile_e_ref, tile_n_ref, first_ref, wslot_ref, next_e_ref, src_hbm, dst_hbm,
                 row0_ref, after_ref, src_ref, dst_ref, sem):
    T = e1_ref.shape[0]
    n_steps = tile_e_ref.shape[0]

    defaults = (pltpu.make_async_copy(src0_hbm, src_ref, sem.at[0]),
                pltpu.make_async_copy(dst0_hbm, dst_ref, sem.at[1]))
    for c in defaults:
        c.start()

    nxt = jnp.int32(-1)
    for e in reversed(range(N_EXPERTS)):
        after_ref[e] = nxt
        nxt = jnp.where(cnt_ref[e] > 0, jnp.int32(e), nxt)

    k = jnp.int32(0)
    order = jnp.int32(0)
    for e in range(N_EXPERTS):
        n = cnt_ref[e]
        nt = (n + (MOE_TM - 1)) // MOE_TM
        row0_ref[e] = k * MOE_TM

        def tile(i, carry, e=e, n=n, k=k, order=order):
            tile_e_ref[k + i] = e
            tile_n_ref[k + i] = jnp.minimum(n - i * MOE_TM, MOE_TM)
            first_ref[k + i] = (i == 0).astype(I32)
            wslot_ref[k + i] = order & 1
            next_e_ref[k + i] = after_ref[e]
            return carry

        lax.fori_loop(0, nt, tile, 0)
        k = k + nt
        order = order + (nt > 0).astype(I32)
    n_act_ref[0] = k

    def idle(i, carry):
        tile_e_ref[i] = 0
        tile_n_ref[i] = 0
        first_ref[i] = 0
        wslot_ref[i] = 0
        next_e_ref[i] = -1
        return carry

    lax.fori_loop(k, n_steps, idle, 0)

    for c in defaults:
        c.wait()

    def assign(t, carry):
        p1 = row0_ref[e1_ref[t]] + r1_ref[t]
        p2 = row0_ref[e2_ref[t]] + r2_ref[t]
        src_ref[p1] = t
        src_ref[p2] = t
        dst_ref[p1 + MOE_TM] = t
        dst_ref[p2 + MOE_TM] = T + t
        return carry

    lax.fori_loop(0, T, assign, 0, unroll=8)

    results = (pltpu.make_async_copy(src_ref, src_hbm, sem.at[0]),
               pltpu.make_async_copy(dst_ref, dst_hbm, sem.at[1]))
    for c in results:
        c.start()
    for c in results:
        c.wait()


def _route_tables(route_i, counts, T, n_tiles):
    n_steps = n_tiles + MOE_DRAIN_STEPS
    n_rows = n_steps * MOE_TM
    cnt = counts[0, ROUTER_EXPERT_LANE:ROUTER_EXPERT_LANE + N_EXPERTS].astype(I32)
    src0 = jnp.zeros((n_rows,), I32)
    dst0 = 2 * T + jnp.arange(n_rows, dtype=I32) % MOE_TM
    smem = pl.BlockSpec(memory_space=pltpu.SMEM)
    hbm = pl.BlockSpec(memory_space=pl.ANY)
    vec = lambda n: jax.ShapeDtypeStruct((n,), I32)
    return pl.pallas_call(
        _tables_body,
        in_specs=[smem] * 5 + [hbm] * 2,
        out_specs=[smem] * 6 + [hbm] * 2,
        out_shape=[vec(1)] + [vec(n_steps)] * 5 + [vec(n_rows)] * 2,
        scratch_shapes=[pltpu.SMEM((N_EXPERTS,), I32), pltpu.SMEM((N_EXPERTS,), I32),
                        pltpu.SMEM((n_rows,), I32), pltpu.SMEM((n_rows,), I32),
                        pltpu.SemaphoreType.DMA((2,))],
        name="route_tables",
    )(route_i[0], route_i[1], route_i[2], route_i[3], cnt, src0, dst0)


def _pad_lanes(w, offset=0):
    return jnp.pad(w, ((0, 0), (offset, LANES - offset - w.shape[1])))


def kernel(x, norm1_g, w_in, conv_w, conv_b, conv_ln_g, conv_ln_b, gla_w2, gla_b2, gla_norm_g, fox_f_b, w_out, norm2_g, router_group_w, router_group_b, router_expert_w, router_expert_b, ffn_w_gate, ffn_w_up, ffn_w_down, final_norm_g):
    B, S, D = x.shape
    T = B * S
    depth = w_in.shape[0]
    n_tiles = (2 * T) // MOE_TM + N_EXPERTS

    wg_all = ffn_w_gate.reshape(depth * N_EXPERTS, D_MODEL, D_EXPERT)
    wu_all = ffn_w_up.reshape(depth * N_EXPERTS, D_MODEL, D_EXPERT)
    wd_all = ffn_w_down.reshape(depth * N_EXPERTS, D_EXPERT, D_MODEL)
    w_proj, ws_hi, ws_lo = _prep_in_call(w_in)
    w_o = _prep_out_call(w_out)

    xt = x.reshape(T, D)
    h, small = _norm_call(xt, norm1_g[0][None, :], small_w=(ws_hi, ws_lo, 0))
    for l in range(depth):
        proj = _inproj_call(h, w_proj, l)

        y_conv = _conv_call(proj, jnp.pad(conv_w[l], ((0, CONV_PAD - CONV_WIDTH), (0, 0))), conv_b[l][None, :],
                            conv_ln_g[l][None, :], conv_ln_b[l][None, :], B, S)
        w2_hi, w2_lo = _split_bf16(jnp.pad(gla_w2[l], ((0, LANES - GLA_RANK), (0, 0))))
        y_gla = _gla_call(proj, small, w2_hi, w2_lo, gla_b2[l][None, :], gla_norm_g[l][None, :], B, S)
        fcol, frow = _fgate_call(small, _pad_lanes(fox_f_b[l][None, :], SMALL_FOX_LANE), B, S)
        y_fox = _fox_call(proj, fcol, frow, B, S)

        w_route = jnp.concatenate([router_group_w[l],
                                   router_expert_w[l].transpose(1, 0, 2).reshape(D_MODEL, N_EXPERTS)], axis=1)
        wr_hi, wr_lo = _split_bf16(_pad_lanes(w_route))
        b_route = _pad_lanes(jnp.concatenate([router_group_b[l], router_expert_b[l].reshape(-1)])[None, :])
        xt, h2, logits = _outproj_call(y_conv, y_gla, y_fox, xt, w_o, l, norm2_g[l][None, :],
                                       wr_hi, wr_lo, b_route)

        route_i, gates, counts = _router_call(logits)
        tables = _route_tables(route_i, counts, T, n_tiles)
        y2 = _moe_call(tables, h2, wg_all, wu_all, wd_all, l, n_tiles)

        if l + 1 < depth:
            xt, h, small = _norm_call(xt, norm1_g[l + 1][None, :], moe=(y2, gates), small_w=(ws_hi, ws_lo, l + 1))
        else:
            (out,) = _norm_call(xt, final_norm_g[None, :], moe=(y2, gates), out_dtype=F32)
    return out.reshape(B, S, D)
```

```python
import functools

import jax
import jax.numpy as jnp
from jax import lax
from jax.experimental import pallas as pl
from jax.experimental.pallas import tpu as pltpu

F32 = jnp.float32
BF16 = jnp.bfloat16
I32 = jnp.int32

D_MODEL = 2048
EPS = 1e-6
D_CONV = 512
CONV_WIDTH = 31
D_GLA = 1024
GLA_HEADS = 4
GLA_DK = 128
GLA_DV = 256
GLA_KEY = GLA_HEADS * GLA_DK
GLA_RANK = 16
GLA_GATE_NORMALIZER = 16.0
GLA_CHUNK = 64
D_FOX = 512
FOX_HEADS = 4
FOX_DH = 128
N_GROUPS = 4
EXPERTS_PER_GROUP = 8
N_EXPERTS = N_GROUPS * EXPERTS_PER_GROUP
D_EXPERT = 512

LANES = 128
SUBLANES = 8
D_MAIN = 2 * D_CONV + 2 * GLA_KEY + 2 * D_GLA
D_PROJ = D_MAIN + 3 * D_FOX
D_IN = D_MAIN + GLA_RANK + 3 * D_FOX + FOX_HEADS
PREP_N = 512
PREP_LAST = 24
SMALL_FOX_LANE = GLA_RANK
ROUTER_EXPERT_LANE = N_GROUPS

VMEM_LIMIT = 56 * 1024 * 1024

ROW_TILE = 512
MM_TM = 1024
MM_TN = 1408
GLA_TS = 256
FOX_TQ = 512
FOX_TK = 256
FOX_VT_BLK = 512
CONV_RC = 256
ROUTE_TR = 512
MOE_TM = 256
MOE_GROUP = SUBLANES
MOE_SLOTS = 3
MOE_DRAIN_STEPS = 2


def _cparams(sem):
    return pltpu.CompilerParams(dimension_semantics=sem, vmem_limit_bytes=VMEM_LIMIT)


def _split_bf16(x):
    hi = x.astype(BF16)
    lo = (x - hi.astype(F32)).astype(BF16)
    return hi, lo


def _dot(a, b):
    return jnp.dot(a, b, preferred_element_type=F32)


def _dot3(a, b_hi, b_lo):
    a_hi, a_lo = _split_bf16(a)
    return _dot(a_hi, b_hi) + _dot(a_lo, b_hi) + _dot(a_hi, b_lo)


def _sigmoid(x):
    return 1.0 / (1.0 + jnp.exp(-x))


def _log_sigmoid(x):
    return jnp.minimum(x, 0.0) - jnp.log(1.0 + jnp.exp(-jnp.abs(x)))


def _norm_body(*refs, combine, project):
    it = iter(refs)
    x_ref = next(it)
    if combine:
        ya_ref, yb_ref, gates_ref = next(it), next(it), next(it)
    g_ref = next(it)
    if project:
        ws_hi_ref, ws_lo_ref = next(it), next(it)
    if combine and project:
        xo_ref = next(it)
    h_ref = next(it)
    if project:
        small_ref = next(it)

    x = x_ref[...]
    if combine:
        gates = gates_ref[...]
        x = x + gates[:, 0:1] * ya_ref[...] + gates[:, 1:2] * yb_ref[...]
        if project:
            xo_ref[...] = x
    y = x * lax.rsqrt(jnp.mean(x * x, axis=-1, keepdims=True) + EPS) * g_ref[...]
    h_ref[...] = y.astype(h_ref.dtype)
    if project:
        y_hi, y_lo = _split_bf16(y)
        ws_hi = ws_hi_ref[...]
        small_ref[...] = _dot_nt(y_hi, ws_hi) + _dot_nt(y_lo, ws_hi) + _dot_nt(y_hi, ws_lo_ref[...])


def _norm_call(x, g_row, *, moe=None, small_w=None, out_dtype=BF16):
    T = x.shape[0]
    tm = ROW_TILE
    combine = moe is not None
    project = small_w is not None
    row_spec = pl.BlockSpec((tm, D_MODEL), lambda i: (i, 0))
    lane_spec = pl.BlockSpec((tm, LANES), lambda i: (i, 0))
    const = lambda shape: pl.BlockSpec(shape, lambda i: (0, 0))
    nblk = T // tm
    ins, in_specs = [x], [row_spec]
    if combine:
        y2, gates = moe
        ins += [y2, y2, gates]
        in_specs += [row_spec, pl.BlockSpec((tm, D_MODEL), lambda i: (i + nblk, 0)), lane_spec]
    ins.append(g_row)
    in_specs.append(const((1, D_MODEL)))
    if project:
        ws_hi, ws_lo, layer = small_w
        ins += [ws_hi, ws_lo]
        in_specs += [pl.BlockSpec((None, LANES, D_MODEL), lambda i: (layer, 0, 0))] * 2
    out_shape, out_specs = [], []
    if combine and project:
        out_shape.append(jax.ShapeDtypeStruct((T, D_MODEL), F32))
        out_specs.append(row_spec)
    out_shape.append(jax.ShapeDtypeStruct((T, D_MODEL), out_dtype))
    out_specs.append(row_spec)
    if project:
        out_shape.append(jax.ShapeDtypeStruct((T, LANES), F32))
        out_specs.append(lane_spec)
    return pl.pallas_call(
        functools.partial(_norm_body, combine=combine, project=project),
        grid=(nblk,),
        in_specs=in_specs,
        out_specs=out_specs,
        out_shape=out_shape,
        compiler_params=_cparams(("parallel",)),
        name="norm",
    )(*ins)


def _prep_in_body(w_hbm, wp_ref, ws_hi_ref, ws_lo_ref, buf, tail_ref, sem):
    j = pl.program_id(0)
    depth = buf.shape[1]
    n_main = D_MAIN // PREP_N
    n_last = D_IN // PREP_N
    keep = PREP_N - GLA_RANK
    slot = j % 2

    def full_block(jj, s, l):
        return pltpu.make_async_copy(w_hbm.at[pl.ds(jj * PREP_N, PREP_N), l, :], buf.at[s, l], sem.at[s])

    def last_block(s, l):
        return pltpu.make_async_copy(w_hbm.at[pl.ds(D_IN - PREP_LAST, PREP_LAST), l, :],
                                     buf.at[s, l, pl.ds(0, PREP_LAST)], sem.at[s])

    def for_block(jj, s, action):
        @pl.when(jj < n_last)
        def _():
            for l in range(depth):
                action(full_block(jj, s, l))

        @pl.when(jj == n_last)
        def _():
            for l in range(depth):
                action(last_block(s, l))

    @pl.when(j == 0)
    def _():
        for_block(j, slot, lambda c: c.start())
        ws_hi_ref[...] = jnp.zeros(ws_hi_ref.shape, BF16)
        ws_lo_ref[...] = jnp.zeros(ws_lo_ref.shape, BF16)

    @pl.when(j < n_last)
    def _():
        for_block(j + 1, 1 - slot, lambda c: c.start())

    for_block(j, slot, lambda c: c.wait())

    head0 = PREP_N * n_last - (D_IN - PREP_LAST)
    for l in range(depth):
        @pl.when(j < n_main)
        def _(l=l):
            wp_ref[l] = buf[slot, l].astype(BF16)

        @pl.when(j == n_main)
        def _(l=l):
            hi, lo = _split_bf16(buf[slot, l, 0:GLA_RANK, :])
            ws_hi_ref[l, 0:GLA_RANK, :] = hi
            ws_lo_ref[l, 0:GLA_RANK, :] = lo

        @pl.when(j > n_main)
        def _(l=l):
            wp_ref[l, 0:keep, :] = tail_ref[l, 0:keep, :]

        @pl.when((j > n_main) & (j < n_last))
        def _(l=l):
            wp_ref[l, keep:PREP_N, :] = buf[slot, l, 0:GLA_RANK, :].astype(BF16)

        @pl.when((j >= n_main) & (j < n_last))
        def _(l=l):
            tail_ref[l, 0:keep, :] = buf[slot, l, GLA_RANK:PREP_N, :].astype(BF16)

        @pl.when(j == n_last)
        def _(l=l):
            w = buf[slot, l, 0:2 * PREP_LAST, :]
            wp_ref[l, keep:PREP_N, :] = w[head0:head0 + GLA_RANK].astype(BF16)
            row = lax.broadcasted_iota(I32, (GLA_RANK, D_MODEL), 0)
            logits = jnp.where(row < FOX_HEADS, w[head0 + GLA_RANK:head0 + 2 * GLA_RANK], 0.0)
            hi, lo = _split_bf16(logits)
            ws_hi_ref[l, GLA_RANK:2 * GLA_RANK, :] = hi
            ws_lo_ref[l, GLA_RANK:2 * GLA_RANK, :] = lo


def _prep_in_call(w_in):
    depth = w_in.shape[0]
    n_main = D_MAIN // PREP_N
    w_t = jnp.transpose(w_in, (2, 0, 1))
    out_block = lambda j: (0, jnp.where(j <= n_main, jnp.minimum(j, n_main - 1), j - 1), 0)
    const = lambda j: (0, 0, 0)
    return pl.pallas_call(
        _prep_in_body,
        grid=(D_IN // PREP_N + 1,),
        in_specs=[pl.BlockSpec(memory_space=pl.ANY)],
        out_specs=[pl.BlockSpec((depth, PREP_N, D_MODEL), out_block),
                   pl.BlockSpec((depth, LANES, D_MODEL), const),
                   pl.BlockSpec((depth, LANES, D_MODEL), const)],
        out_shape=[jax.ShapeDtypeStruct((depth, D_PROJ, D_MODEL), BF16),
                   jax.ShapeDtypeStruct((depth, LANES, D_MODEL), BF16),
                   jax.ShapeDtypeStruct((depth, LANES, D_MODEL), BF16)],
        scratch_shapes=[pltpu.VMEM((2, depth, PREP_N, D_MODEL), F32),
                        pltpu.VMEM((depth, PREP_N, D_MODEL), BF16),
                        pltpu.SemaphoreType.DMA((2,))],
        compiler_params=_cparams(("arbitrary",)),
        name="weight_prep_in",
    )(w_t)


def _prep_out_body(w_ref, o_ref):
    o_ref[...] = w_ref[...].astype(BF16)


def _prep_out_call(w_out):
    depth = w_out.shape[0]
    spec = pl.BlockSpec((None, PREP_N, D_MODEL), lambda l, i: (l, i, 0))
    return pl.pallas_call(
        _prep_out_body,
        grid=(depth, D_MODEL // PREP_N),
        in_specs=[spec],
        out_specs=spec,
        out_shape=jax.ShapeDtypeStruct((depth, D_MODEL, D_MODEL), BF16),
        compiler_params=_cparams(("parallel", "parallel")),
        name="weight_prep_out",
    )(w_out)


def _dot_nt(a, b_t):
    return lax.dot_general(a, b_t, (((1,), (1,)), ((), ())), preferred_element_type=F32)


def _matmul_body(h_ref, w_ref, o_ref):
    o_ref[...] = _dot_nt(h_ref[...], w_ref[...]).astype(o_ref.dtype)


def _inproj_call(h, w_all, layer):
    T = h.shape[0]
    tm = min(MM_TM, T)
    return pl.pallas_call(
        _matmul_body,
        grid=(T // tm, D_PROJ // MM_TN),
        in_specs=[pl.BlockSpec((tm, D_MODEL), lambda i, j: (i, 0)),
                  pl.BlockSpec((None, MM_TN, D_MODEL), lambda i, j: (layer, j, 0))],
        out_specs=pl.BlockSpec((tm, MM_TN), lambda i, j: (i, j)),
        out_shape=jax.ShapeDtypeStruct((T, D_PROJ), BF16),
        compiler_params=_cparams(("parallel", "parallel")),
        name="inproj",
    )(h, w_all)


CONV_PAD = 32


def _conv_body(a_ref, g_ref, w_ref, b_ref, lng_ref, lnb_ref, o_ref, u_ref, sh_ref):
    S = a_ref.shape[0]
    u_ref[0:CONV_PAD, :] = jnp.zeros((CONV_PAD, D_CONV), F32)
    u_ref[CONV_PAD:CONV_PAD + S, :] = a_ref[...].astype(F32) * _sigmoid(g_ref[...].astype(F32))
    bias = b_ref[...]
    lng = lng_ref[...]
    lnb = lnb_ref[...]
    first = CONV_PAD - (CONV_WIDTH - 1)

    def chunk(c, carry):
        r0 = pl.multiple_of(c * CONV_RC, CONV_RC)
        acc = jnp.broadcast_to(bias, (CONV_RC, D_CONV))
        win = u_ref[pl.ds(r0, CONV_RC + CONV_PAD), :]
        for s in range(1, SUBLANES):
            sh_ref[s - 1] = win[s:s + CONV_RC + CONV_PAD - SUBLANES, :]
        for j in range(CONV_WIDTH):
            s = (first + j) % SUBLANES
            a = first + j - s
            tap = win[a:a + CONV_RC, :] if s == 0 else sh_ref[s - 1, a:a + CONV_RC, :]
            acc = acc + w_ref[j:j + 1, :] * tap
        mu = jnp.mean(acc, axis=-1, keepdims=True)
        d = acc - mu
        var = jnp.mean(d * d, axis=-1, keepdims=True)
        yn = d * lax.rsqrt(var + EPS) * lng + lnb
        o_ref[pl.ds(r0, CONV_RC), :] = (yn * _sigmoid(yn)).astype(o_ref.dtype)
        return carry

    lax.fori_loop(0, S // CONV_RC, chunk, 0)


def _conv_call(proj, w_pad, b_row, lng_row, lnb_row, B, S):
    T = B * S
    const = lambda shape: pl.BlockSpec(shape, lambda b: (0, 0))
    return pl.pallas_call(
        _conv_body,
        grid=(B,),
        in_specs=[pl.BlockSpec((S, D_CONV), lambda b: (b, 0)),
                  pl.BlockSpec((S, D_CONV), lambda b: (b, 1)),
                  const((CONV_PAD, D_CONV)), const((1, D_CONV)), const((1, D_CONV)), const((1, D_CONV))],
        out_specs=pl.BlockSpec((S, D_CONV), lambda b: (b, 0)),
        out_shape=jax.ShapeDtypeStruct((T, D_CONV), BF16),
        scratch_shapes=[pltpu.VMEM((CONV_PAD + S, D_CONV), F32),
                        pltpu.VMEM((SUBLANES - 1, CONV_RC + CONV_PAD - SUBLANES, D_CONV), F32)],
        compiler_params=_cparams(("parallel",)),
        name="conv_mixer",
    )(proj, proj, w_pad, b_row, lng_row, lnb_row)


def _gla_body(q_ref, k_ref, v_ref, g_ref, low_ref, w2hi_ref, w2lo_ref, b2_ref, ng_ref, o_ref, st_ref):
    ts = q_ref.shape[0]
    nchunk = ts // GLA_CHUNK

    @pl.when(pl.program_id(1) == 0)
    def _():
        st_ref[...] = jnp.zeros(st_ref.shape, F32)

    la = _log_sigmoid(_dot3(low_ref[...], w2hi_ref[...], w2lo_ref[...]) + b2_ref[...]) * (1.0 / GLA_GATE_NORMALIZER)
    r = lax.broadcasted_iota(I32, (2 * ts, ts), 0)
    c = lax.broadcasted_iota(I32, (2 * ts, ts), 1)
    rr = jnp.where(r >= ts, r - ts, r)
    same_chunk = (rr // GLA_CHUNK) == (c // GLA_CHUNK)
    sel = jnp.where(same_chunk & ((r >= ts) | (c <= rr)), 1.0, 0.0).astype(BF16)
    la_hi, la_lo = _split_bf16(la)
    sums = _dot(sel, la_hi) + _dot(sel, la_lo)
    cum = sums[0:ts, :]
    last = sums[ts:2 * ts, :]
    e_q = jnp.exp(cum)
    e_inv = jnp.exp(-cum)
    e_end = jnp.exp(last - cum)
    e_last = jnp.exp(last)

    qr = lax.broadcasted_iota(I32, (ts, ts), 0)
    qc = lax.broadcasted_iota(I32, (ts, ts), 1)
    att_mask = ((qr // GLA_CHUNK) == (qc // GLA_CHUNK)) & (qc <= qr)
    ng = ng_ref[...]

    for h in range(GLA_HEADS):
        ks = slice(h * GLA_DK, (h + 1) * GLA_DK)
        vs = slice(h * GLA_DV, (h + 1) * GLA_DV)
        qh = q_ref[:, ks].astype(F32) * (GLA_DK ** -0.5)
        kh = k_ref[:, ks].astype(F32)
        vh = v_ref[:, vs]
        q_dec = (qh * e_q[:, ks]).astype(BF16)
        k_inv = (kh * e_inv[:, ks]).astype(BF16)
        k_end = (kh * e_end[:, ks]).astype(BF16)
        att = lax.dot_general(q_dec, k_inv, (((1,), (1,)), ((), ())), preferred_element_type=F32)
        att = jnp.where(att_mask, att, 0.0).astype(BF16)
        o_intra = _dot(att, vh)
        state = st_ref[h]
        outs = []
        for n in range(nchunk):
            rs = slice(n * GLA_CHUNK, (n + 1) * GLA_CHUNK)
            inter = lax.dot_general(q_dec[rs], state.astype(BF16), (((1,), (1,)), ((), ())),
                                    preferred_element_type=F32)
            outs.append(o_intra[rs] + inter)
            kv_t = lax.dot_general(vh[rs], k_end[rs], (((0,), (0,)), ((), ())), preferred_element_type=F32)
            state = state * e_last[n * GLA_CHUNK:n * GLA_CHUNK + 1, ks] + kv_t
        st_ref[h] = state
        o = jnp.concatenate(outs, axis=0)
        o = o * lax.rsqrt(jnp.mean(o * o, axis=-1, keepdims=True) + EPS) * ng
        gate = g_ref[:, vs].astype(F32)
        o_ref[:, vs] = (o * (gate * _sigmoid(gate))).astype(o_ref.dtype)


def _gla_call(proj, small, w2_hi, w2_lo, b2_row, ng_row, B, S):
    T = B * S
    ts = GLA_TS
    nst = S // ts
    row = lambda b, s: b * nst + s
    const = lambda shape: pl.BlockSpec(shape, lambda b, s: (0, 0))
    return pl.pallas_call(
        _gla_body,
        grid=(B, nst),
        in_specs=[pl.BlockSpec((ts, GLA_KEY), lambda b, s: (row(b, s), 2)),
                  pl.BlockSpec((ts, GLA_KEY), lambda b, s: (row(b, s), 3)),
                  pl.BlockSpec((ts, D_GLA), lambda b, s: (row(b, s), 2)),
                  pl.BlockSpec((ts, D_GLA), lambda b, s: (row(b, s), 3)),
                  pl.BlockSpec((ts, LANES), lambda b, s: (row(b, s), 0)),
                  const((LANES, GLA_KEY)), const((LANES, GLA_KEY)), const((1, GLA_KEY)), const((1, GLA_DV))],
        out_specs=pl.BlockSpec((ts, D_GLA), lambda b, s: (row(b, s), 0)),
        out_shape=jax.ShapeDtypeStruct((T, D_GLA), BF16),
        scratch_shapes=[pltpu.VMEM((GLA_HEADS, GLA_DV, GLA_DK), F32)],
        compiler_params=_cparams(("parallel", "arbitrary")),
        name="gla_mixer",
    )(proj, proj, proj, proj, small, w2_hi, w2_lo, b2_row, ng_row)


FGATE_BLK = 256


def _fgate_body(small_ref, fb_ref, fcol_ref, frow_ref):
    S = small_ref.shape[0]
    r = lax.broadcasted_iota(I32, (FGATE_BLK, FGATE_BLK), 0)
    c = lax.broadcasted_iota(I32, (FGATE_BLK, FGATE_BLK), 1)
    tri = jnp.where(c <= r, 1.0, 0.0).astype(BF16)
    carry = jnp.zeros((1, LANES), F32)
    for n in range(S // FGATE_BLK):
        rs = slice(n * FGATE_BLK, (n + 1) * FGATE_BLK)
        lf = _log_sigmoid(small_ref[rs, :] + fb_ref[...])
        p0 = lf.astype(BF16)
        r1 = lf - p0.astype(F32)
        p1 = r1.astype(BF16)
        p2 = (r1 - p1.astype(F32)).astype(BF16)
        blk = _dot(tri, p0) + _dot(tri, p1) + _dot(tri, p2) + carry
        fcol_ref[rs, :] = blk
        carry = blk[FGATE_BLK - 1:FGATE_BLK, :]
    ft = fcol_ref[...].T
    for h in range(FOX_HEADS):
        frow_ref[0, h] = ft[SMALL_FOX_LANE + h:SMALL_FOX_LANE + h + 1, :]


def _fgate_call(small, fb_row, B, S):
    T = B * S
    return pl.pallas_call(
        _fgate_body,
        grid=(B,),
        in_specs=[pl.BlockSpec((S, LANES), lambda b: (b, 0)),
                  pl.BlockSpec((1, LANES), lambda b: (0, 0))],
        out_specs=[pl.BlockSpec((S, LANES), lambda b: (b, 0)),
                   pl.BlockSpec((1, FOX_HEADS, 1, S), lambda b: (b, 0, 0, 0))],
        out_shape=[jax.ShapeDtypeStruct((T, LANES), F32),
                   jax.ShapeDtypeStruct((B, FOX_HEADS, 1, S), F32)],
        compiler_params=_cparams(("parallel",)),
        name="fox_gate",
    )(small, fb_row)


def _fox_body(q_ref, k_ref, v_ref, fcol_ref, frow_ref, o_ref, vt_ref, fb_ref, acc_ref):
    tq = q_ref.shape[0]
    tk = FOX_TK
    S = k_ref.shape[0]
    i = pl.program_id(1)

    @pl.when(i == 0)
    def _():
        for c in range(S // FOX_VT_BLK):
            cs = slice(c * FOX_VT_BLK, (c + 1) * FOX_VT_BLK)
            vt_ref[:, cs] = v_ref[cs, :].astype(F32).T.astype(BF16)
        for h in range(FOX_HEADS):
            fb_ref[h] = jnp.broadcast_to(fcol_ref[:, SMALL_FOX_LANE + h:SMALL_FOX_LANE + h + 1], (S, LANES))

    q0 = pl.multiple_of(i * tq, tq)
    key = lax.broadcasted_iota(I32, (tk, tq), 0)
    qry = lax.broadcasted_iota(I32, (tk, tq), 1)
    n_diag = tq // tk
    n_full = i * n_diag

    heads = []
    for h in range(FOX_HEADS):
        hs = slice(h * FOX_DH, (h + 1) * FOX_DH)
        qh = (q_ref[:, hs].astype(F32) * (FOX_DH ** -0.5)).astype(BF16)
        f_t = frow_ref[0, h, :, pl.ds(q0, tq)]
        heads.append((hs, qh, f_t))

    def update(j, states, diag=None):
        k0 = pl.multiple_of(j * tk, tk)
        zs = []
        for hs, qh, f_t in heads:
            kt = k_ref[pl.ds(k0, tk), hs]
            zs.append(lax.dot_general(kt, qh, (((1,), (1,)), ((), ())), preferred_element_type=F32))
        ps, alphas, new_states = [], [], []
        for h, (hs, qh, f_t) in enumerate(heads):
            m, l = states[h]
            f_s = fb_ref[h, pl.ds(k0, tk), :]
            z = zs[h] - jnp.concatenate([f_s] * (tq // LANES), axis=1)
            if diag is not None:
                z = jnp.where(key + diag * tk <= qry, z, -jnp.inf)
            m_new = jnp.maximum(m, jnp.max(z, axis=0, keepdims=True) + f_t)
            p = jnp.exp(z + (f_t - m_new))
            alpha = jnp.exp(m - m_new)
            new_states.append((m_new, alpha * l + jnp.sum(p, axis=0, keepdims=True)))
            ps.append(p.astype(BF16))
            alphas.append(alpha)
        for h, (hs, qh, f_t) in enumerate(heads):
            pv = _dot(vt_ref[hs, pl.ds(k0, tk)], ps[h])
            acc_ref[h] = alphas[h] * acc_ref[h] + pv
        return tuple(new_states)

    acc_ref[...] = jnp.zeros(acc_ref.shape, F32)
    states = ((jnp.full((1, tq), -jnp.inf, F32), jnp.zeros((1, tq), F32)),) * FOX_HEADS
    for d in range(n_diag):
        states = update(n_full + d, states, diag=d)

    states = lax.fori_loop(0, n_full, update, states)
    for h in range(FOX_HEADS):
        m, l = states[h]
        o_ref[:, heads[h][0]] = (acc_ref[h] / l).T.astype(o_ref.dtype)


def _fox_call(proj, fcol, frow, B, S):
    T = B * S
    tq = FOX_TQ
    nq = S // tq
    col0 = D_MAIN // D_FOX
    return pl.pallas_call(
        _fox_body,
        grid=(B, nq),
        in_specs=[pl.BlockSpec((tq, D_FOX), lambda b, i: (b * nq + i, col0)),
                  pl.BlockSpec((S, D_FOX), lambda b, i: (b, col0 + 1)),
                  pl.BlockSpec((S, D_FOX), lambda b, i: (b, col0 + 2)),
                  pl.BlockSpec((S, LANES), lambda b, i: (b, 0)),
                  pl.BlockSpec((1, FOX_HEADS, 1, S), lambda b, i: (b, 0, 0, 0))],
        out_specs=pl.BlockSpec((tq, D_FOX), lambda b, i: (b * nq + i, 0)),
        out_shape=jax.ShapeDtypeStruct((T, D_FOX), BF16),
        scratch_shapes=[pltpu.VMEM((D_FOX, S), BF16),
                        pltpu.VMEM((FOX_HEADS, S, LANES), F32),
                        pltpu.VMEM((FOX_HEADS, FOX_DH, tq), F32)],
        compiler_params=_cparams(("parallel", "arbitrary")),
        name="fox_mixer",
    )(proj, proj, proj, fcol, frow)


def _outproj_body(yc_ref, yg_ref, yf_ref, x_ref, w_ref, g_ref, wr_ref, br_ref,
                  xo_ref, h_ref, lg_ref):
    acc = x_ref[...]
    acc = acc + _dot(yc_ref[...], w_ref[0:D_CONV, :])
    acc = acc + _dot(yg_ref[...], w_ref[D_CONV:D_CONV + D_GLA, :])
    acc = acc + _dot(yf_ref[...], w_ref[D_CONV + D_GLA:D_MODEL, :])
    xo_ref[...] = acc
    hn = acc * lax.rsqrt(jnp.mean(acc * acc, axis=-1, keepdims=True) + EPS) * g_ref[...]
    h_ref[...] = hn
    hn_hi, hn_lo = _split_bf16(hn)
    both = _dot(hn_hi, wr_ref[...])
    lg_ref[...] = both[:, 0:LANES] + both[:, LANES:2 * LANES] + _dot(hn_lo, wr_ref[:, 0:LANES]) + br_ref[...]


def _outproj_call(yc, yg, yf, x, w_all, layer, g_row, wr_hi, wr_lo, br_row):
    T = x.shape[0]
    tm = ROW_TILE
    const = lambda shape: pl.BlockSpec(shape, lambda i: (0, 0))
    rows = lambda width: pl.BlockSpec((tm, width), lambda i: (i, 0))
    return pl.pallas_call(
        _outproj_body,
        grid=(T // tm,),
        in_specs=[rows(D_CONV), rows(D_GLA), rows(D_FOX), rows(D_MODEL),
                  pl.BlockSpec((None, D_MODEL, D_MODEL), lambda i: (layer, 0, 0)), const((1, D_MODEL)),
                  const((D_MODEL, 2 * LANES)), const((1, LANES))],
        out_specs=[rows(D_MODEL), rows(D_MODEL), rows(LANES)],
        out_shape=[jax.ShapeDtypeStruct((T, D_MODEL), F32),
                   jax.ShapeDtypeStruct((T, D_MODEL), F32),
                   jax.ShapeDtypeStruct((T, LANES), F32)],
        compiler_params=_cparams(("parallel",)),
        name="outproj",
    )(yc, yg, yf, x, w_all, g_row, jnp.concatenate([wr_hi, wr_lo], axis=1), br_row)


def _router_body(lg_ref, ri_ref, rf_ref, cnt_ref, carry_ref):
    tr = lg_ref.shape[0]

    @pl.when(pl.program_id(0) == 0)
    def _():
        carry_ref[...] = jnp.zeros(carry_ref.shape, F32)

    lg = lg_ref[...]
    lane = lax.broadcasted_iota(I32, (tr, LANES), 1).astype(F32)
    big = float(LANES)
    neg = -jnp.inf

    is_g = lane < N_GROUPS
    gl = jnp.where(is_g, lg, neg)
    gmax = jnp.max(gl, axis=-1, keepdims=True)
    gexp = jnp.where(is_g, jnp.exp(lg - gmax), 0.0)
    gprob = gexp / jnp.sum(gexp, axis=-1, keepdims=True)
    gtop = jnp.max(gprob, axis=-1, keepdims=True)
    grp = jnp.min(jnp.where(is_g & (gprob == gtop), lane, big), axis=-1, keepdims=True)

    lo = ROUTER_EXPERT_LANE + grp * EXPERTS_PER_GROUP
    in_grp = (lane >= lo) & (lane < lo + EXPERTS_PER_GROUP)
    el = jnp.where(in_grp, lg, neg)
    v1 = jnp.max(el, axis=-1, keepdims=True)
    i1 = jnp.min(jnp.where(in_grp & (el == v1), lane, big), axis=-1, keepdims=True)
    rest = in_grp & (lane != i1)
    el2 = jnp.where(rest, lg, neg)
    v2 = jnp.max(el2, axis=-1, keepdims=True)
    i2 = jnp.min(jnp.where(rest & (el2 == v2), lane, big), axis=-1, keepdims=True)
    ex = jnp.exp(v2 - v1)
    p1 = 1.0 / (1.0 + ex)
    p2 = ex / (1.0 + ex)

    hit1 = lane == i1
    hit2 = lane == i2
    onehot = jnp.where(hit1 | hit2, 1.0, 0.0)
    r = lax.broadcasted_iota(I32, (tr, tr), 0)
    c = lax.broadcasted_iota(I32, (tr, tr), 1)
    strict = jnp.where(c < r, 1.0, 0.0).astype(BF16)
    before = _dot(strict, onehot.astype(BF16)) + carry_ref[...]
    rank1 = jnp.sum(jnp.where(hit1, before, 0.0), axis=-1, keepdims=True)
    rank2 = jnp.sum(jnp.where(hit2, before, 0.0), axis=-1, keepdims=True)
    carry_ref[...] = carry_ref[...] + jnp.sum(onehot, axis=0, keepdims=True)
    cnt_ref[...] = carry_ref[...]

    e1 = i1 - ROUTER_EXPERT_LANE
    e2 = i2 - ROUTER_EXPERT_LANE
    ri = jnp.where(lane == 0, e1, jnp.where(lane == 1, e2, jnp.where(lane == 2, rank1, jnp.where(lane == 3, rank2, 0.0))))
    ri_ref[...] = ri.T[0:SUBLANES, :].astype(I32)
    rf_ref[...] = jnp.where(lane == 0, gtop * p1, jnp.where(lane == 1, gtop * p2, 0.0))


def _router_call(logits):
    T = logits.shape[0]
    tr = ROUTE_TR
    rows = pl.BlockSpec((tr, LANES), lambda i: (i, 0))
    return pl.pallas_call(
        _router_body,
        grid=(T // tr,),
        in_specs=[rows],
        out_specs=[pl.BlockSpec((SUBLANES, tr), lambda i: (0, i)), rows, pl.BlockSpec((1, LANES), lambda i: (0, 0))],
        out_shape=[jax.ShapeDtypeStruct((SUBLANES, T), I32),
                   jax.ShapeDtypeStruct((T, LANES), F32),
                   jax.ShapeDtypeStruct((1, LANES), F32)],
        scratch_shapes=[pltpu.VMEM((1, LANES), F32)],
        compiler_params=_cparams(("arbitrary",)),
        name="router",
    )(logits)


def _moe_body(n_act_ref, tile_e_ref, tile_n_ref, first_ref, wslot_ref, next_e_ref, src_ref, dst_ref,
              h_hbm, wg_hbm, wu_hbm, wd_hbm, y_hbm,
              xbuf, ybuf, xs, wg_f, wu_f, wd_f, wg_b, wu_b, wd_b, gsem, ssem, wsem, *, expert0):
    k = pl.program_id(0)
    nv = tile_n_ref[k]
    n_act = n_act_ref[0]

    def slot_of(tile):
        return lax.rem(tile + MOE_SLOTS, MOE_SLOTS)

    def rows_moved(tile):
        nv_t = tile_n_ref[jnp.maximum(tile, 0)]
        nv_t = jnp.where(tile < 0, MOE_TM, nv_t)
        return pl.multiple_of(((nv_t + MOE_GROUP - 1) // MOE_GROUP) * MOE_GROUP, MOE_GROUP)

    def weight_copies(e, ws):
        return (pltpu.make_async_copy(wg_hbm.at[expert0 + e], wg_f.at[ws], wsem.at[ws]),
                pltpu.make_async_copy(wu_hbm.at[expert0 + e], wu_f.at[ws], wsem.at[ws]),
                pltpu.make_async_copy(wd_hbm.at[expert0 + e], wd_f.at[ws], wsem.at[ws]))

    def gather_row(tile, r, buf):
        return pltpu.make_async_copy(h_hbm.at[pl.ds(src_ref[tile * MOE_TM + r], 1)],
                                     xbuf.at[buf, pl.ds(r, 1)], gsem.at[buf])

    def scatter_row(tile, r, buf):
        return pltpu.make_async_copy(ybuf.at[buf, pl.ds(r, 1)],
                                     y_hbm.at[pl.ds(dst_ref[(tile + 1) * MOE_TM + r], 1)], ssem.at[buf])

    def wait_gather(tile, buf):
        n = rows_moved(tile)

        @pl.when(n > 0)
        def _():
            pltpu.make_async_copy(h_hbm.at[pl.ds(0, n)], xbuf.at[buf, pl.ds(0, n)], gsem.at[buf]).wait()

    def wait_scatter(tile, buf):
        n = rows_moved(tile)

        @pl.when(n > 0)
        def _():
            pltpu.make_async_copy(ybuf.at[buf, pl.ds(0, n)], y_hbm.at[pl.ds(0, n)], ssem.at[buf]).wait()

    def issue_loop(make_copy, tile, buf):
        def issue(r, carry):
            make_copy(tile, r, buf).start()
            return carry

        lax.fori_loop(0, rows_moved(tile), issue, 0)

    @pl.when(k == 0)
    def _():
        xbuf[...] = jnp.zeros(xbuf.shape, F32)
        ybuf[MOE_SLOTS - 1] = jnp.zeros((MOE_TM, D_MODEL), F32)
        for c in weight_copies(tile_e_ref[0], 0):
            c.start()
        issue_loop(gather_row, 0, 0)
        issue_loop(gather_row, 1, 1)

    @pl.when(nv > 0)
    def _():
        @pl.when(first_ref[k] == 1)
        def _():
            ws = wslot_ref[k]
            for c in weight_copies(tile_e_ref[k], ws):
                c.wait()
            nxt = next_e_ref[k]

            @pl.when(nxt >= 0)
            def _():
                for c in weight_copies(nxt, 1 - ws):
                    c.start(priority=1)

            wg_b[...] = wg_f[ws].astype(BF16)
            wu_b[...] = wu_f[ws].astype(BF16)
            wd_b[...] = wd_f[ws].astype(BF16)

        def tile_step(cur):
            prv = (cur + MOE_SLOTS - 1) % MOE_SLOTS
            wait_gather(k, cur)
            xs[...] = xbuf[cur].astype(BF16)
            n_in = rows_moved(k + 2)
            n_out = rows_moved(k - 1)
            for g in range(0, MOE_TM, MOE_GROUP):
                toks = [src_ref[(k + 2) * MOE_TM + r] for r in range(g, g + MOE_GROUP)]

                @pl.when(g < n_in)
                def _(g=g, toks=toks):
                    for r, tok in zip(range(g, g + MOE_GROUP), toks):
                        pltpu.make_async_copy(h_hbm.at[pl.ds(tok, 1)], xbuf.at[prv, pl.ds(r, 1)],
                                              gsem.at[prv]).start(priority=0)
            for g in range(0, MOE_TM, MOE_GROUP):
                outs = [dst_ref[k * MOE_TM + r] for r in range(g, g + MOE_GROUP)]

                @pl.when(g < n_out)
                def _(g=g, outs=outs):
                    for r, out in zip(range(g, g + MOE_GROUP), outs):
                        pltpu.make_async_copy(ybuf.at[prv, pl.ds(r, 1)], y_hbm.at[pl.ds(out, 1)],
                                              ssem.at[prv]).start(priority=1)
            xb = xs[...]
            gate = _dot(xb, wg_b[...])
            up = _dot(xb, wu_b[...])
            mid = (gate * _sigmoid(gate) * up).astype(BF16)
            ybuf[cur] = _dot(mid, wd_b[...])

        for s in range(MOE_SLOTS):
            pl.when(slot_of(k) == s)(functools.partial(tile_step, s))

        @pl.when(k > 0)
        def _():
            wait_scatter(k - 2, slot_of(k - 2))

    @pl.when(k == n_act)
    def _():
        issue_loop(scatter_row, k - 1, slot_of(k - 1))
        wait_scatter(k - 2, slot_of(k - 2))

    @pl.when(k == n_act + 1)
    def _():
        wait_scatter(k - 2, slot_of(k - 2))


def _moe_call(tables, h, wg, wu, wd, layer, n_tiles):
    T = h.shape[0]
    any_spec = pl.BlockSpec(memory_space=pl.ANY)
    grid_spec = pltpu.PrefetchScalarGridSpec(
        num_scalar_prefetch=len(tables),
        grid=(n_tiles + MOE_DRAIN_STEPS,),
        in_specs=[any_spec, any_spec, any_spec, any_spec],
        out_specs=any_spec,
        scratch_shapes=[pltpu.VMEM((MOE_SLOTS, MOE_TM, D_MODEL), F32),
                        pltpu.VMEM((MOE_SLOTS, MOE_TM, D_MODEL), F32),
                        pltpu.VMEM((MOE_TM, D_MODEL), BF16),
                        pltpu.VMEM((2, D_MODEL, D_EXPERT), F32),
                        pltpu.VMEM((2, D_MODEL, D_EXPERT), F32),
                        pltpu.VMEM((2, D_EXPERT, D_MODEL), F32),
                        pltpu.VMEM((D_MODEL, D_EXPERT), BF16),
                        pltpu.VMEM((D_MODEL, D_EXPERT), BF16),
                        pltpu.VMEM((D_EXPERT, D_MODEL), BF16),
                        pltpu.SemaphoreType.DMA((MOE_SLOTS,)),
                        pltpu.SemaphoreType.DMA((MOE_SLOTS,)),
                        pltpu.SemaphoreType.DMA((2,))],
    )
    return pl.pallas_call(
        functools.partial(_moe_body, expert0=layer * N_EXPERTS),
        grid_spec=grid_spec,
        out_shape=jax.ShapeDtypeStruct((2 * T + MOE_TM, D_MODEL), F32),
        compiler_params=_cparams(("arbitrary",)),
        name="moe_experts",
    )(*tables, h, wg, wu, wd)


def _tables_body(e1_ref, e2_ref, r1_ref, r2_ref, cnt_ref, src0_hbm, dst0_hbm,
                 n_act_ref, tile_e_ref, tile_n_ref, first_ref, wslot_ref, next_e_ref, src_hbm, dst_hbm,
                 row0_ref, after_ref, src_ref, dst_ref, sem):
    T = e1_ref.shape[0]
    n_steps = tile_e_ref.shape[0]

    defaults = (pltpu.make_async_copy(src0_hbm, src_ref, sem.at[0]),
                pltpu.make_async_copy(dst0_hbm, dst_ref, sem.at[1]))
    for c in defaults:
        c.start()

    nxt = jnp.int32(-1)
    for e in reversed(range(N_EXPERTS)):
        after_ref[e] = nxt
        nxt = jnp.where(cnt_ref[e] > 0, jnp.int32(e), nxt)

    k = jnp.int32(0)
    order = jnp.int32(0)
    for e in range(N_EXPERTS):
        n = cnt_ref[e]
        nt = (n + (MOE_TM - 1)) // MOE_TM
        row0_ref[e] = k * MOE_TM

        def tile(i, carry, e=e, n=n, k=k, order=order):
            tile_e_ref[k + i] = e
            tile_n_ref[k + i] = jnp.minimum(n - i * MOE_TM, MOE_TM)
            first_ref[k + i] = (i == 0).astype(I32)
            wslot_ref[k + i] = order & 1
            next_e_ref[k + i] = after_ref[e]
            return carry

        lax.fori_loop(0, nt, tile, 0)
        k = k + nt
        order = order + (nt > 0).astype(I32)
    n_act_ref[0] = k

    def idle(i, carry):
        tile_e_ref[i] = 0
        tile_n_ref[i] = 0
        first_ref[i] = 0
        wslot_ref[i] = 0
        next_e_ref[i] = -1
        return carry

    lax.fori_loop(k, n_steps, idle, 0)

    for c in defaults:
        c.wait()

    def assign(t, carry):
        p1 = row0_ref[e1_ref[t]] + r1_ref[t]
        p2 = row0_ref[e2_ref[t]] + r2_ref[t]
        src_ref[p1] = t
        src_ref[p2] = t
        dst_ref[p1 + MOE_TM] = t
        dst_ref[p2 + MOE_TM] = T + t
        return carry

    lax.fori_loop(0, T, assign, 0, unroll=8)

    results = (pltpu.make_async_copy(src_ref, src_hbm, sem.at[0]),
               pltpu.make_async_copy(dst_ref, dst_hbm, sem.at[1]))
    for c in results:
        c.start()
    for c in results:
        c.wait()


def _route_tables(route_i, counts, T, n_tiles):
    n_steps = n_tiles + MOE_DRAIN_STEPS
    n_rows = n_steps * MOE_TM
    cnt = counts[0, ROUTER_EXPERT_LANE:ROUTER_EXPERT_LANE + N_EXPERTS].astype(I32)
    src0 = jnp.zeros((n_rows,), I32)
    dst0 = 2 * T + jnp.arange(n_rows, dtype=I32) % MOE_TM
    smem = pl.BlockSpec(memory_space=pltpu.SMEM)
    hbm = pl.BlockSpec(memory_space=pl.ANY)
    vec = lambda n: jax.ShapeDtypeStruct((n,), I32)
    return pl.pallas_call(
        _tables_body,
        in_specs=[smem] * 5 + [hbm] * 2,
        out_specs=[smem] * 6 + [hbm] * 2,
        out_shape=[vec(1)] + [vec(n_steps)] * 5 + [vec(n_rows)] * 2,
        scratch_shapes=[pltpu.SMEM((N_EXPERTS,), I32), pltpu.SMEM((N_EXPERTS,), I32),
                        pltpu.SMEM((n_rows,), I32), pltpu.SMEM((n_rows,), I32),
                        pltpu.SemaphoreType.DMA((2,))],
        name="route_tables",
    )(route_i[0], route_i[1], route_i[2], route_i[3], cnt, src0, dst0)


def _pad_lanes(w, offset=0):
    return jnp.pad(w, ((0, 0), (offset, LANES - offset - w.shape[1])))


def kernel(x, norm1_g, w_in, conv_w, conv_b, conv_ln_g, conv_ln_b, gla_w2, gla_b2, gla_norm_g, fox_f_b, w_out, norm2_g, router_group_w, router_group_b, router_expert_w, router_expert_b, ffn_w_gate, ffn_w_up, ffn_w_down, final_norm_g):
    B, S, D = x.shape
    T = B * S
    depth = w_in.shape[0]
    n_tiles = (2 * T) // MOE_TM + N_EXPERTS

    wg_all = ffn_w_gate.reshape(depth * N_EXPERTS, D_MODEL, D_EXPERT)
    wu_all = ffn_w_up.reshape(depth * N_EXPERTS, D_MODEL, D_EXPERT)
    wd_all = ffn_w_down.reshape(depth * N_EXPERTS, D_EXPERT, D_MODEL)
    w_proj, ws_hi, ws_lo = _prep_in_call(w_in)
    w_o = _prep_out_call(w_out)

    xt = x.reshape(T, D)
    h, small = _norm_call(xt, norm1_g[0][None, :], small_w=(ws_hi, ws_lo, 0))
    for l in range(depth):
        proj = _inproj_call(h, w_proj, l)

        y_conv = _conv_call(proj, jnp.pad(conv_w[l], ((0, CONV_PAD - CONV_WIDTH), (0, 0))), conv_b[l][None, :],
                            conv_ln_g[l][None, :], conv_ln_b[l][None, :], B, S)
        w2_hi, w2_lo = _split_bf16(jnp.pad(gla_w2[l], ((0, LANES - GLA_RANK), (0, 0))))
        y_gla = _gla_call(proj, small, w2_hi, w2_lo, gla_b2[l][None, :], gla_norm_g[l][None, :], B, S)
        fcol, frow = _fgate_call(small, _pad_lanes(fox_f_b[l][None, :], SMALL_FOX_LANE), B, S)
        y_fox = _fox_call(proj, fcol, frow, B, S)

        w_route = jnp.concatenate([router_group_w[l],
                                   router_expert_w[l].transpose(1, 0, 2).reshape(D_MODEL, N_EXPERTS)], axis=1)
        wr_hi, wr_lo = _split_bf16(_pad_lanes(w_route))
        b_route = _pad_lanes(jnp.concatenate([router_group_b[l], router_expert_b[l].reshape(-1)])[None, :])
        xt, h2, logits = _outproj_call(y_conv, y_gla, y_fox, xt, w_o, l, norm2_g[l][None, :],
                                       wr_hi, wr_lo, b_route)

        route_i, gates, counts = _router_call(logits)
        tables = _route_tables(route_i, counts, T, n_tiles)
        y2 = _moe_call(tables, h2, wg_all, wu_all, wd_all, l, n_tiles)

        if l + 1 < depth:
            xt, h, small = _norm_call(xt, norm1_g[l + 1][None, :], moe=(y2, gates), small_w=(ws_hi, ws_lo, l + 1))
        else:
            (out,) = _norm_call(xt, final_norm_g[None, :], moe=(y2, gates), out_dtype=F32)
    return out.reshape(B, S, D)
```

```python
import functools

import jax
import jax.numpy as jnp
from jax import lax
from jax.experimental import pallas as pl
from jax.experimental.pallas import tpu as pltpu

F32 = jnp.float32
BF16 = jnp.bfloat16
I32 = jnp.int32

D_MODEL = 2048
EPS = 1e-6
D_CONV = 512
CONV_WIDTH = 31
D_GLA = 1024
GLA_HEADS = 4
GLA_DK = 128
GLA_DV = 256
GLA_KEY = GLA_HEADS * GLA_DK
GLA_RANK = 16
GLA_GATE_NORMALIZER = 16.0
GLA_CHUNK = 64
D_FOX = 512
FOX_HEADS = 4
FOX_DH = 128
N_GROUPS = 4
EXPERTS_PER_GROUP = 8
N_EXPERTS = N_GROUPS * EXPERTS_PER_GROUP
D_EXPERT = 512

LANES = 128
SUBLANES = 8
D_MAIN = 2 * D_CONV + 2 * GLA_KEY + 2 * D_GLA
D_PROJ = D_MAIN + 3 * D_FOX
D_IN = D_MAIN + GLA_RANK + 3 * D_FOX + FOX_HEADS
PREP_N = 512
PREP_LAST = 24
SMALL_FOX_LANE = GLA_RANK
ROUTER_EXPERT_LANE = N_GROUPS

VMEM_LIMIT = 56 * 1024 * 1024

ROW_TILE = 512
MM_TM = 1024
MM_TN = 1408
GLA_TS = 256
FOX_TQ = 512
FOX_TK = 512
FOX_VT_BLK = 512
CONV_RC = 256
ROUTE_TR = 512
MOE_TM = 256
MOE_GROUP = SUBLANES
MOE_SLOTS = 3
MOE_DRAIN_STEPS = 2


def _cparams(sem):
    return pltpu.CompilerParams(dimension_semantics=sem, vmem_limit_bytes=VMEM_LIMIT)


def _split_bf16(x):
    hi = x.astype(BF16)
    lo = (x - hi.astype(F32)).astype(BF16)
    return hi, lo


def _dot(a, b):
    return jnp.dot(a, b, preferred_element_type=F32)


def _dot3(a, b_hi, b_lo):
    a_hi, a_lo = _split_bf16(a)
    return _dot(a_hi, b_hi) + _dot(a_lo, b_hi) + _dot(a_hi, b_lo)


def _sigmoid(x):
    return 1.0 / (1.0 + jnp.exp(-x))


def _log_sigmoid(x):
    return jnp.minimum(x, 0.0) - jnp.log(1.0 + jnp.exp(-jnp.abs(x)))


def _norm_body(*refs, combine, project):
    it = iter(refs)
    x_ref = next(it)
    if combine:
        ya_ref, yb_ref, gates_ref = next(it), next(it), next(it)
    g_ref = next(it)
    if project:
        ws_hi_ref, ws_lo_ref = next(it), next(it)
    if combine and project:
        xo_ref = next(it)
    h_ref = next(it)
    if project:
        small_ref = next(it)

    x = x_ref[...]
    if combine:
        gates = gates_ref[...]
        x = x + gates[:, 0:1] * ya_ref[...] + gates[:, 1:2] * yb_ref[...]
        if project:
            xo_ref[...] = x
    y = x * lax.rsqrt(jnp.mean(x * x, axis=-1, keepdims=True) + EPS) * g_ref[...]
    h_ref[...] = y.astype(h_ref.dtype)
    if project:
        y_hi, y_lo = _split_bf16(y)
        ws_hi = ws_hi_ref[...]
        small_ref[...] = _dot_nt(y_hi, ws_hi) + _dot_nt(y_lo, ws_hi) + _dot_nt(y_hi, ws_lo_ref[...])


def _norm_call(x, g_row, *, moe=None, small_w=None, out_dtype=BF16):
    T = x.shape[0]
    tm = ROW_TILE
    combine = moe is not None
    project = small_w is not None
    row_spec = pl.BlockSpec((tm, D_MODEL), lambda i: (i, 0))
    lane_spec = pl.BlockSpec((tm, LANES), lambda i: (i, 0))
    const = lambda shape: pl.BlockSpec(shape, lambda i: (0, 0))
    nblk = T // tm
    ins, in_specs = [x], [row_spec]
    if combine:
        y2, gates = moe
        ins += [y2, y2, gates]
        in_specs += [row_spec, pl.BlockSpec((tm, D_MODEL), lambda i: (i + nblk, 0)), lane_spec]
    ins.append(g_row)
    in_specs.append(const((1, D_MODEL)))
    if project:
        ws_hi, ws_lo, layer = small_w
        ins += [ws_hi, ws_lo]
        in_specs += [pl.BlockSpec((None, LANES, D_MODEL), lambda i: (layer, 0, 0))] * 2
    out_shape, out_specs = [], []
    if combine and project:
        out_shape.append(jax.ShapeDtypeStruct((T, D_MODEL), F32))
        out_specs.append(row_spec)
    out_shape.append(jax.ShapeDtypeStruct((T, D_MODEL), out_dtype))
    out_specs.append(row_spec)
    if project:
        out_shape.append(jax.ShapeDtypeStruct((T, LANES), F32))
        out_specs.append(lane_spec)
    return pl.pallas_call(
        functools.partial(_norm_body, combine=combine, project=project),
        grid=(nblk,),
        in_specs=in_specs,
        out_specs=out_specs,
        out_shape=out_shape,
        compiler_params=_cparams(("parallel",)),
        name="norm",
    )(*ins)


def _prep_in_body(w_hbm, wp_ref, ws_hi_ref, ws_lo_ref, buf, tail_ref, sem):
    j = pl.program_id(0)
    depth = buf.shape[1]
    n_main = D_MAIN // PREP_N
    n_last = D_IN // PREP_N
    keep = PREP_N - GLA_RANK
    slot = j % 2

    def full_block(jj, s, l):
        return pltpu.make_async_copy(w_hbm.at[pl.ds(jj * PREP_N, PREP_N), l, :], buf.at[s, l], sem.at[s])

    def last_block(s, l):
        return pltpu.make_async_copy(w_hbm.at[pl.ds(D_IN - PREP_LAST, PREP_LAST), l, :],
                                     buf.at[s, l, pl.ds(0, PREP_LAST)], sem.at[s])

    def for_block(jj, s, action):
        @pl.when(jj < n_last)
        def _():
            for l in range(depth):
                action(full_block(jj, s, l))

        @pl.when(jj == n_last)
        def _():
            for l in range(depth):
                action(last_block(s, l))

    @pl.when(j == 0)
    def _():
        for_block(j, slot, lambda c: c.start())
        ws_hi_ref[...] = jnp.zeros(ws_hi_ref.shape, BF16)
        ws_lo_ref[...] = jnp.zeros(ws_lo_ref.shape, BF16)

    @pl.when(j < n_last)
    def _():
        for_block(j + 1, 1 - slot, lambda c: c.start())

    for_block(j, slot, lambda c: c.wait())

    head0 = PREP_N * n_last - (D_IN - PREP_LAST)
    for l in range(depth):
        @pl.when(j < n_main)
        def _(l=l):
            wp_ref[l] = buf[slot, l].astype(BF16)

        @pl.when(j == n_main)
        def _(l=l):
            hi, lo = _split_bf16(buf[slot, l, 0:GLA_RANK, :])
            ws_hi_ref[l, 0:GLA_RANK, :] = hi
            ws_lo_ref[l, 0:GLA_RANK, :] = lo

        @pl.when(j > n_main)
        def _(l=l):
            wp_ref[l, 0:keep, :] = tail_ref[l, 0:keep, :]

        @pl.when((j > n_main) & (j < n_last))
        def _(l=l):
            wp_ref[l, keep:PREP_N, :] = buf[slot, l, 0:GLA_RANK, :].astype(BF16)

        @pl.when((j >= n_main) & (j < n_last))
        def _(l=l):
            tail_ref[l, 0:keep, :] = buf[slot, l, GLA_RANK:PREP_N, :].astype(BF16)

        @pl.when(j == n_last)
        def _(l=l):
            w = buf[slot, l, 0:2 * PREP_LAST, :]
            wp_ref[l, keep:PREP_N, :] = w[head0:head0 + GLA_RANK].astype(BF16)
            row = lax.broadcasted_iota(I32, (GLA_RANK, D_MODEL), 0)
            logits = jnp.where(row < FOX_HEADS, w[head0 + GLA_RANK:head0 + 2 * GLA_RANK], 0.0)
            hi, lo = _split_bf16(logits)
            ws_hi_ref[l, GLA_RANK:2 * GLA_RANK, :] = hi
            ws_lo_ref[l, GLA_RANK:2 * GLA_RANK, :] = lo


def _prep_in_call(w_in):
    depth = w_in.shape[0]
    n_main = D_MAIN // PREP_N
    w_t = jnp.transpose(w_in, (2, 0, 1))
    out_block = lambda j: (0, jnp.where(j <= n_main, jnp.minimum(j, n_main - 1), j - 1), 0)
    const = lambda j: (0, 0, 0)
    return pl.pallas_call(
        _prep_in_body,
        grid=(D_IN // PREP_N + 1,),
        in_specs=[pl.BlockSpec(memory_space=pl.ANY)],
        out_specs=[pl.BlockSpec((depth, PREP_N, D_MODEL), out_block),
                   pl.BlockSpec((depth, LANES, D_MODEL), const),
                   pl.BlockSpec((depth, LANES, D_MODEL), const)],
        out_shape=[jax.ShapeDtypeStruct((depth, D_PROJ, D_MODEL), BF16),
                   jax.ShapeDtypeStruct((depth, LANES, D_MODEL), BF16),
                   jax.ShapeDtypeStruct((depth, LANES, D_MODEL), BF16)],
        scratch_shapes=[pltpu.VMEM((2, depth, PREP_N, D_MODEL), F32),
                        pltpu.VMEM((depth, PREP_N, D_MODEL), BF16),
                        pltpu.SemaphoreType.DMA((2,))],
        compiler_params=_cparams(("arbitrary",)),
        name="weight_prep_in",
    )(w_t)


def _prep_out_body(w_ref, o_ref):
    o_ref[...] = w_ref[...].astype(BF16)


def _prep_out_call(w_out):
    depth = w_out.shape[0]
    spec = pl.BlockSpec((None, PREP_N, D_MODEL), lambda l, i: (l, i, 0))
    return pl.pallas_call(
        _prep_out_body,
        grid=(depth, D_MODEL // PREP_N),
        in_specs=[spec],
        out_specs=spec,
        out_shape=jax.ShapeDtypeStruct((depth, D_MODEL, D_MODEL), BF16),
        compiler_params=_cparams(("parallel", "parallel")),
        name="weight_prep_out",
    )(w_out)


def _dot_nt(a, b_t):
    return lax.dot_general(a, b_t, (((1,), (1,)), ((), ())), preferred_element_type=F32)


def _matmul_body(h_ref, w_ref, o_ref):
    o_ref[...] = _dot_nt(h_ref[...], w_ref[...]).astype(o_ref.dtype)


def _inproj_call(h, w_all, layer):
    T = h.shape[0]
    tm = min(MM_TM, T)
    return pl.pallas_call(
        _matmul_body,
        grid=(T // tm, D_PROJ // MM_TN),
        in_specs=[pl.BlockSpec((tm, D_MODEL), lambda i, j: (i, 0)),
                  pl.BlockSpec((None, MM_TN, D_MODEL), lambda i, j: (layer, j, 0))],
        out_specs=pl.BlockSpec((tm, MM_TN), lambda i, j: (i, j)),
        out_shape=jax.ShapeDtypeStruct((T, D_PROJ), BF16),
        compiler_params=_cparams(("parallel", "parallel")),
        name="inproj",
    )(h, w_all)


CONV_PAD = 32


def _conv_body(a_ref, g_ref, w_ref, b_ref, lng_ref, lnb_ref, o_ref, u_ref, sh_ref):
    S = a_ref.shape[0]
    u_ref[0:CONV_PAD, :] = jnp.zeros((CONV_PAD, D_CONV), F32)
    u_ref[CONV_PAD:CONV_PAD + S, :] = a_ref[...].astype(F32) * _sigmoid(g_ref[...].astype(F32))
    bias = b_ref[...]
    lng = lng_ref[...]
    lnb = lnb_ref[...]
    first = CONV_PAD - (CONV_WIDTH - 1)

    def chunk(c, carry):
        r0 = pl.multiple_of(c * CONV_RC, CONV_RC)
        acc = jnp.broadcast_to(bias, (CONV_RC, D_CONV))
        win = u_ref[pl.ds(r0, CONV_RC + CONV_PAD), :]
        for s in range(1, SUBLANES):
            sh_ref[s - 1] = win[s:s + CONV_RC + CONV_PAD - SUBLANES, :]
        for j in range(CONV_WIDTH):
            s = (first + j) % SUBLANES
            a = first + j - s
            tap = win[a:a + CONV_RC, :] if s == 0 else sh_ref[s - 1, a:a + CONV_RC, :]
            acc = acc + w_ref[j:j + 1, :] * tap
        mu = jnp.mean(acc, axis=-1, keepdims=True)
        d = acc - mu
        var = jnp.mean(d * d, axis=-1, keepdims=True)
        yn = d * lax.rsqrt(var + EPS) * lng + lnb
        o_ref[pl.ds(r0, CONV_RC), :] = (yn * _sigmoid(yn)).astype(o_ref.dtype)
        return carry

    lax.fori_loop(0, S // CONV_RC, chunk, 0)


def _conv_call(proj, w_pad, b_row, lng_row, lnb_row, B, S):
    T = B * S
    const = lambda shape: pl.BlockSpec(shape, lambda b: (0, 0))
    return pl.pallas_call(
        _conv_body,
        grid=(B,),
        in_specs=[pl.BlockSpec((S, D_CONV), lambda b: (b, 0)),
                  pl.BlockSpec((S, D_CONV), lambda b: (b, 1)),
                  const((CONV_PAD, D_CONV)), const((1, D_CONV)), const((1, D_CONV)), const((1, D_CONV))],
        out_specs=pl.BlockSpec((S, D_CONV), lambda b: (b, 0)),
        out_shape=jax.ShapeDtypeStruct((T, D_CONV), BF16),
        scratch_shapes=[pltpu.VMEM((CONV_PAD + S, D_CONV), F32),
                        pltpu.VMEM((SUBLANES - 1, CONV_RC + CONV_PAD - SUBLANES, D_CONV), F32)],
        compiler_params=_cparams(("parallel",)),
        name="conv_mixer",
    )(proj, proj, w_pad, b_row, lng_row, lnb_row)


def _gla_body(q_ref, k_ref, v_ref, g_ref, low_ref, w2hi_ref, w2lo_ref, b2_ref, ng_ref, o_ref, st_ref):
    ts = q_ref.shape[0]
    nchunk = ts // GLA_CHUNK

    @pl.when(pl.program_id(1) == 0)
    def _():
        st_ref[...] = jnp.zeros(st_ref.shape, F32)

    la = _log_sigmoid(_dot3(low_ref[...], w2hi_ref[...], w2lo_ref[...]) + b2_ref[...]) * (1.0 / GLA_GATE_NORMALIZER)
    r = lax.broadcasted_iota(I32, (2 * ts, ts), 0)
    c = lax.broadcasted_iota(I32, (2 * ts, ts), 1)
    rr = jnp.where(r >= ts, r - ts, r)
    same_chunk = (rr // GLA_CHUNK) == (c // GLA_CHUNK)
    sel = jnp.where(same_chunk & ((r >= ts) | (c <= rr)), 1.0, 0.0).astype(BF16)
    la_hi, la_lo = _split_bf16(la)
    sums = _dot(sel, la_hi) + _dot(sel, la_lo)
    cum = sums[0:ts, :]
    last = sums[ts:2 * ts, :]
    e_q = jnp.exp(cum)
    e_inv = jnp.exp(-cum)
    e_end = jnp.exp(last - cum)
    e_last = jnp.exp(last)

    qr = lax.broadcasted_iota(I32, (ts, ts), 0)
    qc = lax.broadcasted_iota(I32, (ts, ts), 1)
    att_mask = ((qr // GLA_CHUNK) == (qc // GLA_CHUNK)) & (qc <= qr)
    ng = ng_ref[...]

    for h in range(GLA_HEADS):
        ks = slice(h * GLA_DK, (h + 1) * GLA_DK)
        vs = slice(h * GLA_DV, (h + 1) * GLA_DV)
        qh = q_ref[:, ks].astype(F32) * (GLA_DK ** -0.5)
        kh = k_ref[:, ks].astype(F32)
        vh = v_ref[:, vs]
        q_dec = (qh * e_q[:, ks]).astype(BF16)
        k_inv = (kh * e_inv[:, ks]).astype(BF16)
        k_end = (kh * e_end[:, ks]).astype(BF16)
        att = lax.dot_general(q_dec, k_inv, (((1,), (1,)), ((), ())), preferred_element_type=F32)
        att = jnp.where(att_mask, att, 0.0).astype(BF16)
        o_intra = _dot(att, vh)
        state = st_ref[h]
        outs = []
        for n in range(nchunk):
            rs = slice(n * GLA_CHUNK, (n + 1) * GLA_CHUNK)
            inter = lax.dot_general(q_dec[rs], state.astype(BF16), (((1,), (1,)), ((), ())),
                                    preferred_element_type=F32)
            outs.append(o_intra[rs] + inter)
            kv_t = lax.dot_general(vh[rs], k_end[rs], (((0,), (0,)), ((), ())), preferred_element_type=F32)
            state = state * e_last[n * GLA_CHUNK:n * GLA_CHUNK + 1, ks] + kv_t
        st_ref[h] = state
        o = jnp.concatenate(outs, axis=0)
        o = o * lax.rsqrt(jnp.mean(o * o, axis=-1, keepdims=True) + EPS) * ng
        gate = g_ref[:, vs].astype(F32)
        o_ref[:, vs] = (o * (gate * _sigmoid(gate))).astype(o_ref.dtype)


def _gla_call(proj, small, w2_hi, w2_lo, b2_row, ng_row, B, S):
    T = B * S
    ts = GLA_TS
    nst = S // ts
    row = lambda b, s: b * nst + s
    const = lambda shape: pl.BlockSpec(shape, lambda b, s: (0, 0))
    return pl.pallas_call(
        _gla_body,
        grid=(B, nst),
        in_specs=[pl.BlockSpec((ts, GLA_KEY), lambda b, s: (row(b, s), 2)),
                  pl.BlockSpec((ts, GLA_KEY), lambda b, s: (row(b, s), 3)),
                  pl.BlockSpec((ts, D_GLA), lambda b, s: (row(b, s), 2)),
                  pl.BlockSpec((ts, D_GLA), lambda b, s: (row(b, s), 3)),
                  pl.BlockSpec((ts, LANES), lambda b, s: (row(b, s), 0)),
                  const((LANES, GLA_KEY)), const((LANES, GLA_KEY)), const((1, GLA_KEY)), const((1, GLA_DV))],
        out_specs=pl.BlockSpec((ts, D_GLA), lambda b, s: (row(b, s), 0)),
        out_shape=jax.ShapeDtypeStruct((T, D_GLA), BF16),
        scratch_shapes=[pltpu.VMEM((GLA_HEADS, GLA_DV, GLA_DK), F32)],
        compiler_params=_cparams(("parallel", "arbitrary")),
        name="gla_mixer",
    )(proj, proj, proj, proj, small, w2_hi, w2_lo, b2_row, ng_row)


FGATE_BLK = 256


def _fgate_body(small_ref, fb_ref, fcol_ref, frow_ref):
    S = small_ref.shape[0]
    r = lax.broadcasted_iota(I32, (FGATE_BLK, FGATE_BLK), 0)
    c = lax.broadcasted_iota(I32, (FGATE_BLK, FGATE_BLK), 1)
    tri = jnp.where(c <= r, 1.0, 0.0).astype(BF16)
    carry = jnp.zeros((1, LANES), F32)
    for n in range(S // FGATE_BLK):
        rs = slice(n * FGATE_BLK, (n + 1) * FGATE_BLK)
        lf = _log_sigmoid(small_ref[rs, :] + fb_ref[...])
        p0 = lf.astype(BF16)
        r1 = lf - p0.astype(F32)
        p1 = r1.astype(BF16)
        p2 = (r1 - p1.astype(F32)).astype(BF16)
        blk = _dot(tri, p0) + _dot(tri, p1) + _dot(tri, p2) + carry
        fcol_ref[rs, :] = blk
        carry = blk[FGATE_BLK - 1:FGATE_BLK, :]
    ft = fcol_ref[...].T
    for h in range(FOX_HEADS):
        frow_ref[0, h] = ft[SMALL_FOX_LANE + h:SMALL_FOX_LANE + h + 1, :]


def _fgate_call(small, fb_row, B, S):
    T = B * S
    return pl.pallas_call(
        _fgate_body,
        grid=(B,),
        in_specs=[pl.BlockSpec((S, LANES), lambda b: (b, 0)),
                  pl.BlockSpec((1, LANES), lambda b: (0, 0))],
        out_specs=[pl.BlockSpec((S, LANES), lambda b: (b, 0)),
                   pl.BlockSpec((1, FOX_HEADS, 1, S), lambda b: (b, 0, 0, 0))],
        out_shape=[jax.ShapeDtypeStruct((T, LANES), F32),
                   jax.ShapeDtypeStruct((B, FOX_HEADS, 1, S), F32)],
        compiler_params=_cparams(("parallel",)),
        name="fox_gate",
    )(small, fb_row)


def _fox_body(q_ref, k_ref, v_ref, fcol_ref, frow_ref, o_ref, vt_ref, fb_ref, acc_ref):
    tq = q_ref.shape[0]
    tk = FOX_TK
    S = k_ref.shape[0]
    i = pl.program_id(1)

    @pl.when(i == 0)
    def _():
        for c in range(S // FOX_VT_BLK):
            cs = slice(c * FOX_VT_BLK, (c + 1) * FOX_VT_BLK)
            vt_ref[:, cs] = v_ref[cs, :].astype(F32).T.astype(BF16)
        for h in range(FOX_HEADS):
            fb_ref[h] = jnp.broadcast_to(fcol_ref[:, SMALL_FOX_LANE + h:SMALL_FOX_LANE + h + 1], (S, LANES))

    q0 = pl.multiple_of(i * tq, tq)
    key = lax.broadcasted_iota(I32, (tk, tq), 0)
    qry = lax.broadcasted_iota(I32, (tk, tq), 1)
    n_diag = tq // tk
    n_full = i * n_diag

    heads = []
    for h in range(FOX_HEADS):
        hs = slice(h * FOX_DH, (h + 1) * FOX_DH)
        qh = (q_ref[:, hs].astype(F32) * (FOX_DH ** -0.5)).astype(BF16)
        f_t = frow_ref[0, h, :, pl.ds(q0, tq)]
        heads.append((hs, qh, f_t))

    def update(j, states, diag=None):
        k0 = pl.multiple_of(j * tk, tk)
        zs = []
        for hs, qh, f_t in heads:
            kt = k_ref[pl.ds(k0, tk), hs]
            zs.append(lax.dot_general(kt, qh, (((1,), (1,)), ((), ())), preferred_element_type=F32))
        ps, alphas, new_states = [], [], []
        for h, (hs, qh, f_t) in enumerate(heads):
            m, l = states[h]
            f_s = fb_ref[h, pl.ds(k0, tk), :]
            z = zs[h] - jnp.concatenate([f_s] * (tq // LANES), axis=1)
            if diag is not None:
                z = jnp.where(key + diag * tk <= qry, z, -jnp.inf)
            m_new = jnp.maximum(m, jnp.max(z, axis=0, keepdims=True) + f_t)
            p = jnp.exp(z + (f_t - m_new))
            alpha = jnp.exp(m - m_new)
            new_states.append((m_new, alpha * l + jnp.sum(p, axis=0, keepdims=True)))
            ps.append(p.astype(BF16))
            alphas.append(alpha)
        for h, (hs, qh, f_t) in enumerate(heads):
            pv = _dot(vt_ref[hs, pl.ds(k0, tk)], ps[h])
            acc_ref[h] = alphas[h] * acc_ref[h] + pv
        return tuple(new_states)

    acc_ref[...] = jnp.zeros(acc_ref.shape, F32)
    states = ((jnp.full((1, tq), -jnp.inf, F32), jnp.zeros((1, tq), F32)),) * FOX_HEADS
    for d in range(n_diag):
        states = update(n_full + d, states, diag=d)

    states = lax.fori_loop(0, n_full, update, states)
    for h in range(FOX_HEADS):
        m, l = states[h]
        o_ref[:, heads[h][0]] = (acc_ref[h] / l).T.astype(o_ref.dtype)


def _fox_call(proj, fcol, frow, B, S):
    T = B * S
    tq = FOX_TQ
    nq = S // tq
    col0 = D_MAIN // D_FOX
    return pl.pallas_call(
        _fox_body,
        grid=(B, nq),
        in_specs=[pl.BlockSpec((tq, D_FOX), lambda b, i: (b * nq + i, col0)),
                  pl.BlockSpec((S, D_FOX), lambda b, i: (b, col0 + 1)),
                  pl.BlockSpec((S, D_FOX), lambda b, i: (b, col0 + 2)),
                  pl.BlockSpec((S, LANES), lambda b, i: (b, 0)),
                  pl.BlockSpec((1, FOX_HEADS, 1, S), lambda b, i: (b, 0, 0, 0))],
        out_specs=pl.BlockSpec((tq, D_FOX), lambda b, i: (b * nq + i, 0)),
        out_shape=jax.ShapeDtypeStruct((T, D_FOX), BF16),
        scratch_shapes=[pltpu.VMEM((D_FOX, S), BF16),
                        pltpu.VMEM((FOX_HEADS, S, LANES), F32),
                        pltpu.VMEM((FOX_HEADS, FOX_DH, tq), F32)],
        compiler_params=_cparams(("parallel", "arbitrary")),
        name="fox_mixer",
    )(proj, proj, proj, fcol, frow)


def _outproj_body(yc_ref, yg_ref, yf_ref, x_ref, w_ref, g_ref, wr_ref, br_ref,
                  xo_ref, h_ref, lg_ref):
    acc = x_ref[...]
    acc = acc + _dot(yc_ref[...], w_ref[0:D_CONV, :])
    acc = acc + _dot(yg_ref[...], w_ref[D_CONV:D_CONV + D_GLA, :])
    acc = acc + _dot(yf_ref[...], w_ref[D_CONV + D_GLA:D_MODEL, :])
    xo_ref[...] = acc
    hn = acc * lax.rsqrt(jnp.mean(acc * acc, axis=-1, keepdims=True) + EPS) * g_ref[...]
    h_ref[...] = hn
    hn_hi, hn_lo = _split_bf16(hn)
    both = _dot(hn_hi, wr_ref[...])
    lg_ref[...] = both[:, 0:LANES] + both[:, LANES:2 * LANES] + _dot(hn_lo, wr_ref[:, 0:LANES]) + br_ref[...]


def _outproj_call(yc, yg, yf, x, w_all, layer, g_row, wr_hi, wr_lo, br_row):
    T = x.shape[0]
    tm = ROW_TILE
    const = lambda shape: pl.BlockSpec(shape, lambda i: (0, 0))
    rows = lambda width: pl.BlockSpec((tm, width), lambda i: (i, 0))
    return pl.pallas_call(
        _outproj_body,
        grid=(T // tm,),
        in_specs=[rows(D_CONV), rows(D_GLA), rows(D_FOX), rows(D_MODEL),
                  pl.BlockSpec((None, D_MODEL, D_MODEL), lambda i: (layer, 0, 0)), const((1, D_MODEL)),
                  const((D_MODEL, 2 * LANES)), const((1, LANES))],
        out_specs=[rows(D_MODEL), rows(D_MODEL), rows(LANES)],
        out_shape=[jax.ShapeDtypeStruct((T, D_MODEL), F32),
                   jax.ShapeDtypeStruct((T, D_MODEL), F32),
                   jax.ShapeDtypeStruct((T, LANES), F32)],
        compiler_params=_cparams(("parallel",)),
        name="outproj",
    )(yc, yg, yf, x, w_all, g_row, jnp.concatenate([wr_hi, wr_lo], axis=1), br_row)


def _router_body(lg_ref, ri_ref, rf_ref, cnt_ref, carry_ref):
    tr = lg_ref.shape[0]

    @pl.when(pl.program_id(0) == 0)
    def _():
        carry_ref[...] = jnp.zeros(carry_ref.shape, F32)

    lg = lg_ref[...]
    lane = lax.broadcasted_iota(I32, (tr, LANES), 1).astype(F32)
    big = float(LANES)
    neg = -jnp.inf

    is_g = lane < N_GROUPS
    gl = jnp.where(is_g, lg, neg)
    gmax = jnp.max(gl, axis=-1, keepdims=True)
    gexp = jnp.where(is_g, jnp.exp(lg - gmax), 0.0)
    gprob = gexp / jnp.sum(gexp, axis=-1, keepdims=True)
    gtop = jnp.max(gprob, axis=-1, keepdims=True)
    grp = jnp.min(jnp.where(is_g & (gprob == gtop), lane, big), axis=-1, keepdims=True)

    lo = ROUTER_EXPERT_LANE + grp * EXPERTS_PER_GROUP
    in_grp = (lane >= lo) & (lane < lo + EXPERTS_PER_GROUP)
    el = jnp.where(in_grp, lg, neg)
    v1 = jnp.max(el, axis=-1, keepdims=True)
    i1 = jnp.min(jnp.where(in_grp & (el == v1), lane, big), axis=-1, keepdims=True)
    rest = in_grp & (lane != i1)
    el2 = jnp.where(rest, lg, neg)
    v2 = jnp.max(el2, axis=-1, keepdims=True)
    i2 = jnp.min(jnp.where(rest & (el2 == v2), lane, big), axis=-1, keepdims=True)
    ex = jnp.exp(v2 - v1)
    p1 = 1.0 / (1.0 + ex)
    p2 = ex / (1.0 + ex)

    hit1 = lane == i1
    hit2 = lane == i2
    onehot = jnp.where(hit1 | hit2, 1.0, 0.0)
    r = lax.broadcasted_iota(I32, (tr, tr), 0)
    c = lax.broadcasted_iota(I32, (tr, tr), 1)
    strict = jnp.where(c < r, 1.0, 0.0).astype(BF16)
    before = _dot(strict, onehot.astype(BF16)) + carry_ref[...]
    rank1 = jnp.sum(jnp.where(hit1, before, 0.0), axis=-1, keepdims=True)
    rank2 = jnp.sum(jnp.where(hit2, before, 0.0), axis=-1, keepdims=True)
    carry_ref[...] = carry_ref[...] + jnp.sum(onehot, axis=0, keepdims=True)
    cnt_ref[...] = carry_ref[...]

    e1 = i1 - ROUTER_EXPERT_LANE
    e2 = i2 - ROUTER_EXPERT_LANE
    ri = jnp.where(lane == 0, e1, jnp.where(lane == 1, e2, jnp.where(lane == 2, rank1, jnp.where(lane == 3, rank2, 0.0))))
    ri_ref[...] = ri.T[0:SUBLANES, :].astype(I32)
    rf_ref[...] = jnp.where(lane == 0, gtop * p1, jnp.where(lane == 1, gtop * p2, 0.0))


def _router_call(logits):
    T = logits.shape[0]
    tr = ROUTE_TR
    rows = pl.BlockSpec((tr, LANES), lambda i: (i, 0))
    return pl.pallas_call(
        _router_body,
        grid=(T // tr,),
        in_specs=[rows],
        out_specs=[pl.BlockSpec((SUBLANES, tr), lambda i: (0, i)), rows, pl.BlockSpec((1, LANES), lambda i: (0, 0))],
        out_shape=[jax.ShapeDtypeStruct((SUBLANES, T), I32),
                   jax.ShapeDtypeStruct((T, LANES), F32),
                   jax.ShapeDtypeStruct((1, LANES), F32)],
        scratch_shapes=[pltpu.VMEM((1, LANES), F32)],
        compiler_params=_cparams(("arbitrary",)),
        name="router",
    )(logits)


def _moe_body(n_act_ref, tile_e_ref, tile_n_ref, first_ref, wslot_ref, next_e_ref, src_ref, dst_ref,
              h_hbm, wg_hbm, wu_hbm, wd_hbm, y_hbm,
              xbuf, ybuf, xs, wg_f, wu_f, wd_f, wg_b, wu_b, wd_b, gsem, ssem, wsem, *, expert0):
    k = pl.program_id(0)
    nv = tile_n_ref[k]
    n_act = n_act_ref[0]

    def slot_of(tile):
        return lax.rem(tile + MOE_SLOTS, MOE_SLOTS)

    def rows_moved(tile):
        nv_t = tile_n_ref[jnp.maximum(tile, 0)]
        nv_t = jnp.where(tile < 0, MOE_TM, nv_t)
        return pl.multiple_of(((nv_t + MOE_GROUP - 1) // MOE_GROUP) * MOE_GROUP, MOE_GROUP)

    def weight_copies(e, ws):
        return (pltpu.make_async_copy(wg_hbm.at[expert0 + e], wg_f.at[ws], wsem.at[ws]),
                pltpu.make_async_copy(wu_hbm.at[expert0 + e], wu_f.at[ws], wsem.at[ws]),
                pltpu.make_async_copy(wd_hbm.at[expert0 + e], wd_f.at[ws], wsem.at[ws]))

    def gather_row(tile, r, buf):
        return pltpu.make_async_copy(h_hbm.at[pl.ds(src_ref[tile * MOE_TM + r], 1)],
                                     xbuf.at[buf, pl.ds(r, 1)], gsem.at[buf])

    def scatter_row(tile, r, buf):
        return pltpu.make_async_copy(ybuf.at[buf, pl.ds(r, 1)],
                                     y_hbm.at[pl.ds(dst_ref[(tile + 1) * MOE_TM + r], 1)], ssem.at[buf])

    def wait_gather(tile, buf):
        n = rows_moved(tile)

        @pl.when(n > 0)
        def _():
            pltpu.make_async_copy(h_hbm.at[pl.ds(0, n)], xbuf.at[buf, pl.ds(0, n)], gsem.at[buf]).wait()

    def wait_scatter(tile, buf):
        n = rows_moved(tile)

        @pl.when(n > 0)
        def _():
            pltpu.make_async_copy(ybuf.at[buf, pl.ds(0, n)], y_hbm.at[pl.ds(0, n)], ssem.at[buf]).wait()

    def issue_loop(make_copy, tile, buf):
        def issue(r, carry):
            make_copy(tile, r, buf).start()
            return carry

        lax.fori_loop(0, rows_moved(tile), issue, 0)

    @pl.when(k == 0)
    def _():
        xbuf[...] = jnp.zeros(xbuf.shape, F32)
        ybuf[MOE_SLOTS - 1] = jnp.zeros((MOE_TM, D_MODEL), F32)
        for c in weight_copies(tile_e_ref[0], 0):
            c.start()
        issue_loop(gather_row, 0, 0)
        issue_loop(gather_row, 1, 1)

    @pl.when(nv > 0)
    def _():
        @pl.when(first_ref[k] == 1)
        def _():
            ws = wslot_ref[k]
            for c in weight_copies(tile_e_ref[k], ws):
                c.wait()
            nxt = next_e_ref[k]

            @pl.when(nxt >= 0)
            def _():
                for c in weight_copies(nxt, 1 - ws):
                    c.start(priority=1)

            wg_b[...] = wg_f[ws].astype(BF16)
            wu_b[...] = wu_f[ws].astype(BF16)
            wd_b[...] = wd_f[ws].astype(BF16)

        def tile_step(cur):
            prv = (cur + MOE_SLOTS - 1) % MOE_SLOTS
            wait_gather(k, cur)
            xs[...] = xbuf[cur].astype(BF16)
            n_in = rows_moved(k + 2)
            n_out = rows_moved(k - 1)
            for g in range(0, MOE_TM, MOE_GROUP):
                toks = [src_ref[(k + 2) * MOE_TM + r] for r in range(g, g + MOE_GROUP)]

                @pl.when(g < n_in)
                def _(g=g, toks=toks):
                    for r, tok in zip(range(g, g + MOE_GROUP), toks):
                        pltpu.make_async_copy(h_hbm.at[pl.ds(tok, 1)], xbuf.at[prv, pl.ds(r, 1)],
                                              gsem.at[prv]).start(priority=0)
            for g in range(0, MOE_TM, MOE_GROUP):
                outs = [dst_ref[k * MOE_TM + r] for r in range(g, g + MOE_GROUP)]

                @pl.when(g < n_out)
                def _(g=g, outs=outs):
                    for r, out in zip(range(g, g + MOE_GROUP), outs):
                        pltpu.make_async_copy(ybuf.at[prv, pl.ds(r, 1)], y_hbm.at[pl.ds(out, 1)],
                                              ssem.at[prv]).start(priority=1)
            xb = xs[...]
            gate = _dot(xb, wg_b[...])
            up = _dot(xb, wu_b[...])
            mid = (gate * _sigmoid(gate) * up).astype(BF16)
            ybuf[cur] = _dot(mid, wd_b[...])

        for s in range(MOE_SLOTS):
            pl.when(slot_of(k) == s)(functools.partial(tile_step, s))

        @pl.when(k > 0)
        def _():
            wait_scatter(k - 2, slot_of(k - 2))

    @pl.when(k == n_act)
    def _():
        issue_loop(scatter_row, k - 1, slot_of(k - 1))
        wait_scatter(k - 2, slot_of(k - 2))

    @pl.when(k == n_act + 1)
    def _():
        wait_scatter(k - 2, slot_of(k - 2))


def _moe_call(tables, h, wg, wu, wd, layer, n_tiles):
    T = h.shape[0]
    any_spec = pl.BlockSpec(memory_space=pl.ANY)
    grid_spec = pltpu.PrefetchScalarGridSpec(
        num_scalar_prefetch=len(tables),
        grid=(n_tiles + MOE_DRAIN_STEPS,),
        in_specs=[any_spec, any_spec, any_spec, any_spec],
        out_specs=any_spec,
        scratch_shapes=[pltpu.VMEM((MOE_SLOTS, MOE_TM, D_MODEL), F32),
                        pltpu.VMEM((MOE_SLOTS, MOE_TM, D_MODEL), F32),
                        pltpu.VMEM((MOE_TM, D_MODEL), BF16),
                        pltpu.VMEM((2, D_MODEL, D_EXPERT), F32),
                        pltpu.VMEM((2, D_MODEL, D_EXPERT), F32),
                        pltpu.VMEM((2, D_EXPERT, D_MODEL), F32),
                        pltpu.VMEM((D_MODEL, D_EXPERT), BF16),
                        pltpu.VMEM((D_MODEL, D_EXPERT), BF16),
                        pltpu.VMEM((D_EXPERT, D_MODEL), BF16),
                        pltpu.SemaphoreType.DMA((MOE_SLOTS,)),
                        pltpu.SemaphoreType.DMA((MOE_SLOTS,)),
                        pltpu.SemaphoreType.DMA((2,))],
    )
    return pl.pallas_call(
        functools.partial(_moe_body, expert0=layer * N_EXPERTS),
        grid_spec=grid_spec,
        out_shape=jax.ShapeDtypeStruct((2 * T + MOE_TM, D_MODEL), F32),
        compiler_params=_cparams(("arbitrary",)),
        name="moe_experts",
    )(*tables, h, wg, wu, wd)


def _tables_body(e1_ref, e2_ref, r1_ref, r2_ref, cnt_ref, src0_hbm, dst0_hbm,
                 n_act_ref, tile_e_ref, tile_n_ref, first_ref, wslot_ref, next_e_ref, src_hbm, dst_hbm,
                 row0_ref, after_ref, src_ref, dst_ref, sem):
    T = e1_ref.shape[0]
    n_steps = tile_e_ref.shape[0]

    defaults = (pltpu.make_async_copy(src0_hbm, src_ref, sem.at[0]),
                pltpu.make_async_copy(dst0_hbm, dst_ref, sem.at[1]))
    for c in defaults:
        c.start()

    nxt = jnp.int32(-1)
    for e in reversed(range(N_EXPERTS)):
        after_ref[e] = nxt
        nxt = jnp.where(cnt_ref[e] > 0, jnp.int32(e), nxt)

    k = jnp.int32(0)
    order = jnp.int32(0)
    for e in range(N_EXPERTS):
        n = cnt_ref[e]
        nt = (n + (MOE_TM - 1)) // MOE_TM
        row0_ref[e] = k * MOE_TM

        def tile(i, carry, e=e, n=n, k=k, order=order):
            tile_e_ref[k + i] = e
            tile_n_ref[k + i] = jnp.minimum(n - i * MOE_TM, MOE_TM)
            first_ref[k + i] = (i == 0).astype(I32)
            wslot_ref[k + i] = order & 1
            next_e_ref[k + i] = after_ref[e]
            return carry

        lax.fori_loop(0, nt, tile, 0)
        k = k + nt
        order = order + (nt > 0).astype(I32)
    n_act_ref[0] = k

    def idle(i, carry):
        tile_e_ref[i] = 0
        tile_n_ref[i] = 0
        first_ref[i] = 0
        wslot_ref[i] = 0
        next_e_ref[i] = -1
        return carry

    lax.fori_loop(k, n_steps, idle, 0)

    for c in defaults:
        c.wait()

    def assign(t, carry):
        p1 = row0_ref[e1_ref[t]] + r1_ref[t]
        p2 = row0_ref[e2_ref[t]] + r2_ref[t]
        src_ref[p1] = t
        src_ref[p2] = t
        dst_ref[p1 + MOE_TM] = t
        dst_ref[p2 + MOE_TM] = T + t
        return carry

    lax.fori_loop(0, T, assign, 0, unroll=8)

    results = (pltpu.make_async_copy(src_ref, src_hbm, sem.at[0]),
               pltpu.make_async_copy(dst_ref, dst_hbm, sem.at[1]))
    for c in results:
        c.start()
    for c in results:
        c.wait()


def _route_tables(route_i, counts, T, n_tiles):
    n_steps = n_tiles + MOE_DRAIN_STEPS
    n_rows = n_steps * MOE_TM
    cnt = counts[0, ROUTER_EXPERT_LANE:ROUTER_EXPERT_LANE + N_EXPERTS].astype(I32)
    src0 = jnp.zeros((n_rows,), I32)
    dst0 = 2 * T + jnp.arange(n_rows, dtype=I32) % MOE_TM
    smem = pl.BlockSpec(memory_space=pltpu.SMEM)
    hbm = pl.BlockSpec(memory_space=pl.ANY)
    vec = lambda n: jax.ShapeDtypeStruct((n,), I32)
    return pl.pallas_call(
        _tables_body,
        in_specs=[smem] * 5 + [hbm] * 2,
        out_specs=[smem] * 6 + [hbm] * 2,
        out_shape=[vec(1)] + [vec(n_steps)] * 5 + [vec(n_rows)] * 2,
        scratch_shapes=[pltpu.SMEM((N_EXPERTS,), I32), pltpu.SMEM((N_EXPERTS,), I32),
                        pltpu.SMEM((n_rows,), I32), pltpu.SMEM((n_rows,), I32),
                        pltpu.SemaphoreType.DMA((2,))],
        name="route_tables",
    )(route_i[0], route_i[1], route_i[2], route_i[3], cnt, src0, dst0)


def _pad_lanes(w, offset=0):
    return jnp.pad(w, ((0, 0), (offset, LANES - offset - w.shape[1])))


def kernel(x, norm1_g, w_in, conv_w, conv_b, conv_ln_g, conv_ln_b, gla_w2, gla_b2, gla_norm_g, fox_f_b, w_out, norm2_g, router_group_w, router_group_b, router_expert_w, router_expert_b, ffn_w_gate, ffn_w_up, ffn_w_down, final_norm_g):
    B, S, D = x.shape
    T = B * S
    depth = w_in.shape[0]
    n_tiles = (2 * T) // MOE_TM + N_EXPERTS

    wg_all = ffn_w_gate.reshape(depth * N_EXPERTS, D_MODEL, D_EXPERT)
    wu_all = ffn_w_up.reshape(depth * N_EXPERTS, D_MODEL, D_EXPERT)
    wd_all = ffn_w_down.reshape(depth * N_EXPERTS, D_EXPERT, D_MODEL)
    w_proj, ws_hi, ws_lo = _prep_in_call(w_in)
    w_o = _prep_out_call(w_out)

    xt = x.reshape(T, D)
    h, small = _norm_call(xt, norm1_g[0][None, :], small_w=(ws_hi, ws_lo, 0))
    for l in range(depth):
        proj = _inproj_call(h, w_proj, l)

        y_conv = _conv_call(proj, jnp.pad(conv_w[l], ((0, CONV_PAD - CONV_WIDTH), (0, 0))), conv_b[l][None, :],
                            conv_ln_g[l][None, :], conv_ln_b[l][None, :], B, S)
        w2_hi, w2_lo = _split_bf16(jnp.pad(gla_w2[l], ((0, LANES - GLA_RANK), (0, 0))))
        y_gla = _gla_call(proj, small, w2_hi, w2_lo, gla_b2[l][None, :], gla_norm_g[l][None, :], B, S)
        fcol, frow = _fgate_call(small, _pad_lanes(fox_f_b[l][None, :], SMALL_FOX_LANE), B, S)
        y_fox = _fox_call(proj, fcol, frow, B, S)

        w_route = jnp.concatenate([router_group_w[l],
                                   router_expert_w[l].transpose(1, 0, 2).reshape(D_MODEL, N_EXPERTS)], axis=1)
        wr_hi, wr_lo = _split_bf16(_pad_lanes(w_route))
        b_route = _pad_lanes(jnp.concatenate([router_group_b[l], router_expert_b[l].reshape(-1)])[None, :])
        xt, h2, logits = _outproj_call(y_conv, y_gla, y_fox, xt, w_o, l, norm2_g[l][None, :],
                                       wr_hi, wr_lo, b_route)

        route_i, gates, counts = _router_call(logits)
        tables = _route_tables(route_i, counts, T, n_tiles)
        y2 = _moe_call(tables, h2, wg_all, wu_all, wd_all, l, n_tiles)

        if l + 1 < depth:
            xt, h, small = _norm_call(xt, norm1_g[l + 1][None, :], moe=(y2, gates), small_w=(ws_hi, ws_lo, l + 1))
        else:
            (out,) = _norm_call(xt, final_norm_g[None, :], moe=(y2, gates), out_dtype=F32)
    return out.reshape(B, S, D)
```
